```python
import jax, jax.numpy as jnp
from jax import lax
import numpy as np

D_MODEL = 1024
BATCH = 8
SEQ = 2048
DEPTH = 1
DEC_BATCH = 128
DEC_SEQ = 1
PAST_LEN = 2048
PAGE_SIZE = 128

HEAD_DIM = 64
N_HEADS_A = D_MODEL // 128
N_KV_A = N_HEADS_A // 4
GROUP_A = N_HEADS_A // N_KV_A
CMP_BLK = 32
CMP_STRIDE = 16
SLC_BLK = 64
N_SEL = 16
WINDOW = 512
Q_BLOCK = 128
C_CONV = D_MODEL // 2
CONV_W = 31
N_MEM = 256
N_HEADS_M = 4
D_FF = ((8 * D_MODEL // 3 + 127) // 128) * 128
FFN_CONV_W = 3
ROPE_THETA = 500000.0
ROPE_DIM = HEAD_DIM // 4
EPS = 1e-6
BIG = 1e9
N_ROW_COMP = 4

Q_A_COLS = N_HEADS_A * HEAD_DIM
KV_A_COLS = 3 * 2 * N_KV_A * HEAD_DIM
GATE_A_COLS = 3 * N_HEADS_A
GLU_COLS = 2 * C_CONV
Q_M_COLS = N_HEADS_M * HEAD_DIM
MERGE_COLS = 3 * D_MODEL
IN_COLS = Q_A_COLS + KV_A_COLS + GATE_A_COLS + GLU_COLS + Q_M_COLS + MERGE_COLS

kernel_name = 'nsa_conformer_memory_hybrid_step'


def _rmsnorm(x, g):
    xf = x.astype(jnp.float32)
    y = xf * lax.rsqrt(jnp.mean(xf * xf, axis=-1, keepdims=True) + EPS)
    return (y * g.astype(jnp.float32)).astype(x.dtype)


def _layernorm(x, g, b):
    xf = x.astype(jnp.float32)
    mu = jnp.mean(xf, axis=-1, keepdims=True)
    var = jnp.mean(jnp.square(xf - mu), axis=-1, keepdims=True)
    y = (xf - mu) * lax.rsqrt(var + EPS)
    return (y * g.astype(jnp.float32) + b.astype(jnp.float32)).astype(x.dtype)


def _masked_softmax(s, mask):
    s = jnp.where(mask, s.astype(jnp.float32), -1e30)
    return jnp.where(mask, jax.nn.softmax(s, axis=-1), 0.0)


def _rope(x, pos):
    inv = ROPE_THETA ** (-jnp.arange(0, ROPE_DIM, 2, dtype=jnp.float32) / ROPE_DIM)
    ang = pos.astype(jnp.float32)[..., None] * inv
    cos, sin = jnp.cos(ang), jnp.sin(ang)
    xr = x[..., :ROPE_DIM].astype(jnp.float32)
    x1, x2 = xr[..., :ROPE_DIM // 2], xr[..., ROPE_DIM // 2:]
    rot = jnp.concatenate([x1 * cos - x2 * sin, x1 * sin + x2 * cos], axis=-1).astype(x.dtype)
    return jnp.concatenate([rot, x[..., ROPE_DIM:]], axis=-1)


def _causal_dwconv(x, buf, w, b):
    xp = jnp.concatenate([buf.astype(x.dtype), x], axis=1)
    out = lax.conv_general_dilated(xp, w[:, None, :].astype(x.dtype), window_strides=(1,), padding='VALID',
                                   dimension_numbers=('NWC', 'WIO', 'NWC'), feature_group_count=x.shape[-1])
    return out + b.astype(x.dtype), xp[:, -(w.shape[0] - 1):]


def _split_in(z):
    o1 = Q_A_COLS
    o2 = o1 + KV_A_COLS
    o3 = o2 + GATE_A_COLS
    o4 = o3 + GLU_COLS
    o5 = o4 + Q_M_COLS
    return z[..., :o1], z[..., o1:o2], z[..., o2:o3], z[..., o3:o4], z[..., o4:o5], z[..., o5:]


def _compress(rows, pe, w):
    n_cmp = (rows.shape[1] - CMP_BLK) // CMP_STRIDE + 1
    idx = jnp.arange(n_cmp)[:, None] * CMP_STRIDE + jnp.arange(CMP_BLK)[None, :]
    blk = rows[:, idx] + pe[:, None, :]
    return jnp.einsum('bnlgd,lde->bnge', blk, w)


def _slc_attend(q, qpos, idx, ok, k_blocks, v_blocks):
    B, KV, Tq, n = idx.shape
    gather = jax.vmap(jax.vmap(lambda kb, ix: kb[ix]))
    kg = gather(k_blocks, idx).reshape(B, KV, Tq, n * SLC_BLK, HEAD_DIM)
    vg = gather(v_blocks, idx).reshape(B, KV, Tq, n * SLC_BLK, HEAD_DIM)
    kpos = idx[..., None] * SLC_BLK + jnp.arange(SLC_BLK)
    mask = (ok[..., None] & (kpos <= qpos[:, None, None])).reshape(B, KV, Tq, n * SLC_BLK)
    s = jnp.einsum('btgjd,bgtmd->bgjtm', q, kg) * (HEAD_DIM ** -0.5)
    pr = _masked_softmax(s, mask[:, :, None])
    return jnp.einsum('bgjtm,bgtmd->btgjd', pr.astype(vg.dtype), vg)


def _nsa_cmp_slc(q, qpos, rows, p):
    B, L = rows.shape[0], rows.shape[1]
    Tq = q.shape[1]
    k_c = _compress(rows[:, :, 0], p['cmp_pe'][0], p['w_cmp'][0])
    v_c = _compress(rows[:, :, 1], p['cmp_pe'][1], p['w_cmp'][1])
    cmp_start = jnp.arange(k_c.shape[1]) * CMP_STRIDE
    cmp_end = cmp_start + (CMP_BLK - 1)
    k_c = _rope(_rmsnorm(k_c, p['k_norm'][0]), cmp_end[:, None])
    s = jnp.einsum('btgjd,bngd->bgjtn', q, k_c) * (HEAD_DIM ** -0.5)
    pc = _masked_softmax(s, cmp_end[None, :] <= qpos[:, None])
    o_cmp = jnp.einsum('bgjtn,bngd->btgjd', pc.astype(v_c.dtype), v_c)
    n_slc = -(-L // SLC_BLK)
    slc = jnp.arange(n_slc)
    slc_start = slc * SLC_BLK
    overlap = ((cmp_start[:, None] < slc_start[None, :] + SLC_BLK)
               & (cmp_start[:, None] + CMP_BLK > slc_start[None, :])).astype(jnp.float32)
    imp = jnp.einsum('bgjtn,ns->bgts', pc, overlap)
    q_blk = (qpos // SLC_BLK)[:, None]
    valid = slc[None, :] <= q_blk
    forced = (slc[None, :] == 0) | (slc[None, :] == q_blk) | (slc[None, :] == q_blk - 1)
    score = jnp.where(forced, BIG, jnp.where(valid, imp, -BIG))
    top_s, idx = lax.top_k(score, min(N_SEL, n_slc))
    ok = top_s > -0.5 * BIG
    pad = n_slc * SLC_BLK - L

    def to_blocks(r):
        r = jnp.pad(r, ((0, 0), (0, pad), (0, 0), (0, 0)))
        return r.reshape(B, n_slc, SLC_BLK, N_KV_A, HEAD_DIM).transpose(0, 3, 1, 2, 4)

    k_blocks = to_blocks(rows[:, :, 2])
    v_blocks = to_blocks(rows[:, :, 3])
    if Tq % Q_BLOCK == 0:
        nqb = Tq // Q_BLOCK
        n = idx.shape[-1]
        qb = q.reshape(B, nqb, Q_BLOCK, N_KV_A, GROUP_A, HEAD_DIM).swapaxes(0, 1)
        pb = qpos.reshape(nqb, Q_BLOCK)
        ib = idx.reshape(B, N_KV_A, nqb, Q_BLOCK, n).transpose(2, 0, 1, 3, 4)
        okb = ok.reshape(B, N_KV_A, nqb, Q_BLOCK, n).transpose(2, 0, 1, 3, 4)
        ob = lax.map(lambda a: _slc_attend(a[0], a[1], a[2], a[3], k_blocks, v_blocks), (qb, pb, ib, okb))
        o_slc = ob.swapaxes(0, 1).reshape(B, Tq, N_KV_A, GROUP_A, HEAD_DIM)
    else:
        o_slc = _slc_attend(q, qpos, idx, ok, k_blocks, v_blocks)
    return o_cmp, o_slc


def _window_attend(q, qpos, k, v, kpos):
    s = jnp.einsum('btgjd,bkgd->bgjtk', q, k) * (HEAD_DIM ** -0.5)
    diff = qpos[:, None] - kpos[None, :]
    mask = (diff >= 0) & (diff <= WINDOW) & (kpos[None, :] >= 0)
    pr = _masked_softmax(s, mask)
    return jnp.einsum('bgjtk,bkgd->btgjd', pr.astype(v.dtype), v)


def _window_banded(q, k, v):
    B, T = q.shape[0], q.shape[1]
    nb = T // Q_BLOCK
    span = WINDOW + Q_BLOCK
    idx = jnp.arange(nb)[:, None] * Q_BLOCK + jnp.arange(span)[None, :]
    padw = ((0, 0), (WINDOW, 0), (0, 0), (0, 0))
    kb = jnp.pad(k, padw)[:, idx]
    vb = jnp.pad(v, padw)[:, idx]
    qb = q.reshape(B, nb, Q_BLOCK, N_KV_A, GROUP_A, HEAD_DIM)
    qpos = jnp.arange(T).reshape(nb, Q_BLOCK)
    o = jax.vmap(_window_attend, in_axes=(1, 0, 1, 1, 0), out_axes=1)(qb, qpos, kb, vb, idx - WINDOW)
    return o.reshape(B, T, N_KV_A, GROUP_A, HEAD_DIM)


def _mem_kv(mem, p):
    B = mem.shape[0]
    m = (_rmsnorm(mem, p['norm_mem']) @ p['w_mem_kv']).reshape(B, N_MEM, 2, N_HEADS_M, HEAD_DIM)
    return jnp.stack([_rmsnorm(m[:, :, 0], p['mk_norm']), m[:, :, 1]], axis=2)


def _layer(p, x, pos0, rows_past, win_past, conv_buf, ffn_buf, mem_kv):
    B, T, _ = x.shape
    qpos = pos0 + jnp.arange(T)
    h = _rmsnorm(x, p['norm_attn'])
    q_a, kv_a, g_a, u_c, q_m, g_m = _split_in(h @ p['w_in'])
    q = _rope(_rmsnorm(q_a.reshape(B, T, N_HEADS_A, HEAD_DIM), p['q_norm']), qpos[:, None])
    q = q.reshape(B, T, N_KV_A, GROUP_A, HEAD_DIM)
    kv = kv_a.reshape(B, T, 3, 2, N_KV_A, HEAD_DIM)
    k_slc = _rope(_rmsnorm(kv[:, :, 1, 0], p['k_norm'][1]), qpos[:, None])
    k_win = _rope(_rmsnorm(kv[:, :, 2, 0], p['k_norm'][2]), qpos[:, None])
    new_rows = jnp.stack([kv[:, :, 0, 0], kv[:, :, 0, 1], k_slc, kv[:, :, 1, 1]], axis=2)
    new_win = jnp.stack([k_win, kv[:, :, 2, 1]], axis=2)
    rows = new_rows if rows_past is None else jnp.concatenate([rows_past, new_rows], axis=1)
    o_cmp, o_slc = _nsa_cmp_slc(q, qpos, rows, p)
    if win_past is None:
        o_win = _window_banded(q, new_win[:, :, 0], new_win[:, :, 1])
        win_all = new_win
    else:
        win_all = jnp.concatenate([win_past, new_win], axis=1)
        kpos = pos0 - win_past.shape[1] + jnp.arange(win_all.shape[1])
        o_win = _window_attend(q, qpos, win_all[:, :, 0], win_all[:, :, 1], kpos)
    win_state = win_all[:, -min(WINDOW, win_all.shape[1]):]
    ga = jax.nn.sigmoid(g_a.reshape(B, T, 3, N_KV_A, GROUP_A, 1))
    o_nsa = ga[:, :, 0] * o_cmp + ga[:, :, 1] * o_slc + ga[:, :, 2] * o_win
    y_a = o_nsa.reshape(B, T, N_HEADS_A * HEAD_DIM) @ p['w_o_nsa']
    a, b = jnp.split(u_c, 2, axis=-1)
    glu = a * jax.nn.sigmoid(b)
    cbuf = jnp.zeros((B, CONV_W - 1, C_CONV), glu.dtype) if conv_buf is None else conv_buf
    c, conv_state = _causal_dwconv(glu, cbuf, p['conv_w'], p['conv_b'])
    y_b = jax.nn.silu(_layernorm(c, p['conv_ln_g'], p['conv_ln_b'])) @ p['w_o_conv']
    qm = _rmsnorm(q_m.reshape(B, T, N_HEADS_M, HEAD_DIM), p['mq_norm'])
    sm = jnp.einsum('bthd,bnhd->bhtn', qm, mem_kv[:, :, 0]) * (HEAD_DIM ** -0.5)
    pm = jax.nn.softmax(sm.astype(jnp.float32), axis=-1).astype(mem_kv.dtype)
    y_m = jnp.einsum('bhtn,bnhd->bthd', pm, mem_kv[:, :, 1]).reshape(B, T, N_HEADS_M * HEAD_DIM) @ p['w_o_mem']
    gm = jax.nn.sigmoid(g_m.reshape(B, T, 3, D_MODEL))
    x = x + (gm[:, :, 0] * y_a + gm[:, :, 1] * y_b + gm[:, :, 2] * y_m) @ p['w_out']
    up = _rmsnorm(x, p['norm_ffn']) @ p['w_ffn_up']
    u, v = jnp.split(up, 2, axis=-1)
    fbuf = jnp.zeros((B, FFN_CONV_W - 1, D_FF), u.dtype) if ffn_buf is None else ffn_buf
    uc, ffn_state = _causal_dwconv(u, fbuf, p['ffn_conv_w'], p['ffn_conv_b'])
    y = x + (jax.nn.gelu(uc) * v) @ p['w_ffn_down']
    return y, new_rows, win_state, conv_state, ffn_state


def setup_inputs(seed: int = 0) -> dict:
    key = jax.random.key(seed)

    def nrm(i, shape, scale):
        return jax.random.normal(jax.random.fold_in(key, i), shape, jnp.float32) * scale

    def gain(i, shape):
        return 1.0 + nrm(i, shape, 0.01)

    n_pages = PAST_LEN // PAGE_SIZE
    n_used = DEC_BATCH * n_pages
    n_phys = n_used + (n_used + 3) // 4
    w_buf = min(WINDOW, PAST_LEN)
    page_table = jax.random.permutation(jax.random.fold_in(key, 999), n_phys)[:n_used]
    page_table = page_table.reshape(DEC_BATCH, n_pages).astype(jnp.int32)
    return {
        'x_prompt': nrm(0, (BATCH, SEQ, D_MODEL), 1.0),
        'x_sample': nrm(1, (DEC_BATCH, DEC_SEQ, D_MODEL), 1.0),
        'cache_nsa': nrm(2, (DEPTH, n_phys, PAGE_SIZE, N_ROW_COMP, N_KV_A, HEAD_DIM), 1.0),
        'cache_win': nrm(3, (DEPTH, DEC_BATCH, w_buf, 2, N_KV_A, HEAD_DIM), 1.0),
        'cache_conv': nrm(4, (DEPTH, DEC_BATCH, CONV_W - 1, C_CONV), 1.0),
        'cache_ffn': nrm(5, (DEPTH, DEC_BATCH, FFN_CONV_W - 1, D_FF), 1.0),
        'cache_mem': nrm(6, (DEPTH, DEC_BATCH, N_MEM, 2, N_HEADS_M, HEAD_DIM), 1.0),
        'page_table': page_table,
        'mem_prompt': nrm(7, (BATCH, N_MEM, D_MODEL), 1.0),
        'norm_attn': gain(8, (DEPTH, D_MODEL)),
        'w_in': nrm(9, (DEPTH, D_MODEL, IN_COLS), D_MODEL ** -0.5),
        'q_norm': gain(10, (DEPTH, HEAD_DIM)),
        'k_norm': gain(11, (DEPTH, 3, HEAD_DIM)),
        'cmp_pe': nrm(12, (DEPTH, 2, CMP_BLK, HEAD_DIM), 0.02),
        'w_cmp': nrm(13, (DEPTH, 2, CMP_BLK, HEAD_DIM, HEAD_DIM), (CMP_BLK * HEAD_DIM) ** -0.5),
        'w_o_nsa': nrm(14, (DEPTH, N_HEADS_A * HEAD_DIM, D_MODEL), (N_HEADS_A * HEAD_DIM) ** -0.5),
        'conv_w': nrm(15, (DEPTH, CONV_W, C_CONV), CONV_W ** -0.5),
        'conv_b': nrm(16, (DEPTH, C_CONV), 0.01),
        'conv_ln_g': gain(17, (DEPTH, C_CONV)),
        'conv_ln_b': nrm(18, (DEPTH, C_CONV), 0.01),
        'w_o_conv': nrm(19, (DEPTH, C_CONV, D_MODEL), C_CONV ** -0.5),
        'norm_mem': gain(20, (DEPTH, D_MODEL)),
        'w_mem_kv': nrm(21, (DEPTH, D_MODEL, 2 * N_HEADS_M * HEAD_DIM), D_MODEL ** -0.5),
        'mq_norm': gain(22, (DEPTH, HEAD_DIM)),
        'mk_norm': gain(23, (DEPTH, HEAD_DIM)),
        'w_o_mem': nrm(24, (DEPTH, N_HEADS_M * HEAD_DIM, D_MODEL), (N_HEADS_M * HEAD_DIM) ** -0.5),
        'w_out': nrm(25, (DEPTH, D_MODEL, D_MODEL), D_MODEL ** -0.5),
        'norm_ffn': gain(26, (DEPTH, D_MODEL)),
        'w_ffn_up': nrm(27, (DEPTH, D_MODEL, 2 * D_FF), D_MODEL ** -0.5),
        'ffn_conv_w': nrm(28, (DEPTH, FFN_CONV_W, D_FF), FFN_CONV_W ** -0.5),
        'ffn_conv_b': nrm(29, (DEPTH, D_FF), 0.01),
        'w_ffn_down': nrm(30, (DEPTH, D_FF, D_MODEL), D_FF ** -0.5),
    }


def reference(x_prompt, x_sample, cache_nsa, cache_win, cache_conv, cache_ffn, cache_mem, page_table, mem_prompt,
              norm_attn, w_in, q_norm, k_norm, cmp_pe, w_cmp, w_o_nsa, conv_w, conv_b, conv_ln_g, conv_ln_b, w_o_conv,
              norm_mem, w_mem_kv, mq_norm, mk_norm, w_o_mem, w_out, norm_ffn, w_ffn_up, ffn_conv_w, ffn_conv_b,
              w_ffn_down):
    params = dict(norm_attn=norm_attn, w_in=w_in, q_norm=q_norm, k_norm=k_norm, cmp_pe=cmp_pe, w_cmp=w_cmp,
                  w_o_nsa=w_o_nsa, conv_w=conv_w, conv_b=conv_b, conv_ln_g=conv_ln_g, conv_ln_b=conv_ln_b,
                  w_o_conv=w_o_conv, norm_mem=norm_mem, w_mem_kv=w_mem_kv, mq_norm=mq_norm, mk_norm=mk_norm,
                  w_o_mem=w_o_mem, w_out=w_out, norm_ffn=norm_ffn, w_ffn_up=w_ffn_up, ffn_conv_w=ffn_conv_w,
                  ffn_conv_b=ffn_conv_b, w_ffn_down=w_ffn_down)
    n_pages = page_table.shape[1]
    dec_b = x_sample.shape[0]
    y_p, y_s = x_prompt, x_sample
    rows_p, rows_s, win_p, win_s, conv_p, conv_s, ffn_p, ffn_s, mem_p = [], [], [], [], [], [], [], [], []
    for layer in range(DEPTH):
        p = {name: w[layer] for name, w in params.items()}
        mkv = _mem_kv(mem_prompt, p)
        y_p, r, wv, c, f = _layer(p, y_p, 0, None, None, None, None, mkv)
        rows_p.append(r); win_p.append(wv); conv_p.append(c); ffn_p.append(f); mem_p.append(mkv)
        past = cache_nsa[layer][page_table].reshape(dec_b, n_pages * PAGE_SIZE, N_ROW_COMP, N_KV_A, HEAD_DIM)
        y_s, r, wv, c, f = _layer(p, y_s, PAST_LEN, past, cache_win[layer], cache_conv[layer], cache_ffn[layer],
                                  cache_mem[layer])
        rows_s.append(r); win_s.append(wv); conv_s.append(c); ffn_s.append(f)
    return (y_p, y_s, jnp.stack(rows_p), jnp.stack(rows_s), jnp.stack(win_p), jnp.stack(win_s),
            jnp.stack(conv_p), jnp.stack(conv_s), jnp.stack(ffn_p), jnp.stack(ffn_s), jnp.stack(mem_p))
```

```python
import functools

import jax
import jax.numpy as jnp
from jax import lax
from jax.experimental import pallas as pl
from jax.experimental.pallas import tpu as pltpu

F32 = jnp.float32
BF16 = jnp.bfloat16

D_MODEL = 1024
HEAD_DIM = 64
N_HEADS_A = 8
N_KV_A = 2
GROUP_A = 4
CMP_BLK = 32
CMP_STRIDE = 16
SLC_BLK = 64
N_SEL = 16
WINDOW = 512
C_CONV = 512
CONV_W = 31
N_MEM = 256
N_HEADS_M = 4
D_FF = 2816
FFN_CONV_W = 3
ROPE_THETA = 500000.0
ROPE_DIM = 16
EPS = 1e-6
BIG = 1e9
NEG = -1e30
PAGE = 128

C_Q = 0
C_KV = 512
C_GA = 1280
C_GLU = 1408
C_QM = 2432
C_GM = 2688
IN_PAD = 5760

LANES = 128
VMEM_LIMIT = 56 * 1024 * 1024

TM = 256
TQ = 256
TB = 2


def _dot(a, b):
    return jnp.dot(a, b, preferred_element_type=F32)


def _dot_t(a, b):
    return lax.dot_general(a, b, (((1,), (1,)), ((), ())), preferred_element_type=F32)


def _dot_hilo(x, m):
    hi = x.astype(BF16)
    lo = (x - hi.astype(F32)).astype(BF16)
    return _dot(hi, m) + _dot(lo, m)


def _rms(x, g):
    return x * lax.rsqrt(jnp.mean(x * x, axis=-1, keepdims=True) + EPS) * g


def _head_norm(blk, gain, gmat):
    ms = _dot_hilo(blk * blk, gmat)
    return blk * lax.rsqrt(ms + EPS) * gain


def _rope(blk, tab):
    c, s1, s2 = tab
    return blk * c + pltpu.roll(blk, 8, 1) * s1 + pltpu.roll(blk, LANES - 8, 1) * s2


def _softmax_rows(s):
    m = jnp.max(s, axis=-1, keepdims=True)
    p = jnp.exp(s - m)
    return p, jnp.sum(p, axis=-1, keepdims=True)


def _in_proj_common(h, w_ref, qn, kn1, kn2, gmat, tab):
    zq = _dot(h, w_ref[:, C_Q:C_Q + 512])
    q_blocks = []
    for cb in range(4):
        blk = zq[:, cb * LANES:(cb + 1) * LANES]
        q_blocks.append(_rope(_head_norm(blk, qn, gmat), tab) * (HEAD_DIM ** -0.5))
    q = jnp.concatenate(q_blocks, axis=1)
    zkv = _dot(h, w_ref[:, C_KV:C_KV + 768])
    k_slc = _rope(_head_norm(zkv[:, 256:384], kn1, gmat), tab)
    k_win = _rope(_head_norm(zkv[:, 512:640], kn2, gmat), tab)
    rows = jnp.concatenate([zkv[:, 0:256], k_slc, zkv[:, 384:512]], axis=1)
    win = jnp.concatenate([k_win, zkv[:, 640:768]], axis=1)
    ga = jax.nn.sigmoid(_dot(h, w_ref[:, C_GA:C_GA + LANES]))
    return q, rows, win, ga


def _glu(h, w_ref):
    z = _dot(h, w_ref[:, C_GLU:C_GLU + 1024])
    return z[:, :C_CONV] * jax.nn.sigmoid(z[:, C_CONV:])


def _conv_tail(c, lng, lnb, wo_ref):
    mu = jnp.mean(c, axis=-1, keepdims=True)
    var = jnp.mean(jnp.square(c - mu), axis=-1, keepdims=True)
    y = (c - mu) * lax.rsqrt(var + EPS) * lng + lnb
    return _dot(jax.nn.silu(y).astype(BF16), wo_ref[...])


def _mem_q(h, w_ref, mqn, gmat):
    z = _dot(h, w_ref[:, C_QM:C_QM + 256])
    return jnp.concatenate(
        [_head_norm(z[:, cb * LANES:(cb + 1) * LANES], mqn, gmat) for cb in range(2)], axis=1) * (HEAD_DIM ** -0.5)


def _ffn_tail(x1, nf, wup_ref, u_prev2, u_prev1_fn, fcw, fcb, wdn_ref):
    h2 = _rms(x1, nf).astype(BF16)
    up = _dot(h2, wup_ref[...])
    u = up[:, :D_FF]
    v = up[:, D_FF:]
    uc = fcw[0:1] * u_prev2(u) + fcw[1:2] * u_prev1_fn(u) + fcw[2:3] * u + fcb
    act = jax.nn.gelu(uc, approximate=True) * v
    return x1 + _dot(act.astype(BF16), wdn_ref[...]), u


def _memkv_kernel(mem_ref, nm_ref, w_ref, mkn_ref, gmat_ref, o_ref):
    h = _rms(mem_ref[0], nm_ref[...]).astype(BF16)
    z = _dot(h, w_ref[...])
    gmat = gmat_ref[...]
    k = [_head_norm(z[:, cb * LANES:(cb + 1) * LANES], mkn_ref[...], gmat) for cb in range(2)]
    o_ref[0] = jnp.concatenate(k + [z[:, 256:512]], axis=1)


def _pre_prompt_kernel(x_ref, na_ref, w_ref, qn_ref, kn1_ref, kn2_ref, gmat_ref, tab_ref,
                       cw_ref, cb_ref, lng_ref, lnb_ref, woc_ref, mkv_ref, mqn_ref, wom_ref,
                       q_ref, rows_ref, kva_ref, win_ref, ga_ref, gm0_ref, part_ref, cst_ref, hbuf):
    t = pl.program_id(1)
    gmat = gmat_ref[...]
    tab = (tab_ref[0], tab_ref[1], tab_ref[2])
    h = _rms(x_ref[0], na_ref[...]).astype(BF16)
    q, rows, win, ga = _in_proj_common(h, w_ref, qn_ref[...], kn1_ref[...], kn2_ref[...], gmat, tab)
    q_ref[0] = q.astype(BF16)
    rows_ref[0] = rows
    win_ref[0] = win
    ga_ref[0] = ga
    kva_ref[0] = jnp.concatenate([rows[:, 256:512], win], axis=1).astype(BF16)

    glu = _glu(h, w_ref)

    @pl.when(t == 0)
    def _():
        hbuf[0:32, :] = jnp.zeros((32, C_CONV), F32)

    hbuf[32:32 + TM, :] = glu
    cw = cw_ref[...]
    c = jnp.zeros((TM, C_CONV), F32) + cb_ref[...]
    for k in range(CONV_W):
        c = c + cw[k:k + 1] * hbuf[pl.ds(k + 2, TM), :]
    cst_ref[0] = hbuf[pl.ds(TM + 2, CONV_W - 1), :]
    hbuf[0:32, :] = hbuf[TM:TM + 32, :]
    y_b = _conv_tail(c, lng_ref[...], lnb_ref[...], woc_ref)

    qm = _mem_q(h, w_ref, mqn_ref[...], gmat).astype(BF16)
    mkv = mkv_ref[0].astype(BF16)
    heads = []
    for hh in range(N_HEADS_M):
        s = _dot_t(qm[:, hh * 64:(hh + 1) * 64], mkv[:, hh * 64:(hh + 1) * 64])
        p, den = _softmax_rows(s)
        heads.append(_dot(p.astype(BF16), mkv[:, 256 + hh * 64:256 + (hh + 1) * 64]) / den)
    y_m = _dot(jnp.concatenate(heads, axis=1).astype(BF16), wom_ref[...])

    gm = jax.nn.sigmoid(_dot(h, w_ref[:, C_GM:C_GM + 3072]))
    gm0_ref[0] = gm[:, 0:1024].astype(BF16)
    part_ref[0] = (gm[:, 1024:2048] * y_b + gm[:, 2048:3072] * y_m).astype(BF16)


def _pe_term_kernel(pe_ref, wcat_ref, o_ref):
    acc = jnp.zeros((8, 256), F32)
    for l in range(16):
        top = jnp.broadcast_to(pe_ref[l:l + 1, :], (8, 256)).astype(BF16)
        bot = jnp.broadcast_to(pe_ref[l + 16:l + 17, :], (8, 256)).astype(BF16)
        acc = acc + _dot(top, wcat_ref[l][:, 0:256]) + _dot(bot, wcat_ref[l][:, 256:512])
    o_ref[...] = acc


def _compress(tap_fn, n_rows, wcat_ref, pe_term, kn0, gmat, ctab):
    acc = jnp.zeros((n_rows, 512), F32)
    for l in range(16):
        acc = acc + _dot(tap_fn(l).astype(BF16), wcat_ref[l])
    top = acc[:, 0:256]
    bot = pltpu.roll(acc[:, 256:512], n_rows - 1, 0)
    kcv = top + bot + pe_term
    kc = _rope(_head_norm(kcv[:, 0:128], kn0, gmat), ctab)
    return kc.astype(BF16), kcv[:, 128:256].astype(BF16)


def _compress_prompt_kernel(kr_ref, vr_ref, wcat_ref, pet_ref, kn0_ref, gmat_ref, ctab_ref, kc_ref, vc_ref):
    def tap(l):
        return jnp.concatenate([r[0, pl.ds(l, 128, stride=16), :] for r in (kr_ref, vr_ref)], axis=1)

    kc, vc = _compress(tap, 128, wcat_ref, pet_ref[0:1, :],
                       kn0_ref[...], gmat_ref[...], (ctab_ref[0], ctab_ref[1], ctab_ref[2]))
    kc_ref[0] = kc
    vc_ref[0] = vc


def _select_rows(imp, tpos, n_slc):
    lane = lax.broadcasted_iota(jnp.int32, imp.shape, 1)
    qblk = jnp.right_shift(tpos, 6)
    forced = (lane == 0) | (lane == qblk) | (lane == qblk - 1)
    score = jnp.where(forced, BIG, jnp.where(lane <= qblk, imp, -BIG))
    score = jnp.where(lane < n_slc, score, -3.0 * BIG)
    rank = jnp.zeros(imp.shape, F32)
    for s in range(n_slc):
        col = score[:, s:s + 1]
        beats = (col > score) | ((col == score) & (lane > s))
        rank = rank + jnp.where(beats, 1.0, 0.0)
    return jnp.where((rank < N_SEL) & (score > -0.5 * BIG), 1.0, 0.0)


def _attn_prompt_kernel(q_ref, kva_ref, kc_ref, vc_ref, ga_ref, eg_ref, ov_ref, ex_ref, o_ref):
    qt = pl.program_id(1)
    t0 = qt * TQ
    q = q_ref[0]
    tpos = t0 + lax.broadcasted_iota(jnp.int32, (TQ, 1), 0)
    gates = _dot_hilo(ga_ref[0], eg_ref[...])
    n_cmp = lax.broadcasted_iota(jnp.int32, (TQ, LANES), 1)
    cmask = jnp.where(((n_cmp * CMP_STRIDE + (CMP_BLK - 1)) <= tpos) & (n_cmp < 127), 1.0, 0.0)
    cbias = (cmask - 1.0) * (-NEG)
    w0 = pl.multiple_of(jnp.maximum(qt - 2, 0) * TQ, TQ)
    wpos = w0 + lax.broadcasted_iota(jnp.int32, (1, 3 * TQ), 1)
    wdiff = tpos - wpos
    wbias = jnp.where((wdiff >= 0) & (wdiff <= WINDOW), 0.0, NEG)

    outs = [[], [], []]
    for g in range(N_KV_A):
        qs = jnp.concatenate([q[:, (4 * g + j) * 64:(4 * g + j + 1) * 64] for j in range(GROUP_A)], axis=0)
        lo, hi = g * 64, (g + 1) * 64
        s = _dot_t(qs, kc_ref[0][:, lo:hi]).reshape(GROUP_A, TQ, LANES)
        s = s + cbias[None]
        m = jnp.max(s, axis=-1, keepdims=True)
        p = jnp.exp(s - m) * cmask[None]
        den = jnp.sum(p, axis=-1, keepdims=True)
        p = p / jnp.where(den > 0.0, den, 1.0)
        o_cmp = _dot(p.reshape(GROUP_A * TQ, LANES).astype(BF16), vc_ref[0][:, lo:hi])
        imp = _dot_hilo(jnp.sum(p, axis=0), ov_ref[...])
        sel = _select_rows(imp, tpos, 32).astype(BF16)

        def body(c, carry):
            m_i, l_i, acc = carry
            k0 = pl.multiple_of(c * TQ, TQ)
            kk = kva_ref[0, pl.ds(k0, TQ), lo:hi]
            vv = kva_ref[0, pl.ds(k0, TQ), 128 + lo:128 + hi]
            selx = _dot(sel, ex_ref[c])
            kpos = k0 + lax.broadcasted_iota(jnp.int32, (1, TQ), 1)
            bias = jnp.where((selx > 0.5) & (kpos <= tpos), 0.0, NEG)
            sc = _dot_t(qs, kk).reshape(GROUP_A, TQ, TQ) + bias[None]
            m_new = jnp.maximum(m_i, jnp.max(sc, axis=-1, keepdims=True))
            alpha = jnp.exp(m_i - m_new)
            pp = jnp.exp(sc - m_new)
            l_new = alpha * l_i + jnp.sum(pp, axis=-1, keepdims=True)
            pv = _dot(pp.reshape(GROUP_A * TQ, TQ).astype(BF16), vv)
            acc = alpha.reshape(GROUP_A * TQ, 1) * acc + pv
            return m_new, l_new, acc

        m0 = jnp.full((GROUP_A, TQ, 1), NEG, F32)
        l0 = jnp.zeros((GROUP_A, TQ, 1), F32)
        a0 = jnp.zeros((GROUP_A * TQ, 64), F32)
        _, l_f, acc = lax.fori_loop(0, qt + 1, body, (m0, l0, a0))
        o_slc = acc / l_f.reshape(GROUP_A * TQ, 1)

        kw = kva_ref[0, pl.ds(w0, 3 * TQ), 256 + lo:256 + hi]
        vw = kva_ref[0, pl.ds(w0, 3 * TQ), 384 + lo:384 + hi]
        sw = _dot_t(qs, kw).reshape(GROUP_A, TQ, 3 * TQ)
        sw = sw + wbias[None]
        pw, dw = _softmax_rows(sw)
        o_win = _dot(pw.reshape(GROUP_A * TQ, 3 * TQ).astype(BF16), vw) / dw.reshape(GROUP_A * TQ, 1)

        for c, o in enumerate((o_cmp, o_slc, o_win)):
            outs[c] += [o[j * TQ:(j + 1) * TQ, :] for j in range(GROUP_A)]

    o_nsa = jnp.zeros((TQ, 512), F32)
    for c in range(3):
        o_nsa = o_nsa + gates[:, c * 512:(c + 1) * 512] * jnp.concatenate(outs[c], axis=1)
    o_ref[0] = o_nsa.astype(BF16)


def _post_prompt_kernel(x_ref, on_ref, gm0_ref, part_ref, wonsa_ref, wout_ref, nf_ref, wup_ref, fcw_ref,
                        fcb_ref, wdn_ref, y_ref, fst_ref, ubuf):
    t = pl.program_id(1)
    y_a = _dot(on_ref[0], wonsa_ref[...])
    merged = gm0_ref[0].astype(F32) * y_a + part_ref[0].astype(F32)
    x1 = x_ref[0] + _dot(merged.astype(BF16), wout_ref[...])

    @pl.when(t == 0)
    def _():
        ubuf[0:8, :] = jnp.zeros((8, D_FF), F32)

    def prev2(u):
        ubuf[8:8 + TM, :] = u
        return ubuf[pl.ds(6, TM), :]

    def prev1(u):
        return ubuf[pl.ds(7, TM), :]

    y, u = _ffn_tail(x1, nf_ref[...], wup_ref, prev2, prev1, fcw_ref[...], fcb_ref[...], wdn_ref)
    y_ref[0] = y
    fst_ref[0] = u[TM - 2:TM, :]
    ubuf[0:8, :] = ubuf[TM:TM + 8, :]


def _pre_sample_kernel(x_ref, na_ref, w_ref, qn_ref, kn1_ref, kn2_ref, gmat_ref, tab_ref,
                       cw_ref, cb_ref, lng_ref, lnb_ref, woc_ref, cc_ref, mqn_ref, eg_ref,
                       q_ref, rows_ref, win_ref, gae_ref, qm_ref, gm_ref, yb_ref, cst_ref):
    gmat = gmat_ref[...]
    tab = (tab_ref[0], tab_ref[1], tab_ref[2])
    h = _rms(x_ref[...], na_ref[...]).astype(BF16)
    q, rows, win, ga = _in_proj_common(h, w_ref, qn_ref[...], kn1_ref[...], kn2_ref[...], gmat, tab)
    q_ref[...] = q
    rows_ref[...] = rows
    win_ref[...] = win
    gae_ref[...] = _dot_hilo(ga, eg_ref[...])
    glu = _glu(h, w_ref)
    cw = cw_ref[...]
    c = cw[CONV_W - 1:CONV_W] * glu + cb_ref[...]
    for k in range(CONV_W - 1):
        c = c + cw[k:k + 1] * cc_ref[:, k, :]
    for k in range(CONV_W - 2):
        cst_ref[:, k, :] = cc_ref[:, k + 1, :]
    cst_ref[:, CONV_W - 2, :] = glu
    yb_ref[...] = _conv_tail(c, lng_ref[...], lnb_ref[...], woc_ref)
    qm_ref[...] = _mem_q(h, w_ref, mqn_ref[...], gmat)
    gm_ref[...] = jax.nn.sigmoid(_dot(h, w_ref[:, C_GM:C_GM + 3072]))


def _heads_to_lanes(o8):
    lane = lax.broadcasted_iota(jnp.int32, (1, LANES), 1)
    blocks = []
    for cb in range(4):
        a = o8[2 * cb:2 * cb + 1, :]
        b = o8[2 * cb + 1:2 * cb + 2, :]
        if cb // 2 == 1:
            a = pltpu.roll(a, 64, 1)
        else:
            b = pltpu.roll(b, 64, 1)
        blocks.append(jnp.where(lane < 64, a, b))
    return jnp.concatenate(blocks, axis=1)


def _attn_sample_kernel(pt_ref, *refs):
    del pt_ref
    page_refs = refs[:TB * 16 * 4]
    (q_ref, rn_ref, wn_ref, gae_ref, qm_ref, cw_ref, cm_ref, wcat_ref, pet_ref, kn0_ref, gmat_ref, ctab_ref,
     ov_ref, ex_ref, on_ref, om_ref, wo_ref) = refs[TB * 16 * 4:]
    lane = lax.broadcasted_iota(jnp.int32, (1, LANES), 1)
    row8 = lax.broadcasted_iota(jnp.int32, (8, LANES), 0)
    lane8 = lax.broadcasted_iota(jnp.int32, (8, LANES), 1)

    def tap(l):
        return jnp.concatenate(
            [jnp.concatenate([page_refs[4 * i + c][0, pl.ds(l, 8, stride=16), :] for i in range(TB * 16)], axis=0)
             for c in range(2)], axis=1)

    kc_all, vc_all = _compress(tap, TB * 128, wcat_ref, pet_ref[0:1, :], kn0_ref[...], gmat_ref[...],
                               tuple(jnp.concatenate([ctab_ref[i]] * TB, axis=0) for i in range(3)))

    for tok in range(TB):
        k_pages = [page_refs[(tok * 16 + i) * 4 + 2] for i in range(16)]
        v_pages = [page_refs[(tok * 16 + i) * 4 + 3] for i in range(16)]
        q = q_ref[tok]
        rn = rn_ref[tok]
        wn = wn_ref[tok]
        gae = gae_ref[tok]
        q_rows = []
        for r in range(N_HEADS_A):
            piece = q[:, (r // 2) * LANES:(r // 2 + 1) * LANES]
            if (r % 2) != (r // 4):
                piece = pltpu.roll(piece, 64, 1)
            q_rows.append(jnp.where(jnp.right_shift(lane, 6) == (r // 4), piece, 0.0))
        q8 = jnp.concatenate(q_rows, axis=0).astype(BF16)
        q8f = q8.astype(F32)
        kc = kc_all[tok * 128:(tok + 1) * 128]
        vc = vc_all[tok * 128:(tok + 1) * 128]

        s = jnp.where(lane8 < 127, _dot_t(q8, kc), NEG)
        p, den = _softmax_rows(s)
        p = p / den
        o_cmp = _dot(p.astype(BF16), vc)
        imp8 = _dot_hilo(p, ov_ref[...])

        sel_rows = []
        for g in range(N_KV_A):
            imp = jnp.sum(imp8[4 * g:4 * g + 4, :], axis=0, keepdims=True)
            forced = (lane == 0) | (lane == 31) | (lane == 32)
            score = jnp.where(lane < 33, jnp.where(forced, BIG, imp), -3.0 * BIG)
            a = jnp.broadcast_to(score, (LANES, LANES))
            b = a.T
            sub = lax.broadcasted_iota(jnp.int32, (LANES, LANES), 0)
            ln = lax.broadcasted_iota(jnp.int32, (LANES, LANES), 1)
            beats = (b > a) | ((b == a) & (sub < ln))
            rank = jnp.sum(jnp.where(beats, 1.0, 0.0), axis=0, keepdims=True)
            sel_rows.append(jnp.where((rank < N_SEL) & (lane < 33), 1.0, 0.0))
        sel8 = jnp.where(row8 < 4, sel_rows[0], sel_rows[1]).astype(BF16)

        s_parts = [_dot_t(q8, k_pages[i][0].astype(BF16)) for i in range(16)]
        s = jnp.concatenate(s_parts, axis=1)
        selx = _dot(sel8, ex_ref[...])
        s = jnp.where(selx > 0.5, s, NEG)
        s_new = jnp.sum(q8f * rn[:, 256:384].astype(BF16).astype(F32), axis=-1, keepdims=True)
        m = jnp.maximum(jnp.max(s, axis=-1, keepdims=True), s_new)
        p = jnp.exp(s - m)
        p_new = jnp.exp(s_new - m)
        den = jnp.sum(p, axis=-1, keepdims=True) + p_new
        o_slc = p_new.astype(BF16).astype(F32) * rn[:, 384:512].astype(BF16).astype(F32)
        for i in range(16):
            o_slc = o_slc + _dot(p[:, i * PAGE:(i + 1) * PAGE].astype(BF16), v_pages[i][0].astype(BF16))
        o_slc = o_slc / den

        cw = cw_ref[tok]
        s = _dot_t(q8, cw[:, 0:128].astype(BF16))
        s_new = jnp.sum(q8f * wn[:, 0:128].astype(BF16).astype(F32), axis=-1, keepdims=True)
        m = jnp.maximum(jnp.max(s, axis=-1, keepdims=True), s_new)
        p = jnp.exp(s - m)
        p_new = jnp.exp(s_new - m)
        den = jnp.sum(p, axis=-1, keepdims=True) + p_new
        o_win = (_dot(p.astype(BF16), cw[:, 128:256].astype(BF16))
                 + p_new.astype(BF16).astype(F32) * wn[:, 128:256].astype(BF16).astype(F32)) / den
        rolled = pltpu.roll(cw, WINDOW - 1, 0)
        last = lax.broadcasted_iota(jnp.int32, (WINDOW, 1), 0) == WINDOW - 1
        wo_ref[tok] = jnp.where(last, wn, rolled)

        on_ref[tok] = (gae[:, 0:512] * _heads_to_lanes(o_cmp) + gae[:, 512:1024] * _heads_to_lanes(o_slc)
                       + gae[:, 1024:1536] * _heads_to_lanes(o_win))

        cm = cm_ref[tok]
        rowm = lax.broadcasted_iota(jnp.int32, (8, 256), 0)
        headm = jnp.right_shift(lax.broadcasted_iota(jnp.int32, (8, 256), 1), 6)
        qm8 = jnp.where(rowm == headm, jnp.broadcast_to(qm_ref[tok], (8, 256)), 0.0).astype(BF16)
        s = _dot_t(qm8, cm[:, 0:256].astype(BF16))
        p, den = _softmax_rows(s)
        o8 = _dot(p.astype(BF16), cm[:, 256:512].astype(BF16)) / den
        om_ref[tok] = jnp.sum(jnp.where(rowm == headm, o8, 0.0), axis=0, keepdims=True)


def _post_sample_kernel(x_ref, on_ref, om_ref, gm_ref, yb_ref, wonsa_ref, wom_ref, wout_ref, nf_ref, wup_ref,
                        fcw_ref, fcb_ref, wdn_ref, cf_ref, y_ref, fst_ref):
    y_a = _dot(on_ref[...].astype(BF16), wonsa_ref[...])
    y_m = _dot(om_ref[...].astype(BF16), wom_ref[...])
    gm = gm_ref[...]
    merged = gm[:, 0:1024] * y_a + gm[:, 1024:2048] * yb_ref[...] + gm[:, 2048:3072] * y_m
    x1 = x_ref[...] + _dot(merged.astype(BF16), wout_ref[...])
    y, u = _ffn_tail(x1, nf_ref[...], wup_ref, lambda u: cf_ref[:, 0, :], lambda u: cf_ref[:, 1, :],
                     fcw_ref[...], fcb_ref[...], wdn_ref)
    y_ref[...] = y
    fst_ref[:, 0, :] = cf_ref[:, 1, :]
    fst_ref[:, 1, :] = u


def _rope_tables(pos):
    inv = ROPE_THETA ** (-jnp.arange(0, ROPE_DIM, 2, dtype=F32) / ROPE_DIM)
    ang = pos.astype(F32)[:, None] * inv
    cos, sin = jnp.cos(ang), jnp.sin(ang)
    n = pos.shape[0]
    one = jnp.ones((n, HEAD_DIM - ROPE_DIM), F32)
    z8 = jnp.zeros((n, 8), F32)
    z48 = jnp.zeros((n, HEAD_DIM - ROPE_DIM), F32)
    c = jnp.concatenate([cos, cos, one], axis=1)
    s1 = jnp.concatenate([z8, sin, z48], axis=1)
    s2 = jnp.concatenate([-sin, z8, z48], axis=1)
    return jnp.stack([jnp.tile(c, (1, 2)), jnp.tile(s1, (1, 2)), jnp.tile(s2, (1, 2))])


def _tile2(v):
    return jnp.tile(v.reshape(1, HEAD_DIM), (1, 2))


def _const(shape):
    nd = len(shape)
    return pl.BlockSpec(shape, lambda *_: (0,) * nd, pipeline_mode=pl.Buffered(1))


def _params(*sem):
    return pltpu.CompilerParams(dimension_semantics=sem, vmem_limit_bytes=VMEM_LIMIT)


def kernel(x_prompt, x_sample, cache_nsa, cache_win, cache_conv, cache_ffn, cache_mem, page_table, mem_prompt,
           norm_attn, w_in, q_norm, k_norm, cmp_pe, w_cmp, w_o_nsa, conv_w, conv_b, conv_ln_g, conv_ln_b, w_o_conv,
           norm_mem, w_mem_kv, mq_norm, mk_norm, w_o_mem, w_out, norm_ffn, w_ffn_up, ffn_conv_w, ffn_conv_b,
           w_ffn_down):
    B, T, _ = x_prompt.shape
    NS = x_sample.shape[0]
    n_pages = page_table.shape[1]
    assert w_in.shape[0] == 1 and T == 2048 and n_pages * PAGE == 2048 and cache_win.shape[2] == WINDOW
    nt = T // TM

    w_in0 = w_in[0]
    w_in_p = jnp.concatenate([w_in0[:, :1304], jnp.zeros((D_MODEL, 104), F32), w_in0[:, 1304:]],
                             axis=1).astype(BF16)
    na = norm_attn.reshape(1, D_MODEL)
    nf = norm_ffn.reshape(1, D_MODEL)
    nm = norm_mem.reshape(1, D_MODEL)
    qn, mqn, mkn = _tile2(q_norm[0]), _tile2(mq_norm[0]), _tile2(mk_norm[0])
    kn0, kn1, kn2 = _tile2(k_norm[0, 0]), _tile2(k_norm[0, 1]), _tile2(k_norm[0, 2])
    ii = jnp.arange(LANES)
    gmat = jnp.where((ii[:, None] // 64) == (ii[None, :] // 64), 1.0 / 64, 0.0).astype(BF16)
    tab_p = _rope_tables(jnp.arange(T))
    tab_s = _rope_tables(jnp.full((1,), n_pages * PAGE))
    ctab = _rope_tables(jnp.arange(128) * CMP_STRIDE + (CMP_BLK - 1))
    eye2 = jnp.eye(2, dtype=F32)
    w_l = jnp.einsum('klde,kK,gG->lkgdKGe', w_cmp[0], eye2, eye2).reshape(CMP_BLK, 256, 256)
    wcat = jnp.concatenate([w_l[:16], w_l[16:]], axis=-1).astype(BF16)
    pe2 = jnp.broadcast_to(cmp_pe[0].transpose(1, 0, 2)[:, :, None, :], (CMP_BLK, 2, 2, HEAD_DIM)).reshape(
        CMP_BLK, 256)
    col = jnp.arange(3 * 512)
    egate = (jnp.arange(LANES)[:, None] == ((col // 512) * 8 + (col % 512) // 64)[None, :]).astype(BF16)
    cs = jnp.arange(LANES)[:, None] * CMP_STRIDE
    ss = jnp.arange(LANES)[None, :] * SLC_BLK
    overlap = ((cs < ss + SLC_BLK) & (cs + CMP_BLK > ss) & (jnp.arange(LANES)[:, None] < 127)
               & (jnp.arange(LANES)[None, :] < 33)).astype(BF16)
    kpos = jnp.arange(T)
    expand = ((jnp.arange(LANES)[:, None] == (kpos // SLC_BLK)[None, :])).astype(BF16)
    expand_c = expand.reshape(LANES, T // TQ, TQ).transpose(1, 0, 2)
    woc = w_o_conv[0].astype(BF16)
    wom = w_o_mem[0].astype(BF16)
    wonsa = w_o_nsa[0].astype(BF16)
    wout = w_out[0].astype(BF16)
    wup = w_ffn_up[0].astype(BF16)
    wdn = w_ffn_down[0].astype(BF16)
    wmkv = w_mem_kv[0].astype(BF16)
    cw, cb = conv_w[0], conv_b.reshape(1, C_CONV)
    lng, lnb = conv_ln_g.reshape(1, C_CONV), conv_ln_b.reshape(1, C_CONV)
    fcw, fcb = ffn_conv_w[0], ffn_conv_b.reshape(1, D_FF)

    mem_kv = pl.pallas_call(
        _memkv_kernel, grid=(B,),
        in_specs=[pl.BlockSpec((1, N_MEM, D_MODEL), lambda b: (b, 0, 0)), _const((1, D_MODEL)),
                  _const((D_MODEL, 512)), _const((1, LANES)), _const((LANES, LANES))],
        out_specs=pl.BlockSpec((1, N_MEM, 512), lambda b: (b, 0, 0)),
        out_shape=jax.ShapeDtypeStruct((B, N_MEM, 512), F32),
        compiler_params=_params("arbitrary"), name="mem_kv",
    )(mem_prompt, nm, wmkv, mkn, gmat)

    def tile(width):
        return pl.BlockSpec((1, TM, width), lambda b, t: (b, t, 0))

    n_win_t = WINDOW // TM
    pre_out_shapes = (
        jax.ShapeDtypeStruct((B, T, 512), BF16),
        jax.ShapeDtypeStruct((B, T, 512), F32),
        jax.ShapeDtypeStruct((B, T, 512), BF16),
        jax.ShapeDtypeStruct((B, WINDOW, 256), F32),
        jax.ShapeDtypeStruct((B, T, LANES), F32),
        jax.ShapeDtypeStruct((B, T, D_MODEL), BF16),
        jax.ShapeDtypeStruct((B, T, D_MODEL), BF16),
        jax.ShapeDtypeStruct((B, CONV_W - 1, C_CONV), F32),
    )
    q_p, rows_p, kva_p, win_p, ga_p, gm0_p, part_p, cst_p = pl.pallas_call(
        _pre_prompt_kernel, grid=(B, nt),
        in_specs=[tile(D_MODEL), _const((1, D_MODEL)), _const((D_MODEL, IN_PAD)), _const((1, LANES)),
                  _const((1, LANES)), _const((1, LANES)), _const((LANES, LANES)),
                  pl.BlockSpec((3, TM, LANES), lambda b, t: (0, t, 0)),
                  _const((CONV_W, C_CONV)), _const((1, C_CONV)), _const((1, C_CONV)), _const((1, C_CONV)),
                  _const((C_CONV, D_MODEL)),
                  pl.BlockSpec((1, N_MEM, 512), lambda b, t: (b, 0, 0)), _const((1, LANES)),
                  _const((256, D_MODEL))],
        out_specs=(tile(512), tile(512), tile(512),
                   pl.BlockSpec((1, TM, 256), lambda b, t: (b, jnp.maximum(t - (nt - n_win_t), 0), 0)),
                   tile(LANES), tile(D_MODEL), tile(D_MODEL),
                   pl.BlockSpec((1, CONV_W - 1, C_CONV), lambda b, t: (b, 0, 0))),
        out_shape=pre_out_shapes,
        scratch_shapes=[pltpu.VMEM((TM + 32, C_CONV), F32)],
        compiler_params=_params("arbitrary", "arbitrary"), name="pre_prompt",
    )(x_prompt, na, w_in_p, qn, kn1, kn2, gmat, tab_p, cw, cb, lng, lnb, woc, mem_kv, mqn, wom)

    pe_term = pl.pallas_call(
        _pe_term_kernel, out_shape=jax.ShapeDtypeStruct((8, 256), F32), name="pe_term",
    )(pe2, wcat)
    kc_p, vc_p = pl.pallas_call(
        _compress_prompt_kernel, grid=(B,),
        in_specs=[pl.BlockSpec((1, T, LANES), lambda b: (b, 0, 0)), pl.BlockSpec((1, T, LANES), lambda b: (b, 0, 1)),
                  _const((16, 256, 512)), _const((8, 256)),
                  _const((1, LANES)), _const((LANES, LANES)), _const((3, LANES, LANES))],
        out_specs=(pl.BlockSpec((1, LANES, LANES), lambda b: (b, 0, 0)),) * 2,
        out_shape=(jax.ShapeDtypeStruct((B, LANES, LANES), BF16),) * 2,
        compiler_params=_params("arbitrary"), name="compress_prompt",
    )(rows_p, rows_p, wcat, pe_term, kn0, gmat, ctab)

    o_nsa_p = pl.pallas_call(
        _attn_prompt_kernel, grid=(B, T // TQ),
        in_specs=[pl.BlockSpec((1, TQ, 512), lambda b, t: (b, t, 0)),
                  pl.BlockSpec((1, T, 512), lambda b, t: (b, 0, 0)),
                  pl.BlockSpec((1, LANES, LANES), lambda b, t: (b, 0, 0)),
                  pl.BlockSpec((1, LANES, LANES), lambda b, t: (b, 0, 0)),
                  pl.BlockSpec((1, TQ, LANES), lambda b, t: (b, t, 0)),
                  _const((LANES, 3 * 512)), _const((LANES, LANES)), _const((T // TQ, LANES, TQ))],
        out_specs=pl.BlockSpec((1, TQ, 512), lambda b, t: (b, t, 0)),
        out_shape=jax.ShapeDtypeStruct((B, T, 512), BF16),
        compiler_params=_params("arbitrary", "arbitrary"), name="attn_prompt",
    )(q_p, kva_p, kc_p, vc_p, ga_p, egate, overlap, expand_c)

    y_p, fst_p = pl.pallas_call(
        _post_prompt_kernel, grid=(B, nt),
        in_specs=[tile(D_MODEL), tile(512), tile(D_MODEL), tile(D_MODEL), _const((512, D_MODEL)),
                  _const((D_MODEL, D_MODEL)), _const((1, D_MODEL)), _const((D_MODEL, 2 * D_FF)),
                  _const((FFN_CONV_W, D_FF)), _const((1, D_FF)), _const((D_FF, D_MODEL))],
        out_specs=(tile(D_MODEL), pl.BlockSpec((1, FFN_CONV_W - 1, D_FF), lambda b, t: (b, 0, 0))),
        out_shape=(jax.ShapeDtypeStruct((B, T, D_MODEL), F32),
                   jax.ShapeDtypeStruct((B, FFN_CONV_W - 1, D_FF), F32)),
        scratch_shapes=[pltpu.VMEM((TM + 8, D_FF), F32)],
        compiler_params=_params("arbitrary", "arbitrary"), name="post_prompt",
    )(x_prompt, o_nsa_p, gm0_p, part_p, wonsa, wout, nf, wup, fcw, fcb, wdn)

    xs = x_sample.reshape(NS, D_MODEL)
    pre_s_shapes = (
        jax.ShapeDtypeStruct((NS, 512), F32), jax.ShapeDtypeStruct((NS, 512), F32),
        jax.ShapeDtypeStruct((NS, 256), F32), jax.ShapeDtypeStruct((NS, 3 * 512), F32),
        jax.ShapeDtypeStruct((NS, 256), F32), jax.ShapeDtypeStruct((NS, 3 * D_MODEL), F32),
        jax.ShapeDtypeStruct((NS, D_MODEL), F32), jax.ShapeDtypeStruct((NS, CONV_W - 1, C_CONV), F32),
    )
    q_s, rows_s, win_s, gae_s, qm_s, gm_s, yb_s, cst_s = pl.pallas_call(
        _pre_sample_kernel, out_shape=pre_s_shapes,
        compiler_params=pltpu.CompilerParams(vmem_limit_bytes=VMEM_LIMIT), name="pre_sample",
    )(xs, na, w_in_p, qn, kn1, kn2, gmat, tab_s, cw, cb, lng, lnb, woc, cache_conv[0], mqn, egate)

    pages = cache_nsa[0].reshape(cache_nsa.shape[1], PAGE, 512)
    pt_flat = page_table.reshape(-1)

    def page_spec(tok, p, c):
        return pl.BlockSpec((1, PAGE, LANES), lambda i, pt: (pt[(i * TB + tok) * n_pages + p], 0, c))

    def tok_spec(width):
        return pl.BlockSpec((TB, 1, width), lambda i, pt: (i, 0, 0))

    def cst_spec(shape):
        nd = len(shape)
        return pl.BlockSpec(shape, lambda i, pt: (0,) * nd, pipeline_mode=pl.Buffered(1))

    grid_spec = pltpu.PrefetchScalarGridSpec(
        num_scalar_prefetch=1, grid=(NS // TB,),
        in_specs=[page_spec(tok, p, c) for tok in range(TB) for p in range(n_pages) for c in range(4)] + [
            tok_spec(512), tok_spec(512), tok_spec(256), tok_spec(3 * 512), tok_spec(256),
            pl.BlockSpec((TB, WINDOW, 256), lambda i, pt: (i, 0, 0)),
            pl.BlockSpec((TB, N_MEM, 512), lambda i, pt: (i, 0, 0)),
            cst_spec((16, 256, 512)), cst_spec((8, 256)), cst_spec((1, LANES)), cst_spec((LANES, LANES)),
            cst_spec((3, LANES, LANES)), cst_spec((LANES, LANES)), cst_spec((LANES, T))],
        out_specs=(tok_spec(512), tok_spec(256), pl.BlockSpec((TB, WINDOW, 256), lambda i, pt: (i, 0, 0))),
    )
    on_s, om_s, wo_s = pl.pallas_call(
        _attn_sample_kernel, grid_spec=grid_spec,
        out_shape=(jax.ShapeDtypeStruct((NS, 1, 512), F32), jax.ShapeDtypeStruct((NS, 1, 256), F32),
                   jax.ShapeDtypeStruct((NS, WINDOW, 256), F32)),
        compiler_params=_params("arbitrary"), name="attn_sample",
    )(pt_flat, *([pages] * (TB * n_pages * 4)), q_s.reshape(NS, 1, 512), rows_s.reshape(NS, 1, 512),
      win_s.reshape(NS, 1, 256), gae_s.reshape(NS, 1, 3 * 512), qm_s.reshape(NS, 1, 256),
      cache_win[0].reshape(NS, WINDOW, 256), cache_mem[0].reshape(NS, N_MEM, 512),
      wcat, pe_term, kn0, gmat, ctab, overlap, expand)

    y_s, fst_s = pl.pallas_call(
        _post_sample_kernel,
        out_shape=(jax.ShapeDtypeStruct((NS, D_MODEL), F32),
                   jax.ShapeDtypeStruct((NS, FFN_CONV_W - 1, D_FF), F32)),
        compiler_params=pltpu.CompilerParams(vmem_limit_bytes=VMEM_LIMIT), name="post_sample",
    )(xs, on_s.reshape(NS, 512), om_s.reshape(NS, 256), gm_s, yb_s, wonsa, wom, wout, nf, wup, fcw, fcb, wdn,
      cache_ffn[0])

    return (y_p, y_s.reshape(NS, 1, D_MODEL),
            rows_p.reshape(1, B, T, 4, N_KV_A, HEAD_DIM), rows_s.reshape(1, NS, 1, 4, N_KV_A, HEAD_DIM),
            win_p.reshape(1, B, WINDOW, 2, N_KV_A, HEAD_DIM), wo_s.reshape(1, NS, WINDOW, 2, N_KV_A, HEAD_DIM),
            cst_p[None], cst_s[None], fst_p[None], fst_s[None],
            mem_kv.reshape(1, B, N_MEM, 2, N_HEADS_M, HEAD_DIM))
```

```python
import functools

import jax
import jax.numpy as jnp
from jax import lax
from jax.experimental import pallas as pl
from jax.experimental.pallas import tpu as pltpu

F32 = jnp.float32
BF16 = jnp.bfloat16

D_MODEL = 1024
HEAD_DIM = 64
N_HEADS_A = 8
N_KV_A = 2
GROUP_A = 4
CMP_BLK = 32
CMP_STRIDE = 16
SLC_BLK = 64
N_SEL = 16
WINDOW = 512
C_CONV = 512
CONV_W = 31
N_MEM = 256
N_HEADS_M = 4
D_FF = 2816
FFN_CONV_W = 3
ROPE_THETA = 500000.0
ROPE_DIM = 16
EPS = 1e-6
BIG = 1e9
NEG = -1e30
PAGE = 128

C_Q = 0
C_KV = 512
C_GA = 1280
C_GLU = 1408
C_QM = 2432
C_GM = 2688
IN_PAD = 5760

LANES = 128
VMEM_LIMIT = 56 * 1024 * 1024

TM = 256
TQ = 256
TB = 2


def _dot(a, b):
    return jnp.dot(a, b, preferred_element_type=F32)


def _dot_t(a, b):
    return lax.dot_general(a, b, (((1,), (1,)), ((), ())), preferred_element_type=F32)


def _dot_hilo(x, m):
    hi = x.astype(BF16)
    lo = (x - hi.astype(F32)).astype(BF16)
    return _dot(hi, m) + _dot(lo, m)


def _rms(x, g):
    return x * lax.rsqrt(jnp.mean(x * x, axis=-1, keepdims=True) + EPS) * g


def _head_norm(blk, gain, gmat):
    ms = _dot_hilo(blk * blk, gmat)
    return blk * lax.rsqrt(ms + EPS) * gain


def _rope(blk, tab):
    c, s1, s2 = tab
    return blk * c + pltpu.roll(blk, 8, 1) * s1 + pltpu.roll(blk, LANES - 8, 1) * s2


def _softmax_rows(s):
    m = jnp.max(s, axis=-1, keepdims=True)
    p = jnp.exp(s - m)
    return p, jnp.sum(p, axis=-1, keepdims=True)


def _in_proj_common(h, w_ref, qn, kn1, kn2, gmat, tab):
    zq = _dot(h, w_ref[:, C_Q:C_Q + 512])
    q_blocks = []
    for cb in range(4):
        blk = zq[:, cb * LANES:(cb + 1) * LANES]
        q_blocks.append(_rope(_head_norm(blk, qn, gmat), tab) * (HEAD_DIM ** -0.5))
    q = jnp.concatenate(q_blocks, axis=1)
    zkv = _dot(h, w_ref[:, C_KV:C_KV + 768])
    k_slc = _rope(_head_norm(zkv[:, 256:384], kn1, gmat), tab)
    k_win = _rope(_head_norm(zkv[:, 512:640], kn2, gmat), tab)
    rows = jnp.concatenate([zkv[:, 0:256], k_slc, zkv[:, 384:512]], axis=1)
    win = jnp.concatenate([k_win, zkv[:, 640:768]], axis=1)
    ga = jax.nn.sigmoid(_dot(h, w_ref[:, C_GA:C_GA + LANES]))
    return q, rows, win, ga


def _glu(h, w_ref):
    z = _dot(h, w_ref[:, C_GLU:C_GLU + 1024])
    return z[:, :C_CONV] * jax.nn.sigmoid(z[:, C_CONV:])


def _conv_tail(c, lng, lnb, wo_ref):
    mu = jnp.mean(c, axis=-1, keepdims=True)
    var = jnp.mean(jnp.square(c - mu), axis=-1, keepdims=True)
    y = (c - mu) * lax.rsqrt(var + EPS) * lng + lnb
    return _dot(jax.nn.silu(y).astype(BF16), wo_ref[...])


def _mem_q(h, w_ref, mqn, gmat):
    z = _dot(h, w_ref[:, C_QM:C_QM + 256])
    return jnp.concatenate(
        [_head_norm(z[:, cb * LANES:(cb + 1) * LANES], mqn, gmat) for cb in range(2)], axis=1) * (HEAD_DIM ** -0.5)


def _ffn_tail(x1, nf, wup_ref, u_prev2, u_prev1_fn, fcw, fcb, wdn_ref):
    h2 = _rms(x1, nf).astype(BF16)
    up = _dot(h2, wup_ref[...])
    u = up[:, :D_FF]
    v = up[:, D_FF:]
    uc = fcw[0:1] * u_prev2(u) + fcw[1:2] * u_prev1_fn(u) + fcw[2:3] * u + fcb
    act = jax.nn.gelu(uc, approximate=True) * v
    return x1 + _dot(act.astype(BF16), wdn_ref[...]), u


def _memkv_kernel(mem_ref, nm_ref, w_ref, mkn_ref, gmat_ref, o_ref):
    h = _rms(mem_ref[0], nm_ref[...]).astype(BF16)
    z = _dot(h, w_ref[...])
    gmat = gmat_ref[...]
    k = [_head_norm(z[:, cb * LANES:(cb + 1) * LANES], mkn_ref[...], gmat) for cb in range(2)]
    o_ref[0] = jnp.concatenate(k + [z[:, 256:512]], axis=1)


def _pre_prompt_kernel(x_ref, na_ref, w_ref, qn_ref, kn1_ref, kn2_ref, gmat_ref, tab_ref,
                       cw_ref, cb_ref, lng_ref, lnb_ref, woc_ref, mkv_ref, mqn_ref, wom_ref,
                       q_ref, rows_ref, kk_ref, vt_ref, win_ref, ga_ref, gm0_ref, part_ref, cst_ref, hbuf):
    t = pl.program_id(1)
    gmat = gmat_ref[...]
    tab = (tab_ref[0], tab_ref[1], tab_ref[2])
    h = _rms(x_ref[0], na_ref[...]).astype(BF16)
    q, rows, win, ga = _in_proj_common(h, w_ref, qn_ref[...], kn1_ref[...], kn2_ref[...], gmat, tab)
    q_ref[0] = q.astype(BF16)
    rows_ref[0] = rows
    win_ref[0] = win
    ga_ref[0] = ga
    kk_ref[0] = jnp.concatenate([rows[:, 256:384], win[:, 0:128]], axis=1).astype(BF16)
    vt_ref[0, 0] = jnp.concatenate([rows[:, 384:512], win[:, 128:256]], axis=1).T.astype(BF16)

    glu = _glu(h, w_ref)

    @pl.when(t == 0)
    def _():
        hbuf[0:32, :] = jnp.zeros((32, C_CONV), F32)

    hbuf[32:32 + TM, :] = glu
    cw = cw_ref[...]
    c = jnp.zeros((TM, C_CONV), F32) + cb_ref[...]
    for k in range(CONV_W):
        c = c + cw[k:k + 1] * hbuf[pl.ds(k + 2, TM), :]
    cst_ref[0] = hbuf[pl.ds(TM + 2, CONV_W - 1), :]
    hbuf[0:32, :] = hbuf[TM:TM + 32, :]
    y_b = _conv_tail(c, lng_ref[...], lnb_ref[...], woc_ref)

    qm = _mem_q(h, w_ref, mqn_ref[...], gmat).astype(BF16)
    mkv = mkv_ref[0].astype(BF16)
    heads = []
    for hh in range(N_HEADS_M):
        s = _dot_t(qm[:, hh * 64:(hh + 1) * 64], mkv[:, hh * 64:(hh + 1) * 64])
        p, den = _softmax_rows(s)
        heads.append(_dot(p.astype(BF16), mkv[:, 256 + hh * 64:256 + (hh + 1) * 64]) / den)
    y_m = _dot(jnp.concatenate(heads, axis=1).astype(BF16), wom_ref[...])

    gm = jax.nn.sigmoid(_dot(h, w_ref[:, C_GM:C_GM + 3072]))
    gm0_ref[0] = gm[:, 0:1024].astype(BF16)
    part_ref[0] = (gm[:, 1024:2048] * y_b + gm[:, 2048:3072] * y_m).astype(BF16)


def _pe_term_kernel(pe_ref, wcat_ref, o_ref):
    acc = jnp.zeros((8, 256), F32)
    for l in range(16):
        top = jnp.broadcast_to(pe_ref[l:l + 1, :], (8, 256)).astype(BF16)
        bot = jnp.broadcast_to(pe_ref[l + 16:l + 17, :], (8, 256)).astype(BF16)
        acc = acc + _dot(top, wcat_ref[l][:, 0:256]) + _dot(bot, wcat_ref[l][:, 256:512])
    o_ref[...] = acc


def _compress(tap_fn, n_rows, wcat_ref, pe_term, kn0, gmat, ctab):
    acc = jnp.zeros((n_rows, 512), F32)
    for l in range(16):
        acc = acc + _dot(tap_fn(l).astype(BF16), wcat_ref[l])
    top = acc[:, 0:256]
    bot = pltpu.roll(acc[:, 256:512], n_rows - 1, 0)
    kcv = top + bot + pe_term
    kc = _rope(_head_norm(kcv[:, 0:128], kn0, gmat), ctab)
    return kc.astype(BF16), kcv[:, 128:256]


def _compress_prompt_kernel(kr_ref, vr_ref, wcat_ref, pet_ref, kn0_ref, gmat_ref, ctab_ref, kc_ref, vc_ref):
    def tap(l):
        return jnp.concatenate([r[0, pl.ds(l, 128, stride=16), :] for r in (kr_ref, vr_ref)], axis=1)

    kc, vc = _compress(tap, 128, wcat_ref, pet_ref[0:1, :],
                       kn0_ref[...], gmat_ref[...], (ctab_ref[0], ctab_ref[1], ctab_ref[2]))
    kc_ref[0] = kc
    vc_ref[0] = vc.T.astype(BF16)


def _select_cols(imp_t, tpos):
    n_slc = imp_t.shape[0]
    sidx = lax.broadcasted_iota(jnp.int32, imp_t.shape, 0)
    qblk = jnp.right_shift(tpos, 6)
    forced = (sidx == 0) | (sidx == qblk) | (sidx == qblk - 1)
    score = jnp.where(forced, BIG, jnp.where(sidx <= qblk, imp_t, -BIG))
    rank = jnp.zeros(imp_t.shape, F32)
    for s in range(n_slc):
        row = score[s:s + 1, :]
        tie = jnp.where(sidx > s, 1.0, 0.0)
        rank = rank + jnp.where(row > score, 1.0, jnp.where(row == score, tie, 0.0))
    return jnp.where((rank < N_SEL) & (score > -0.5 * BIG), 1.0, 0.0)


def _attn_prompt_kernel(q_ref, kk_ref, vt_ref, kc_ref, vct_ref, ga_ref, egt_ref, ovt_ref, ext_ref, o_ref):
    qt = pl.program_id(1)
    t0 = qt * TQ
    q = q_ref[0]
    tpos = t0 + lax.broadcasted_iota(jnp.int32, (1, TQ), 1)
    n_cmp = lax.broadcasted_iota(jnp.int32, (LANES, TQ), 0)
    cmask = jnp.where(((n_cmp * CMP_STRIDE + (CMP_BLK - 1)) <= tpos) & (n_cmp < 127), 1.0, 0.0)
    cbias = (cmask - 1.0) * (-NEG)
    cw0 = jnp.maximum(qt - 2, 0)
    w0 = pl.multiple_of(cw0 * TQ, TQ)
    wdiff = tpos - (w0 + lax.broadcasted_iota(jnp.int32, (3 * TQ, 1), 0))
    wbias = jnp.where((wdiff >= 0) & (wdiff <= WINDOW), 0.0, NEG)
    causal = jnp.where((t0 + lax.broadcasted_iota(jnp.int32, (TQ, 1), 0)) <= tpos, 0.0, NEG)
    ga_t = ga_ref[0].T
    ga_hi = ga_t.astype(BF16)
    ga_lo = (ga_t - ga_hi.astype(F32)).astype(BF16)
    gates_t = _dot(egt_ref[...], ga_hi) + _dot(egt_ref[...], ga_lo)

    def add4(s, b):
        return jnp.concatenate([s[:, j * TQ:(j + 1) * TQ] + b for j in range(GROUP_A)], axis=1)

    def with_ones(vt):
        return jnp.concatenate([vt, jnp.ones((16, vt.shape[1]), BF16)], axis=0)

    outs = [[], [], []]
    for g in range(N_KV_A):
        lo, hi = g * 64, (g + 1) * 64
        qs = jnp.concatenate([q[:, (4 * g + j) * 64:(4 * g + j + 1) * 64] for j in range(GROUP_A)], axis=0)

        s_cmp = add4(_dot_t(kc_ref[0][:, lo:hi], qs), cbias)
        s_win = add4(_dot_t(kk_ref[0, pl.ds(w0, 3 * TQ), 128 + lo:128 + hi], qs), wbias)

        p = jnp.exp(s_cmp - jnp.max(s_cmp, axis=0, keepdims=True))
        p = jnp.concatenate([p[:, j * TQ:(j + 1) * TQ] * cmask for j in range(GROUP_A)], axis=1)
        den = jnp.sum(p, axis=0, keepdims=True)
        p = p / jnp.where(den > 0.0, den, 1.0)
        o_cmp = _dot(vct_ref[0][lo:hi, :], p.astype(BF16))
        p4 = p[:, 0:TQ] + p[:, TQ:2 * TQ] + p[:, 2 * TQ:3 * TQ] + p[:, 3 * TQ:4 * TQ]
        p4_hi = p4.astype(BF16)
        p4_lo = (p4 - p4_hi.astype(F32)).astype(BF16)
        imp_t = _dot(ovt_ref[...], p4_hi) + _dot(ovt_ref[...], p4_lo)
        sel32 = _select_cols(imp_t[0:32, :], tpos)
        sel_t = jnp.concatenate([sel32, jnp.zeros((LANES - 32, TQ), F32)], axis=0).astype(BF16)

        def slc_scores(c, extra):
            bias = (_dot(ext_ref[c], sel_t) - 1.0) * (-NEG) + extra
            return add4(_dot_t(kk_ref[0, pl.ds(pl.multiple_of(c * TQ, TQ), TQ), lo:hi], qs), bias)

        s_first = slc_scores(0, jnp.where(qt == 0, causal, 0.0))

        pb = jnp.exp(s_win - jnp.max(s_win, axis=0, keepdims=True)).astype(BF16)
        o_win = jnp.zeros((80, GROUP_A * TQ), F32)
        for i in range(3):
            o_win = o_win + _dot(with_ones(vt_ref[0, cw0 + i, 128 + lo:128 + hi, :]), pb[i * TQ:(i + 1) * TQ, :])
        o_win = o_win[0:64, :] / o_win[64:65, :]

        def chunk(c, s, m_i, acc):
            m_new = jnp.maximum(m_i, jnp.max(s, axis=0, keepdims=True))
            p = jnp.exp(s - m_new).astype(BF16)
            acc = jnp.exp(m_i - m_new) * acc + _dot(with_ones(vt_ref[0, c, lo:hi, :]), p)
            return m_new, acc

        def body(c, carry):
            s_next = slc_scores(c + 1, jnp.where(c + 1 == qt, causal, 0.0))
            return (s_next,) + chunk(c, *carry)

        init = (s_first, jnp.full((1, GROUP_A * TQ), NEG, F32), jnp.zeros((80, GROUP_A * TQ), F32))
        carry = lax.fori_loop(0, qt, body, init)
        _, acc = chunk(qt, *carry)
        o_slc = acc[0:64, :] / acc[64:65, :]

        for c, o in enumerate((o_cmp, o_slc, o_win)):
            outs[c] += [o[:, j * TQ:(j + 1) * TQ] for j in range(GROUP_A)]

    o_t = jnp.zeros((512, TQ), F32)
    for c in range(3):
        o_t = o_t + gates_t[c * 512:(c + 1) * 512, :] * jnp.concatenate(outs[c], axis=0)
    o_ref[0] = o_t.T.astype(BF16)


def _post_prompt_kernel(x_ref, on_ref, gm0_ref, part_ref, wonsa_ref, wout_ref, nf_ref, wup_ref, fcw_ref,
                        fcb_ref, wdn_ref, y_ref, fst_ref, ubuf):
    t = pl.program_id(1)
    y_a = _dot(on_ref[0], wonsa_ref[...])
    merged = gm0_ref[0].astype(F32) * y_a + part_ref[0].astype(F32)
    x1 = x_ref[0] + _dot(merged.astype(BF16), wout_ref[...])

    @pl.when(t == 0)
    def _():
        ubuf[0:8, :] = jnp.zeros((8, D_FF), F32)

    def prev2(u):
        ubuf[8:8 + TM, :] = u
        return ubuf[pl.ds(6, TM), :]

    def prev1(u):
        return ubuf[pl.ds(7, TM), :]

    y, u = _ffn_tail(x1, nf_ref[...], wup_ref, prev2, prev1, fcw_ref[...], fcb_ref[...], wdn_ref)
    y_ref[0] = y
    fst_ref[0] = u[TM - 2:TM, :]
    ubuf[0:8, :] = ubuf[TM:TM + 8, :]


def _pre_sample_kernel(x_ref, na_ref, w_ref, qn_ref, kn1_ref, kn2_ref, gmat_ref, tab_ref,
                       cw_ref, cb_ref, lng_ref, lnb_ref, woc_ref, cc_ref, mqn_ref, eg_ref,
                       q_ref, rows_ref, win_ref, gae_ref, qm_ref, gm_ref, yb_ref, cst_ref):
    gmat = gmat_ref[...]
    tab = (tab_ref[0], tab_ref[1], tab_ref[2])
    h = _rms(x_ref[...], na_ref[...]).astype(BF16)
    q, rows, win, ga = _in_proj_common(h, w_ref, qn_ref[...], kn1_ref[...], kn2_ref[...], gmat, tab)
    q_ref[...] = q
    rows_ref[...] = rows
    win_ref[...] = win
    gae_ref[...] = _dot_hilo(ga, eg_ref[...])
    glu = _glu(h, w_ref)
    cw = cw_ref[...]
    c = cw[CONV_W - 1:CONV_W] * glu + cb_ref[...]
    for k in range(CONV_W - 1):
        c = c + cw[k:k + 1] * cc_ref[:, k, :]
    for k in range(CONV_W - 2):
        cst_ref[:, k, :] = cc_ref[:, k + 1, :]
    cst_ref[:, CONV_W - 2, :] = glu
    yb_ref[...] = _conv_tail(c, lng_ref[...], lnb_ref[...], woc_ref)
    qm_ref[...] = _mem_q(h, w_ref, mqn_ref[...], gmat)
    gm_ref[...] = jax.nn.sigmoid(_dot(h, w_ref[:, C_GM:C_GM + 3072]))


def _heads_to_lanes(o8):
    lane = lax.broadcasted_iota(jnp.int32, (1, LANES), 1)
    blocks = []
    for cb in range(4):
        a = o8[2 * cb:2 * cb + 1, :]
        b = o8[2 * cb + 1:2 * cb + 2, :]
        if cb // 2 == 1:
            a = pltpu.roll(a, 64, 1)
        else:
            b = pltpu.roll(b, 64, 1)
        blocks.append(jnp.where(lane < 64, a, b))
    return jnp.concatenate(blocks, axis=1)


def _attn_sample_kernel(pt_ref, *refs):
    del pt_ref
    page_refs = refs[:TB * 16 * 4]
    (q_ref, rn_ref, wn_ref, gae_ref, qm_ref, cw_ref, cm_ref, wcat_ref, pet_ref, kn0_ref, gmat_ref, ctab_ref,
     ov_ref, ex_ref, on_ref, om_ref, wo_ref, ks_scr, vs_scr) = refs[TB * 16 * 4:]
    lane = lax.broadcasted_iota(jnp.int32, (1, LANES), 1)
    row8 = lax.broadcasted_iota(jnp.int32, (8, LANES), 0)
    lane8 = lax.broadcasted_iota(jnp.int32, (8, LANES), 1)

    for i in range(TB * 16):
        ks_scr[i * PAGE:(i + 1) * PAGE, :] = page_refs[4 * i][0].T
        vs_scr[i * PAGE:(i + 1) * PAGE, :] = page_refs[4 * i + 1][0].T

    def tap(l):
        return jnp.concatenate([r[pl.ds(l, TB * 128, stride=16), :] for r in (ks_scr, vs_scr)], axis=1)

    kc_all, vc_all = _compress(tap, TB * 128, wcat_ref, pet_ref[0:1, :], kn0_ref[...], gmat_ref[...],
                               tuple(jnp.concatenate([ctab_ref[i]] * TB, axis=0) for i in range(3)))

    for tok in range(TB):
        k_pages = [page_refs[(tok * 16 + i) * 4 + 2] for i in range(16)]
        v_pages = [page_refs[(tok * 16 + i) * 4 + 3] for i in range(16)]
        q = q_ref[tok]
        rn = rn_ref[tok]
        wn = wn_ref[tok]
        gae = gae_ref[tok]
        q_rows = []
        for r in range(N_HEADS_A):
            piece = q[:, (r // 2) * LANES:(r // 2 + 1) * LANES]
            if (r % 2) != (r // 4):
                piece = pltpu.roll(piece, 64, 1)
            q_rows.append(jnp.where(jnp.right_shift(lane, 6) == (r // 4), piece, 0.0))
        q8 = jnp.concatenate(q_rows, axis=0).astype(BF16)
        q8f = q8.astype(F32)
        kc = kc_all[tok * 128:(tok + 1) * 128]
        vc = vc_all[tok * 128:(tok + 1) * 128].astype(BF16)

        s = jnp.where(lane8 < 127, _dot_t(q8, kc), NEG)
        p, den = _softmax_rows(s)
        p = p / den
        o_cmp = _dot(p.astype(BF16), vc)
        imp8 = _dot_hilo(p, ov_ref[...])

        sel_rows = []
        for g in range(N_KV_A):
            imp = jnp.sum(imp8[4 * g:4 * g + 4, :], axis=0, keepdims=True)
            forced = (lane == 0) | (lane == 31) | (lane == 32)
            score = jnp.where(lane < 33, jnp.where(forced, BIG, imp), -3.0 * BIG)
            a = jnp.broadcast_to(score, (LANES, LANES))
            b = a.T
            sub = lax.broadcasted_iota(jnp.int32, (LANES, LANES), 0)
            ln = lax.broadcasted_iota(jnp.int32, (LANES, LANES), 1)
            beats = (b > a) | ((b == a) & (sub < ln))
            rank = jnp.sum(jnp.where(beats, 1.0, 0.0), axis=0, keepdims=True)
            sel_rows.append(jnp.where((rank < N_SEL) & (lane < 33), 1.0, 0.0))
        sel8 = jnp.where(row8 < 4, sel_rows[0], sel_rows[1]).astype(BF16)

        s_parts = [_dot(q8, k_pages[i][0].astype(BF16)) for i in range(16)]
        s = jnp.concatenate(s_parts, axis=1)
        selx = _dot(sel8, ex_ref[...])
        s = jnp.where(selx > 0.5, s, NEG)
        s_new = jnp.sum(q8f * rn[:, 256:384].astype(BF16).astype(F32), axis=-1, keepdims=True)
        m = jnp.maximum(jnp.max(s, axis=-1, keepdims=True), s_new)
        p = jnp.exp(s - m)
        p_new = jnp.exp(s_new - m)
        den = jnp.sum(p, axis=-1, keepdims=True) + p_new
        o_slc = p_new.astype(BF16).astype(F32) * rn[:, 384:512].astype(BF16).astype(F32)
        for i in range(16):
            o_slc = o_slc + _dot_t(p[:, i * PAGE:(i + 1) * PAGE].astype(BF16), v_pages[i][0].astype(BF16))
        o_slc = o_slc / den

        cw = cw_ref[tok]
        s = _dot(q8, cw[0:128, :].astype(BF16))
        s_new = jnp.sum(q8f * wn[:, 0:128].astype(BF16).astype(F32), axis=-1, keepdims=True)
        m = jnp.maximum(jnp.max(s, axis=-1, keepdims=True), s_new)
        p = jnp.exp(s - m)
        p_new = jnp.exp(s_new - m)
        den = jnp.sum(p, axis=-1, keepdims=True) + p_new
        o_win = (_dot_t(p.astype(BF16), cw[128:256, :].astype(BF16))
                 + p_new.astype(BF16).astype(F32) * wn[:, 128:256].astype(BF16).astype(F32)) / den
        rolled = pltpu.roll(cw, WINDOW - 1, 1)
        last = lax.broadcasted_iota(jnp.int32, (1, WINDOW), 1) == WINDOW - 1
        wn_col = jnp.broadcast_to(wn, (LANES, 256)).T[:, 0:1]
        wo_ref[tok] = jnp.where(last, wn_col, rolled)

        on_ref[tok] = (gae[:, 0:512] * _heads_to_lanes(o_cmp) + gae[:, 512:1024] * _heads_to_lanes(o_slc)
                       + gae[:, 1024:1536] * _heads_to_lanes(o_win))

        cm = cm_ref[tok]
        rowm = lax.broadcasted_iota(jnp.int32, (8, 256), 0)
        headm = jnp.right_shift(lax.broadcasted_iota(jnp.int32, (8, 256), 1), 6)
        qm8 = jnp.where(rowm == headm, jnp.broadcast_to(qm_ref[tok], (8, 256)), 0.0).astype(BF16)
        s = _dot(qm8, cm[0:256, :].astype(BF16))
        p, den = _softmax_rows(s)
        o8 = _dot_t(p.astype(BF16), cm[256:512, :].astype(BF16)) / den
        om_ref[tok] = jnp.sum(jnp.where(rowm == headm, o8, 0.0), axis=0, keepdims=True)


def _post_sample_kernel(x_ref, on_ref, om_ref, gm_ref, yb_ref, wonsa_ref, wom_ref, wout_ref, nf_ref, wup_ref,
                        fcw_ref, fcb_ref, wdn_ref, cf_ref, y_ref, fst_ref):
    y_a = _dot(on_ref[...].astype(BF16), wonsa_ref[...])
    y_m = _dot(om_ref[...].astype(BF16), wom_ref[...])
    gm = gm_ref[...]
    merged = gm[:, 0:1024] * y_a + gm[:, 1024:2048] * yb_ref[...] + gm[:, 2048:3072] * y_m
    x1 = x_ref[...] + _dot(merged.astype(BF16), wout_ref[...])
    y, u = _ffn_tail(x1, nf_ref[...], wup_ref, lambda u: cf_ref[:, 0, :], lambda u: cf_ref[:, 1, :],
                     fcw_ref[...], fcb_ref[...], wdn_ref)
    y_ref[...] = y
    fst_ref[:, 0, :] = cf_ref[:, 1, :]
    fst_ref[:, 1, :] = u


def _rope_tables(pos):
    inv = ROPE_THETA ** (-jnp.arange(0, ROPE_DIM, 2, dtype=F32) / ROPE_DIM)
    ang = pos.astype(F32)[:, None] * inv
    cos, sin = jnp.cos(ang), jnp.sin(ang)
    n = pos.shape[0]
    one = jnp.ones((n, HEAD_DIM - ROPE_DIM), F32)
    z8 = jnp.zeros((n, 8), F32)
    z48 = jnp.zeros((n, HEAD_DIM - ROPE_DIM), F32)
    c = jnp.concatenate([cos, cos, one], axis=1)
    s1 = jnp.concatenate([z8, sin, z48], axis=1)
    s2 = jnp.concatenate([-sin, z8, z48], axis=1)
    return jnp.stack([jnp.tile(c, (1, 2)), jnp.tile(s1, (1, 2)), jnp.tile(s2, (1, 2))])


def _tile2(v):
    return jnp.tile(v.reshape(1, HEAD_DIM), (1, 2))


def _const(shape):
    nd = len(shape)
    return pl.BlockSpec(shape, lambda *_: (0,) * nd, pipeline_mode=pl.Buffered(1))


def _params(*sem):
    return pltpu.CompilerParams(dimension_semantics=sem, vmem_limit_bytes=VMEM_LIMIT)


def kernel(x_prompt, x_sample, cache_nsa, cache_win, cache_conv, cache_ffn, cache_mem, page_table, mem_prompt,
           norm_attn, w_in, q_norm, k_norm, cmp_pe, w_cmp, w_o_nsa, conv_w, conv_b, conv_ln_g, conv_ln_b, w_o_conv,
           norm_mem, w_mem_kv, mq_norm, mk_norm, w_o_mem, w_out, norm_ffn, w_ffn_up, ffn_conv_w, ffn_conv_b,
           w_ffn_down):
    B, T, _ = x_prompt.shape
    NS = x_sample.shape[0]
    n_pages = page_table.shape[1]
    assert w_in.shape[0] == 1 and T == 2048 and n_pages * PAGE == 2048 and cache_win.shape[2] == WINDOW
    nt = T // TM
    assert TQ == TM

    w_in0 = w_in[0]
    w_in_p = jnp.concatenate([w_in0[:, :1304], jnp.zeros((D_MODEL, 104), F32), w_in0[:, 1304:]],
                             axis=1).astype(BF16)
    na = norm_attn.reshape(1, D_MODEL)
    nf = norm_ffn.reshape(1, D_MODEL)
    nm = norm_mem.reshape(1, D_MODEL)
    qn, mqn, mkn = _tile2(q_norm[0]), _tile2(mq_norm[0]), _tile2(mk_norm[0])
    kn0, kn1, kn2 = _tile2(k_norm[0, 0]), _tile2(k_norm[0, 1]), _tile2(k_norm[0, 2])
    ii = jnp.arange(LANES)
    gmat = jnp.where((ii[:, None] // 64) == (ii[None, :] // 64), 1.0 / 64, 0.0).astype(BF16)
    tab_p = _rope_tables(jnp.arange(T))
    tab_s = _rope_tables(jnp.full((1,), n_pages * PAGE))
    ctab = _rope_tables(jnp.arange(128) * CMP_STRIDE + (CMP_BLK - 1))
    eye2 = jnp.eye(2, dtype=F32)
    w_l = jnp.einsum('klde,kK,gG->lkgdKGe', w_cmp[0], eye2, eye2).reshape(CMP_BLK, 256, 256)
    wcat = jnp.concatenate([w_l[:16], w_l[16:]], axis=-1).astype(BF16)
    pe2 = jnp.broadcast_to(cmp_pe[0].transpose(1, 0, 2)[:, :, None, :], (CMP_BLK, 2, 2, HEAD_DIM)).reshape(
        CMP_BLK, 256)
    col = jnp.arange(3 * 512)
    egate = (jnp.arange(LANES)[:, None] == ((col // 512) * 8 + (col % 512) // 64)[None, :]).astype(BF16)
    egate_t = egate.T
    cs = jnp.arange(LANES)[:, None] * CMP_STRIDE
    ss = jnp.arange(LANES)[None, :] * SLC_BLK
    overlap = ((cs < ss + SLC_BLK) & (cs + CMP_BLK > ss) & (jnp.arange(LANES)[:, None] < 127)
               & (jnp.arange(LANES)[None, :] < 33)).astype(BF16)
    kpos = jnp.arange(T)
    expand = ((jnp.arange(LANES)[:, None] == (kpos // SLC_BLK)[None, :])).astype(BF16)
    expand_ct = expand.T.reshape(T // TQ, TQ, LANES)
    woc = w_o_conv[0].astype(BF16)
    wom = w_o_mem[0].astype(BF16)
    wonsa = w_o_nsa[0].astype(BF16)
    wout = w_out[0].astype(BF16)
    wup = w_ffn_up[0].astype(BF16)
    wdn = w_ffn_down[0].astype(BF16)
    wmkv = w_mem_kv[0].astype(BF16)
    cw, cb = conv_w[0], conv_b.reshape(1, C_CONV)
    lng, lnb = conv_ln_g.reshape(1, C_CONV), conv_ln_b.reshape(1, C_CONV)
    fcw, fcb = ffn_conv_w[0], ffn_conv_b.reshape(1, D_FF)

    mem_kv = pl.pallas_call(
        _memkv_kernel, grid=(B,),
        in_specs=[pl.BlockSpec((1, N_MEM, D_MODEL), lambda b: (b, 0, 0)), _const((1, D_MODEL)),
                  _const((D_MODEL, 512)), _const((1, LANES)), _const((LANES, LANES))],
        out_specs=pl.BlockSpec((1, N_MEM, 512), lambda b: (b, 0, 0)),
        out_shape=jax.ShapeDtypeStruct((B, N_MEM, 512), F32),
        compiler_params=_params("arbitrary"), name="mem_kv",
    )(mem_prompt, nm, wmkv, mkn, gmat)

    def tile(width):
        return pl.BlockSpec((1, TM, width), lambda b, t: (b, t, 0))

    n_win_t = WINDOW // TM
    pre_out_shapes = (
        jax.ShapeDtypeStruct((B, T, 512), BF16),
        jax.ShapeDtypeStruct((B, T, 512), F32),
        jax.ShapeDtypeStruct((B, T, 256), BF16),
        jax.ShapeDtypeStruct((B, nt, 256, TM), BF16),
        jax.ShapeDtypeStruct((B, WINDOW, 256), F32),
        jax.ShapeDtypeStruct((B, T, LANES), F32),
        jax.ShapeDtypeStruct((B, T, D_MODEL), BF16),
        jax.ShapeDtypeStruct((B, T, D_MODEL), BF16),
        jax.ShapeDtypeStruct((B, CONV_W - 1, C_CONV), F32),
    )
    q_p, rows_p, kk_p, vt_p, win_p, ga_p, gm0_p, part_p, cst_p = pl.pallas_call(
        _pre_prompt_kernel, grid=(B, nt),
        in_specs=[tile(D_MODEL), _const((1, D_MODEL)), _const((D_MODEL, IN_PAD)), _const((1, LANES)),
                  _const((1, LANES)), _const((1, LANES)), _const((LANES, LANES)),
                  pl.BlockSpec((3, TM, LANES), lambda b, t: (0, t, 0)),
                  _const((CONV_W, C_CONV)), _const((1, C_CONV)), _const((1, C_CONV)), _const((1, C_CONV)),
                  _const((C_CONV, D_MODEL)),
                  pl.BlockSpec((1, N_MEM, 512), lambda b, t: (b, 0, 0)), _const((1, LANES)),
                  _const((256, D_MODEL))],
        out_specs=(tile(512), tile(512), tile(256),
                   pl.BlockSpec((1, 1, 256, TM), lambda b, t: (b, t, 0, 0)),
                   pl.BlockSpec((1, TM, 256), lambda b, t: (b, jnp.maximum(t - (nt - n_win_t), 0), 0)),
                   tile(LANES), tile(D_MODEL), tile(D_MODEL),
                   pl.BlockSpec((1, CONV_W - 1, C_CONV), lambda b, t: (b, 0, 0))),
        out_shape=pre_out_shapes,
        scratch_shapes=[pltpu.VMEM((TM + 32, C_CONV), F32)],
        compiler_params=_params("arbitrary", "arbitrary"), name="pre_prompt",
    )(x_prompt, na, w_in_p, qn, kn1, kn2, gmat, tab_p, cw, cb, lng, lnb, woc, mem_kv, mqn, wom)

    pe_term = pl.pallas_call(
        _pe_term_kernel, out_shape=jax.ShapeDtypeStruct((8, 256), F32), name="pe_term",
    )(pe2, wcat)
    kc_p, vc_p = pl.pallas_call(
        _compress_prompt_kernel, grid=(B,),
        in_specs=[pl.BlockSpec((1, T, LANES), lambda b: (b, 0, 0)), pl.BlockSpec((1, T, LANES), lambda b: (b, 0, 1)),
                  _const((16, 256, 512)), _const((8, 256)),
                  _const((1, LANES)), _const((LANES, LANES)), _const((3, LANES, LANES))],
        out_specs=(pl.BlockSpec((1, LANES, LANES), lambda b: (b, 0, 0)),) * 2,
        out_shape=(jax.ShapeDtypeStruct((B, LANES, LANES), BF16),) * 2,
        compiler_params=_params("arbitrary"), name="compress_prompt",
    )(rows_p, rows_p, wcat, pe_term, kn0, gmat, ctab)

    o_nsa_p = pl.pallas_call(
        _attn_prompt_kernel, grid=(B, T // TQ),
        in_specs=[pl.BlockSpec((1, TQ, 512), lambda b, t: (b, t, 0)),
                  pl.BlockSpec((1, T, 256), lambda b, t: (b, 0, 0)),
                  pl.BlockSpec((1, nt, 256, TM), lambda b, t: (b, 0, 0, 0)),
                  pl.BlockSpec((1, LANES, LANES), lambda b, t: (b, 0, 0)),
                  pl.BlockSpec((1, LANES, LANES), lambda b, t: (b, 0, 0)),
                  pl.BlockSpec((1, TQ, LANES), lambda b, t: (b, t, 0)),
                  _const((3 * 512, LANES)), _const((LANES, LANES)), _const((T // TQ, TQ, LANES))],
        out_specs=pl.BlockSpec((1, TQ, 512), lambda b, t: (b, t, 0)),
        out_shape=jax.ShapeDtypeStruct((B, T, 512), BF16),
        compiler_params=_params("arbitrary", "arbitrary"), name="attn_prompt",
    )(q_p, kk_p, vt_p, kc_p, vc_p, ga_p, egate_t, overlap.T, expand_ct)

    y_p, fst_p = pl.pallas_call(
        _post_prompt_kernel, grid=(B, nt),
        in_specs=[tile(D_MODEL), tile(512), tile(D_MODEL), tile(D_MODEL), _const((512, D_MODEL)),
                  _const((D_MODEL, D_MODEL)), _const((1, D_MODEL)), _const((D_MODEL, 2 * D_FF)),
                  _const((FFN_CONV_W, D_FF)), _const((1, D_FF)), _const((D_FF, D_MODEL))],
        out_specs=(tile(D_MODEL), pl.BlockSpec((1, FFN_CONV_W - 1, D_FF), lambda b, t: (b, 0, 0))),
        out_shape=(jax.ShapeDtypeStruct((B, T, D_MODEL), F32),
                   jax.ShapeDtypeStruct((B, FFN_CONV_W - 1, D_FF), F32)),
        scratch_shapes=[pltpu.VMEM((TM + 8, D_FF), F32)],
        compiler_params=_params("arbitrary", "arbitrary"), name="post_prompt",
    )(x_prompt, o_nsa_p, gm0_p, part_p, wonsa, wout, nf, wup, fcw, fcb, wdn)

    xs = x_sample.reshape(NS, D_MODEL)
    pre_s_shapes = (
        jax.ShapeDtypeStruct((NS, 512), F32), jax.ShapeDtypeStruct((NS, 512), F32),
        jax.ShapeDtypeStruct((NS, 256), F32), jax.ShapeDtypeStruct((NS, 3 * 512), F32),
        jax.ShapeDtypeStruct((NS, 256), F32), jax.ShapeDtypeStruct((NS, 3 * D_MODEL), F32),
        jax.ShapeDtypeStruct((NS, D_MODEL), F32), jax.ShapeDtypeStruct((NS, CONV_W - 1, C_CONV), F32),
    )
    q_s, rows_s, win_s, gae_s, qm_s, gm_s, yb_s, cst_s = pl.pallas_call(
        _pre_sample_kernel, out_shape=pre_s_shapes,
        compiler_params=pltpu.CompilerParams(vmem_limit_bytes=VMEM_LIMIT), name="pre_sample",
    )(xs, na, w_in_p, qn, kn1, kn2, gmat, tab_s, cw, cb, lng, lnb, woc, cache_conv[0], mqn, egate)

    pages = cache_nsa[0].transpose(0, 2, 3, 4, 1).reshape(cache_nsa.shape[1], 512, PAGE)
    cw_t = cache_win[0].transpose(0, 2, 3, 4, 1).reshape(NS, 256, WINDOW)
    cm_t = cache_mem[0].transpose(0, 2, 3, 4, 1).reshape(NS, 512, N_MEM)
    pt_flat = page_table.reshape(-1)

    def page_spec(tok, p, c):
        return pl.BlockSpec((1, LANES, PAGE), lambda i, pt: (pt[(i * TB + tok) * n_pages + p], c, 0))

    def tok_spec(width):
        return pl.BlockSpec((TB, 1, width), lambda i, pt: (i, 0, 0))

    def cst_spec(shape):
        nd = len(shape)
        return pl.BlockSpec(shape, lambda i, pt: (0,) * nd, pipeline_mode=pl.Buffered(1))

    grid_spec = pltpu.PrefetchScalarGridSpec(
        num_scalar_prefetch=1, grid=(NS // TB,),
        in_specs=[page_spec(tok, p, c) for tok in range(TB) for p in range(n_pages) for c in range(4)] + [
            tok_spec(512), tok_spec(512), tok_spec(256), tok_spec(3 * 512), tok_spec(256),
            pl.BlockSpec((TB, 256, WINDOW), lambda i, pt: (i, 0, 0)),
            pl.BlockSpec((TB, 512, N_MEM), lambda i, pt: (i, 0, 0)),
            cst_spec((16, 256, 512)), cst_spec((8, 256)), cst_spec((1, LANES)), cst_spec((LANES, LANES)),
            cst_spec((3, LANES, LANES)), cst_spec((LANES, LANES)), cst_spec((LANES, T))],
        out_specs=(tok_spec(512), tok_spec(256), pl.BlockSpec((TB, 256, WINDOW), lambda i, pt: (i, 0, 0))),
        scratch_shapes=[pltpu.VMEM((TB * n_pages * PAGE, LANES), F32)] * 2,
    )
    on_s, om_s, wo_s = pl.pallas_call(
        _attn_sample_kernel, grid_spec=grid_spec,
        out_shape=(jax.ShapeDtypeStruct((NS, 1, 512), F32), jax.ShapeDtypeStruct((NS, 1, 256), F32),
                   jax.ShapeDtypeStruct((NS, 256, WINDOW), F32)),
        compiler_params=_params("arbitrary"), name="attn_sample",
    )(pt_flat, *([pages] * (TB * n_pages * 4)), q_s.reshape(NS, 1, 512), rows_s.reshape(NS, 1, 512),
      win_s.reshape(NS, 1, 256), gae_s.reshape(NS, 1, 3 * 512), qm_s.reshape(NS, 1, 256),
      cw_t, cm_t, wcat, pe_term, kn0, gmat, ctab, overlap, expand)
    wo_s = wo_s.reshape(NS, 2, N_KV_A, HEAD_DIM, WINDOW).transpose(0, 4, 1, 2, 3)

    y_s, fst_s = pl.pallas_call(
        _post_sample_kernel,
        out_shape=(jax.ShapeDtypeStruct((NS, D_MODEL), F32),
                   jax.ShapeDtypeStruct((NS, FFN_CONV_W - 1, D_FF), F32)),
        compiler_params=pltpu.CompilerParams(vmem_limit_bytes=VMEM_LIMIT), name="post_sample",
    )(xs, on_s.reshape(NS, 512), om_s.reshape(NS, 256), gm_s, yb_s, wonsa, wom, wout, nf, wup, fcw, fcb, wdn,
      cache_ffn[0])

    return (y_p, y_s.reshape(NS, 1, D_MODEL),
            rows_p.reshape(1, B, T, 4, N_KV_A, HEAD_DIM), rows_s.reshape(1, NS, 1, 4, N_KV_A, HEAD_DIM),
            win_p.reshape(1, B, WINDOW, 2, N_KV_A, HEAD_DIM), wo_s[None],
            cst_p[None], cst_s[None], fst_p[None], fst_s[None],
            mem_kv.reshape(1, B, N_MEM, 2, N_HEADS_M, HEAD_DIM))
```

```python
import jax
import jax.numpy as jnp
from jax import lax
from jax.experimental import pallas as pl
from jax.experimental.pallas import tpu as pltpu

F32 = jnp.float32
BF16 = jnp.bfloat16

D_MODEL = 1024
HEAD_DIM = 64
N_HEADS_A = 8
N_KV_A = 2
GROUP_A = 4
CMP_BLK = 32
CMP_STRIDE = 16
SLC_BLK = 64
N_SEL = 16
WINDOW = 512
C_CONV = 512
CONV_W = 31
N_MEM = 256
N_HEADS_M = 4
D_FF = 2816
FFN_CONV_W = 3
ROPE_THETA = 500000.0
ROPE_DIM = 16
EPS = 1e-6
BIG = 1e9
NEG = -1e30
PAGE = 128

N_QKV = 1280
N_GATES = 24
C_GLU = 0
C_QM = 1024
C_GM = 1280
N_REST = 4352

LANES = 128
VMEM_LIMIT = 56 * 1024 * 1024

TM = 256
TQ = 256
TB = 2


def _dot(a, b):
    return jnp.dot(a, b, preferred_element_type=F32)


def _dot_t(a, b):
    return lax.dot_general(a, b, (((1,), (1,)), ((), ())), preferred_element_type=F32)


def _dot_hilo(x, m):
    hi = x.astype(BF16)
    lo = (x - hi.astype(F32)).astype(BF16)
    return _dot(hi, m) + _dot(lo, m)


def _rms(x, g):
    return x * lax.rsqrt(jnp.mean(x * x, axis=-1, keepdims=True) + EPS) * g


def _head_norm(blk, gain, gmat):
    ms = _dot_hilo(blk * blk, gmat)
    return blk * lax.rsqrt(ms + EPS) * gain


def _rope(blk, tab):
    c, s1, s2 = tab
    return blk * c + pltpu.roll(blk, 8, 1) * s1 + pltpu.roll(blk, LANES - 8, 1) * s2


def _softmax_rows(s):
    m = jnp.max(s, axis=-1, keepdims=True)
    p = jnp.exp(s - m)
    return p, jnp.sum(p, axis=-1, keepdims=True)


def _project_qkv(h, wa_ref, wg_ref, w_ref):
    return (_dot(h, wa_ref[:, 0:512]), _dot(h, wa_ref[:, 512:N_QKV]), _dot(h, wg_ref[...]),
            _dot(h, w_ref[:, C_GLU:C_GLU + 1024]))


def _project_rest(h, w_ref):
    return _dot(h, w_ref[:, C_QM:C_QM + 256]), _dot(h, w_ref[:, C_GM:C_GM + 3072])


def _in_proj_common(zq, zkv, zga, qn, kn1, kn2, gmat, tab):
    q_blocks = []
    for cb in range(4):
        blk = zq[:, cb * LANES:(cb + 1) * LANES]
        q_blocks.append(_rope(_head_norm(blk, qn, gmat), tab) * (HEAD_DIM ** -0.5))
    q = jnp.concatenate(q_blocks, axis=1)
    k_slc = _rope(_head_norm(zkv[:, 256:384], kn1, gmat), tab)
    k_win = _rope(_head_norm(zkv[:, 512:640], kn2, gmat), tab)
    rows = jnp.concatenate([zkv[:, 0:256], k_slc, zkv[:, 384:512]], axis=1)
    win = jnp.concatenate([k_win, zkv[:, 640:768]], axis=1)
    return q, rows, win, jax.nn.sigmoid(zga)


def _glu(z):
    return z[:, :C_CONV] * jax.nn.sigmoid(z[:, C_CONV:])


def _conv_tail(c, lng, lnb, wo_ref):
    mu = jnp.mean(c, axis=-1, keepdims=True)
    var = jnp.mean(jnp.square(c - mu), axis=-1, keepdims=True)
    y = (c - mu) * lax.rsqrt(var + EPS) * lng + lnb
    return _dot(jax.nn.silu(y).astype(BF16), wo_ref[...])


def _mem_q(z, mqn, gmat):
    return jnp.concatenate(
        [_head_norm(z[:, cb * LANES:(cb + 1) * LANES], mqn, gmat) for cb in range(2)], axis=1) * (HEAD_DIM ** -0.5)


def _ffn_tail(x1, nf, wup_ref, u_prev2, u_prev1_fn, fcw, fcb, wdn_ref):
    h2 = _rms(x1, nf).astype(BF16)
    up = _dot(h2, wup_ref[...])
    u = up[:, :D_FF]
    v = up[:, D_FF:]
    uc = fcw[0:1] * u_prev2(u) + fcw[1:2] * u_prev1_fn(u) + fcw[2:3] * u + fcb
    act = jax.nn.gelu(uc, approximate=True) * v
    return x1 + _dot(act.astype(BF16), wdn_ref[...]), u


def _memkv_kernel(mem_ref, nm_ref, w_ref, mkn_ref, gmat_ref, o_ref):
    h = _rms(mem_ref[0], nm_ref[...]).astype(BF16)
    z = _dot(h, w_ref[...])
    gmat = gmat_ref[...]
    k = [_head_norm(z[:, cb * LANES:(cb + 1) * LANES], mkn_ref[...], gmat) for cb in range(2)]
    o_ref[0] = jnp.concatenate(k + [z[:, 256:512]], axis=1)


def _pre_prompt_kernel(x_ref, na_ref, wa_ref, wg_ref, w_ref, qn_ref, kn1_ref, kn2_ref, gmat_ref, tab_ref,
                       cw_ref, cb_ref, lng_ref, lnb_ref, woc_ref, mkv_ref, mqn_ref, wom_ref,
                       q_ref, rows_ref, kk_ref, vt_ref, win_ref, ga_ref, gm0_ref, part_ref, cst_ref, hbuf, sbuf):
    @pl.when(pl.program_id(1) == 0)
    def _():
        hbuf[0:32, :] = jnp.zeros((32, C_CONV), F32)

    gmat = gmat_ref[...]
    tab = (tab_ref[0], tab_ref[1], tab_ref[2])
    h = _rms(x_ref[0], na_ref[...]).astype(BF16)
    zq, zkv, zga, zglu = _project_qkv(h, wa_ref, wg_ref, w_ref)
    q, rows, win, ga = _in_proj_common(zq, zkv, zga, qn_ref[...], kn1_ref[...], kn2_ref[...], gmat, tab)
    zqm, zgm = _project_rest(h, w_ref)
    q_ref[0] = q.astype(BF16)
    rows_ref[0] = rows
    win_ref[0] = win
    ga_ref[0] = ga
    kk_ref[0] = jnp.concatenate([rows[:, 256:384], win[:, 0:128]], axis=1).astype(BF16)
    vt_ref[0, 0] = jnp.concatenate([rows[:, 384:512], win[:, 128:256]], axis=1).T.astype(BF16)

    glu = _glu(zglu)
    hbuf[32:32 + TM, :] = glu
    cw = cw_ref[...]
    c = jnp.zeros((TM, C_CONV), F32) + cb_ref[...]
    for r in range(1, 8):
        sbuf[r - 1] = hbuf[pl.ds(r, TM + 24), :]
    for r in range(8):
        for a in range(5):
            k = 8 * a + r - 2
            if 0 <= k < CONV_W:
                src = hbuf[8 * a:8 * a + TM, :] if r == 0 else sbuf[r - 1, 8 * a:8 * a + TM, :]
                c = c + cw[k:k + 1] * src
    cst_ref[0] = hbuf[pl.ds(TM + 2, CONV_W - 1), :]
    hbuf[0:32, :] = hbuf[TM:TM + 32, :]
    y_b = _conv_tail(c, lng_ref[...], lnb_ref[...], woc_ref)

    qm = _mem_q(zqm, mqn_ref[...], gmat).astype(BF16)
    mkv = mkv_ref[0].astype(BF16)
    scores = [_dot_t(qm[:, hh * 64:(hh + 1) * 64], mkv[:, hh * 64:(hh + 1) * 64]) for hh in range(N_HEADS_M)]
    probs = [_softmax_rows(s) for s in scores]
    heads = [_dot(p.astype(BF16), mkv[:, 256 + hh * 64:256 + (hh + 1) * 64]) / den
             for hh, (p, den) in enumerate(probs)]
    y_m = _dot(jnp.concatenate(heads, axis=1).astype(BF16), wom_ref[...])

    gm = jax.nn.sigmoid(zgm)
    gm0_ref[0] = gm[:, 0:1024].astype(BF16)
    part_ref[0] = (gm[:, 1024:2048] * y_b + gm[:, 2048:3072] * y_m).astype(BF16)


def _pe_term_kernel(pe_ref, wcat_ref, o_ref):
    acc = jnp.zeros((8, 256), F32)
    for l in range(16):
        top = jnp.broadcast_to(pe_ref[l:l + 1, :], (8, 256)).astype(BF16)
        bot = jnp.broadcast_to(pe_ref[l + 16:l + 17, :], (8, 256)).astype(BF16)
        acc = acc + _dot(top, wcat_ref[l][:, 0:256]) + _dot(bot, wcat_ref[l][:, 256:512])
    o_ref[...] = acc


def _compress(tap_fn, n_rows, wcat_ref, pe_term, kn0, gmat, ctab):
    acc = jnp.zeros((n_rows, 512), F32)
    for l in range(16):
        acc = acc + _dot(tap_fn(l).astype(BF16), wcat_ref[l])
    top = acc[:, 0:256]
    bot = pltpu.roll(acc[:, 256:512], n_rows - 1, 0)
    kcv = top + bot + pe_term
    kc = _rope(_head_norm(kcv[:, 0:128], kn0, gmat), ctab)
    return kc.astype(BF16), kcv[:, 128:256]


def _compress_prompt_kernel(kr_ref, vr_ref, wcat_ref, pet_ref, kn0_ref, gmat_ref, ctab_ref, kc_ref, vc_ref):
    def tap(l):
        return jnp.concatenate([r[0, pl.ds(l, 128, stride=16), :] for r in (kr_ref, vr_ref)], axis=1)

    kc, vc = _compress(tap, 128, wcat_ref, pet_ref[0:1, :],
                       kn0_ref[...], gmat_ref[...], (ctab_ref[0], ctab_ref[1], ctab_ref[2]))
    kc_ref[0] = kc
    vc_ref[0] = vc.T.astype(BF16)


def _select_cols(imp_t, tpos):
    n_slc = imp_t.shape[0]
    sidx = lax.broadcasted_iota(jnp.int32, imp_t.shape, 0)
    qblk = jnp.right_shift(tpos, 6)
    forced = (sidx == 0) | (sidx == qblk) | (sidx == qblk - 1)
    score = jnp.where(forced, BIG, jnp.where(sidx <= qblk, imp_t, -BIG))
    rank = jnp.zeros(imp_t.shape, F32)
    for s in range(n_slc):
        row = score[s:s + 1, :]
        tie = jnp.where(sidx > s, 1.0, 0.0)
        rank = rank + jnp.where(row > score, 1.0, jnp.where(row == score, tie, 0.0))
    return jnp.where((rank < N_SEL) & (score > -0.5 * BIG), 1.0, 0.0)


def _attn_prompt_kernel(q_ref, kk_ref, vt_ref, kc_ref, vct_ref, ga_ref, egt_ref, ovt_ref, ext_ref, o_ref):
    qt = pl.program_id(1)
    t0 = qt * TQ
    q = q_ref[0]
    tpos = t0 + lax.broadcasted_iota(jnp.int32, (1, TQ), 1)
    n_cmp = lax.broadcasted_iota(jnp.int32, (LANES, TQ), 0)
    cmask = jnp.where(((n_cmp * CMP_STRIDE + (CMP_BLK - 1)) <= tpos) & (n_cmp < 127), 1.0, 0.0)
    cbias = (cmask - 1.0) * (-NEG)
    cw0 = jnp.maximum(qt - 2, 0)
    w0 = pl.multiple_of(cw0 * TQ, TQ)
    wdiff = tpos - (w0 + lax.broadcasted_iota(jnp.int32, (3 * TQ, 1), 0))
    wbias = jnp.where((wdiff >= 0) & (wdiff <= WINDOW), 0.0, NEG)
    causal = jnp.where((t0 + lax.broadcasted_iota(jnp.int32, (TQ, 1), 0)) <= tpos, 0.0, NEG)
    ga_t = ga_ref[0].T
    ga_hi = ga_t.astype(BF16)
    ga_lo = (ga_t - ga_hi.astype(F32)).astype(BF16)
    gates_t = _dot(egt_ref[...], ga_hi) + _dot(egt_ref[...], ga_lo)

    def add4(s, b):
        return jnp.concatenate([s[:, j * TQ:(j + 1) * TQ] + b for j in range(GROUP_A)], axis=1)

    def with_ones(vt):
        return jnp.concatenate([vt, jnp.ones((16, vt.shape[1]), BF16)], axis=0)

    outs = [[], [], []]
    for g in range(N_KV_A):
        lo, hi = g * 64, (g + 1) * 64
        qs = jnp.concatenate([q[:, (4 * g + j) * 64:(4 * g + j + 1) * 64] for j in range(GROUP_A)], axis=0)

        s_cmp = add4(_dot_t(kc_ref[0][:, lo:hi], qs), cbias)
        s_win = add4(_dot_t(kk_ref[0, pl.ds(w0, 3 * TQ), 128 + lo:128 + hi], qs), wbias)

        p = jnp.exp(s_cmp - jnp.max(s_cmp, axis=0, keepdims=True))
        p = jnp.concatenate([p[:, j * TQ:(j + 1) * TQ] * cmask for j in range(GROUP_A)], axis=1)
        den = jnp.sum(p, axis=0, keepdims=True)
        p = p / jnp.where(den > 0.0, den, 1.0)
        o_cmp = _dot(vct_ref[0][lo:hi, :], p.astype(BF16))
        p4 = p[:, 0:TQ] + p[:, TQ:2 * TQ] + p[:, 2 * TQ:3 * TQ] + p[:, 3 * TQ:4 * TQ]
        p4_hi = p4.astype(BF16)
        p4_lo = (p4 - p4_hi.astype(F32)).astype(BF16)
        imp_t = _dot(ovt_ref[...], p4_hi) + _dot(ovt_ref[...], p4_lo)
        sel32 = _select_cols(imp_t[0:32, :], tpos)
        sel_t = jnp.concatenate([sel32, jnp.zeros((LANES - 32, TQ), F32)], axis=0).astype(BF16)

        def slc_scores(c, extra):
            bias = (_dot(ext_ref[c], sel_t) - 1.0) * (-NEG) + extra
            return add4(_dot_t(kk_ref[0, pl.ds(pl.multiple_of(c * TQ, TQ), TQ), lo:hi], qs), bias)

        s_first = slc_scores(0, jnp.where(qt == 0, causal, 0.0))

        pb = jnp.exp(s_win - jnp.max(s_win, axis=0, keepdims=True)).astype(BF16)
        o_win = jnp.zeros((80, GROUP_A * TQ), F32)
        for i in range(3):
            o_win = o_win + _dot(with_ones(vt_ref[0, cw0 + i, 128 + lo:128 + hi, :]), pb[i * TQ:(i + 1) * TQ, :])
        o_win = o_win[0:64, :] / o_win[64:65, :]

        def chunk(c, s, m_i, acc):
            m_new = jnp.maximum(m_i, jnp.max(s, axis=0, keepdims=True))
            p = jnp.exp(s - m_new).astype(BF16)
            acc = jnp.exp(m_i - m_new) * acc + _dot(with_ones(vt_ref[0, c, lo:hi, :]), p)
            return m_new, acc

        def body(c, carry):
            s_next = slc_scores(c + 1, jnp.where(c + 1 == qt, causal, 0.0))
            return (s_next,) + chunk(c, *carry)

        init = (s_first, jnp.full((1, GROUP_A * TQ), NEG, F32), jnp.zeros((80, GROUP_A * TQ), F32))
        carry = lax.fori_loop(0, qt, body, init)
        _, acc = chunk(qt, *carry)
        o_slc = acc[0:64, :] / acc[64:65, :]

        for c, o in enumerate((o_cmp, o_slc, o_win)):
            outs[c] += [o[:, j * TQ:(j + 1) * TQ] for j in range(GROUP_A)]

    o_t = jnp.zeros((512, TQ), F32)
    for c in range(3):
        o_t = o_t + gates_t[c * 512:(c + 1) * 512, :] * jnp.concatenate(outs[c], axis=0)
    o_ref[0] = o_t.T.astype(BF16)


def _post_prompt_kernel(x_ref, on_ref, gm0_ref, part_ref, wonsa_ref, wout_ref, nf_ref, wup_ref, fcw_ref,
                        fcb_ref, wdn_ref, y_ref, fst_ref, ubuf):
    @pl.when(pl.program_id(1) == 0)
    def _():
        ubuf[0:8, :] = jnp.zeros((8, D_FF), F32)

    y_a = _dot(on_ref[0], wonsa_ref[...])
    merged = gm0_ref[0].astype(F32) * y_a + part_ref[0].astype(F32)
    x1 = x_ref[0] + _dot(merged.astype(BF16), wout_ref[...])

    def prev2(u):
        ubuf[8:8 + TM, :] = u
        return ubuf[pl.ds(6, TM), :]

    def prev1(u):
        return ubuf[pl.ds(7, TM), :]

    y, u = _ffn_tail(x1, nf_ref[...], wup_ref, prev2, prev1, fcw_ref[...], fcb_ref[...], wdn_ref)
    y_ref[0] = y
    fst_ref[0] = u[TM - 2:TM, :]
    ubuf[0:8, :] = ubuf[TM:TM + 8, :]


def _pre_sample_kernel(x_ref, na_ref, wa_ref, wg_ref, w_ref, qn_ref, kn1_ref, kn2_ref, gmat_ref, tab_ref,
                       cw_ref, cb_ref, lng_ref, lnb_ref, woc_ref, cc_ref, mqn_ref, eg_ref,
                       q_ref, rows_ref, win_ref, gae_ref, qm_ref, gm_ref, yb_ref, cst_ref):
    gmat = gmat_ref[...]
    tab = (tab_ref[0], tab_ref[1], tab_ref[2])
    h = _rms(x_ref[...], na_ref[...]).astype(BF16)
    zq, zkv, zga, zglu = _project_qkv(h, wa_ref, wg_ref, w_ref)
    q, rows, win, ga = _in_proj_common(zq, zkv, zga, qn_ref[...], kn1_ref[...], kn2_ref[...], gmat, tab)
    zqm, zgm = _project_rest(h, w_ref)
    q_ref[...] = q
    rows_ref[...] = rows
    win_ref[...] = win
    gae_ref[...] = _dot_hilo(ga, eg_ref[...])
    glu = _glu(zglu)
    cw = cw_ref[...]
    c = cw[CONV_W - 1:CONV_W] * glu + cb_ref[...]
    for k in range(CONV_W - 1):
        c = c + cw[k:k + 1] * cc_ref[k]
    for k in range(CONV_W - 2):
        cst_ref[k] = cc_ref[k + 1]
    cst_ref[CONV_W - 2] = glu
    yb_ref[...] = _conv_tail(c, lng_ref[...], lnb_ref[...], woc_ref)
    qm_ref[...] = _mem_q(zqm, mqn_ref[...], gmat)
    gm_ref[...] = jax.nn.sigmoid(zgm)


def _heads_to_lanes(o8):
    lane = lax.broadcasted_iota(jnp.int32, (1, LANES), 1)
    blocks = []
    for cb in range(4):
        a = o8[2 * cb:2 * cb + 1, :]
        b = o8[2 * cb + 1:2 * cb + 2, :]
        if cb // 2 == 1:
            a = pltpu.roll(a, 64, 1)
        else:
            b = pltpu.roll(b, 64, 1)
        blocks.append(jnp.where(lane < 64, a, b))
    return jnp.concatenate(blocks, axis=1)


def _attn_sample_kernel(pt_ref, *refs):
    del pt_ref
    page_refs = refs[:TB * 16]
    (q_ref, rn_ref, wn_ref, gae_ref, qm_ref, cw_ref, cm_ref, wcat_ref, pet_ref, kn0_ref, gmat_ref, ctab_ref,
     ov_ref, ex_ref, pm_ref, on_ref, om_ref, wo_ref) = refs[TB * 16:]
    lane = lax.broadcasted_iota(jnp.int32, (1, LANES), 1)
    row8 = lax.broadcasted_iota(jnp.int32, (8, LANES), 0)
    lane8 = lax.broadcasted_iota(jnp.int32, (8, LANES), 1)

    pm = pm_ref[...]
    taps = [[_dot_t(pm, page_refs[i][0, c * LANES:(c + 1) * LANES, :].astype(BF16)) for c in range(2)]
            for i in range(TB * 16)]

    def tap(l):
        return jnp.concatenate(
            [jnp.concatenate([taps[i][c][l * 8:(l + 1) * 8, :] for i in range(TB * 16)], axis=0) for c in range(2)],
            axis=1)

    kc_all, vc_all = _compress(tap, TB * 128, wcat_ref, pet_ref[0:1, :], kn0_ref[...], gmat_ref[...],
                               tuple(jnp.concatenate([ctab_ref[i]] * TB, axis=0) for i in range(3)))

    toks = range(TB)
    rowm = lax.broadcasted_iota(jnp.int32, (8, 256), 0)
    headm = jnp.right_shift(lax.broadcasted_iota(jnp.int32, (8, 256), 1), 6)

    def bf(x):
        return x.astype(BF16).astype(F32)

    q8s, s_slc, s_win, s_mem = [], [], [], []
    for tok in toks:
        q = q_ref[tok]
        q_rows = []
        for r in range(N_HEADS_A):
            piece = q[:, (r // 2) * LANES:(r // 2 + 1) * LANES]
            if (r % 2) != (r // 4):
                piece = pltpu.roll(piece, 64, 1)
            q_rows.append(jnp.where(jnp.right_shift(lane, 6) == (r // 4), piece, 0.0))
        q8 = jnp.concatenate(q_rows, axis=0).astype(BF16)
        q8s.append(q8)
        s_slc.append(jnp.concatenate(
            [_dot(q8, page_refs[tok * 16 + i][0, 256:384, :].astype(BF16)) for i in range(16)], axis=1))
        s_win.append(_dot(q8, cw_ref[tok, 0:128, :].astype(BF16)))
        qm8 = jnp.where(rowm == headm, jnp.broadcast_to(qm_ref[tok], (8, 256)), 0.0).astype(BF16)
        s_mem.append(_dot(qm8, cm_ref[tok, 0:256, :].astype(BF16)))

    o_cmp, imp8 = [], []
    for tok in toks:
        s = jnp.where(lane8 < 127, _dot_t(q8s[tok], kc_all[tok * 128:(tok + 1) * 128]), NEG)
        p, den = _softmax_rows(s)
        p = p / den
        o_cmp.append(_dot(p.astype(BF16), vc_all[tok * 128:(tok + 1) * 128].astype(BF16)))
        imp8.append(_dot_hilo(p, ov_ref[...]))

    o_win = []
    for tok in toks:
        wn = wn_ref[tok]
        s = s_win[tok]
        s_new = jnp.sum(q8s[tok].astype(F32) * bf(wn[:, 0:128]), axis=-1, keepdims=True)
        m = jnp.maximum(jnp.max(s, axis=-1, keepdims=True), s_new)
        p = jnp.exp(s - m)
        p_new = jnp.exp(s_new - m)
        den = jnp.sum(p, axis=-1, keepdims=True) + p_new
        o_win.append((_dot_t(p.astype(BF16), cw_ref[tok, 128:256, :].astype(BF16))
                      + bf(p_new) * bf(wn[:, 128:256])) / den)
        p, den = _softmax_rows(s_mem[tok])
        o8 = _dot_t(p.astype(BF16), cm_ref[tok, 256:512, :].astype(BF16)) / den
        om_ref[tok] = jnp.sum(jnp.where(rowm == headm, o8, 0.0), axis=0, keepdims=True)
        cw = cw_ref[tok]
        rolled = pltpu.roll(cw, WINDOW - 1, 1)
        last = lax.broadcasted_iota(jnp.int32, (1, WINDOW), 1) == WINDOW - 1
        wn_col = jnp.broadcast_to(wn, (LANES, 256)).T[:, 0:1]
        wo_ref[tok] = jnp.where(last, wn_col, rolled)

    selx = []
    for tok in toks:
        sel_rows = []
        for g in range(N_KV_A):
            imp = jnp.sum(imp8[tok][4 * g:4 * g + 4, :], axis=0, keepdims=True)
            forced = (lane == 0) | (lane == 31) | (lane == 32)
            score = jnp.where(lane < 33, jnp.where(forced, BIG, imp), -3.0 * BIG)
            a = jnp.broadcast_to(score, (LANES, LANES))
            b = a.T
            sub = lax.broadcasted_iota(jnp.int32, (LANES, LANES), 0)
            ln = lax.broadcasted_iota(jnp.int32, (LANES, LANES), 1)
            beats = (b > a) | ((b == a) & (sub < ln))
            rank = jnp.sum(jnp.where(beats, 1.0, 0.0), axis=0, keepdims=True)
            sel_rows.append(jnp.where((rank < N_SEL) & (lane < 33), 1.0, 0.0))
        sel8 = jnp.where(row8 < 4, sel_rows[0], sel_rows[1]).astype(BF16)
        selx.append(_dot(sel8, ex_ref[...]))

    for tok in toks:
        rn = rn_ref[tok]
        gae = gae_ref[tok]
        s = jnp.where(selx[tok] > 0.5, s_slc[tok], NEG)
        s_new = jnp.sum(q8s[tok].astype(F32) * bf(rn[:, 256:384]), axis=-1, keepdims=True)
        m = jnp.maximum(jnp.max(s, axis=-1, keepdims=True), s_new)
        p = jnp.exp(s - m)
        p_new = jnp.exp(s_new - m)
        den = jnp.sum(p, axis=-1, keepdims=True) + p_new
        o_slc = bf(p_new) * bf(rn[:, 384:512])
        for i in range(16):
            o_slc = o_slc + _dot_t(p[:, i * PAGE:(i + 1) * PAGE].astype(BF16),
                                   page_refs[tok * 16 + i][0, 384:512, :].astype(BF16))
        o_slc = o_slc / den
        on_ref[tok] = (gae[:, 0:512] * _heads_to_lanes(o_cmp[tok]) + gae[:, 512:1024] * _heads_to_lanes(o_slc)
                       + gae[:, 1024:1536] * _heads_to_lanes(o_win[tok]))


def _post_sample_kernel(x_ref, on_ref, om_ref, gm_ref, yb_ref, wonsa_ref, wom_ref, wout_ref, nf_ref, wup_ref,
                        fcw_ref, fcb_ref, wdn_ref, cf_ref, y_ref, fst_ref):
    y_a = _dot(on_ref[...].astype(BF16), wonsa_ref[...])
    y_m = _dot(om_ref[...].astype(BF16), wom_ref[...])
    gm = gm_ref[...]
    merged = gm[:, 0:1024] * y_a + gm[:, 1024:2048] * yb_ref[...] + gm[:, 2048:3072] * y_m
    x1 = x_ref[...] + _dot(merged.astype(BF16), wout_ref[...])
    y, u = _ffn_tail(x1, nf_ref[...], wup_ref, lambda u: cf_ref[:, 0, :], lambda u: cf_ref[:, 1, :],
                     fcw_ref[...], fcb_ref[...], wdn_ref)
    y_ref[...] = y
    fst_ref[:, 0, :] = cf_ref[:, 1, :]
    fst_ref[:, 1, :] = u


def _rope_tables(pos):
    inv = ROPE_THETA ** (-jnp.arange(0, ROPE_DIM, 2, dtype=F32) / ROPE_DIM)
    ang = pos.astype(F32)[:, None] * inv
    cos, sin = jnp.cos(ang), jnp.sin(ang)
    n = pos.shape[0]
    one = jnp.ones((n, HEAD_DIM - ROPE_DIM), F32)
    z8 = jnp.zeros((n, 8), F32)
    z48 = jnp.zeros((n, HEAD_DIM - ROPE_DIM), F32)
    c = jnp.concatenate([cos, cos, one], axis=1)
    s1 = jnp.concatenate([z8, sin, z48], axis=1)
    s2 = jnp.concatenate([-sin, z8, z48], axis=1)
    return jnp.stack([jnp.tile(c, (1, 2)), jnp.tile(s1, (1, 2)), jnp.tile(s2, (1, 2))])


def _tile2(v):
    return jnp.tile(v.reshape(1, HEAD_DIM), (1, 2))


def _const(shape):
    nd = len(shape)
    return pl.BlockSpec(shape, lambda *_: (0,) * nd, pipeline_mode=pl.Buffered(1))


def _params(*sem):
    return pltpu.CompilerParams(dimension_semantics=sem, vmem_limit_bytes=VMEM_LIMIT)


def kernel(x_prompt, x_sample, cache_nsa, cache_win, cache_conv, cache_ffn, cache_mem, page_table, mem_prompt,
           norm_attn, w_in, q_norm, k_norm, cmp_pe, w_cmp, w_o_nsa, conv_w, conv_b, conv_ln_g, conv_ln_b, w_o_conv,
           norm_mem, w_mem_kv, mq_norm, mk_norm, w_o_mem, w_out, norm_ffn, w_ffn_up, ffn_conv_w, ffn_conv_b,
           w_ffn_down):
    B, T, _ = x_prompt.shape
    NS = x_sample.shape[0]
    n_pages = page_table.shape[1]
    assert w_in.shape[0] == 1 and T == 2048 and n_pages * PAGE == 2048 and cache_win.shape[2] == WINDOW
    nt = T // TM
    assert TQ == TM

    w_in0 = w_in[0]
    w_qkv = w_in0[:, :N_QKV].astype(BF16)
    w_gate = jnp.pad(w_in0[:, N_QKV:N_QKV + N_GATES], ((0, 0), (0, LANES - N_GATES))).astype(BF16)
    w_rest = w_in0[:, N_QKV + N_GATES:].astype(BF16)
    na = norm_attn.reshape(1, D_MODEL)
    nf = norm_ffn.reshape(1, D_MODEL)
    nm = norm_mem.reshape(1, D_MODEL)
    qn, mqn, mkn = _tile2(q_norm[0]), _tile2(mq_norm[0]), _tile2(mk_norm[0])
    kn0, kn1, kn2 = _tile2(k_norm[0, 0]), _tile2(k_norm[0, 1]), _tile2(k_norm[0, 2])
    ii = jnp.arange(LANES)
    gmat = jnp.where((ii[:, None] // 64) == (ii[None, :] // 64), 1.0 / 64, 0.0).astype(BF16)
    tab_p = _rope_tables(jnp.arange(T))
    tab_s = _rope_tables(jnp.full((1,), n_pages * PAGE))
    ctab = _rope_tables(jnp.arange(128) * CMP_STRIDE + (CMP_BLK - 1))
    wk, wv = w_cmp[0, 0].astype(BF16), w_cmp[0, 1].astype(BF16)
    z = jnp.zeros((CMP_BLK, HEAD_DIM, HEAD_DIM), BF16)
    w_l = jnp.concatenate([jnp.concatenate(r, axis=-1) for r in
                           ([wk, z, z, z], [z, wk, z, z], [z, z, wv, z], [z, z, z, wv])], axis=1)
    wcat = jnp.concatenate([w_l[:16], w_l[16:]], axis=-1)
    pe2 = jnp.broadcast_to(cmp_pe[0].transpose(1, 0, 2)[:, :, None, :], (CMP_BLK, 2, 2, HEAD_DIM)).reshape(
        CMP_BLK, 256)
    col = jnp.arange(3 * 512)
    egate = (jnp.arange(LANES)[:, None] == ((col // 512) * 8 + (col % 512) // 64)[None, :]).astype(BF16)
    egate_t = egate.T
    cs = jnp.arange(LANES)[:, None] * CMP_STRIDE
    ss = jnp.arange(LANES)[None, :] * SLC_BLK
    overlap = ((cs < ss + SLC_BLK) & (cs + CMP_BLK > ss) & (jnp.arange(LANES)[:, None] < 127)
               & (jnp.arange(LANES)[None, :] < 33)).astype(BF16)
    kpos = jnp.arange(T)
    expand = ((jnp.arange(LANES)[:, None] == (kpos // SLC_BLK)[None, :])).astype(BF16)
    expand_ct = expand.T.reshape(T // TQ, TQ, LANES)
    woc = w_o_conv[0].astype(BF16)
    wom = w_o_mem[0].astype(BF16)
    wonsa = w_o_nsa[0].astype(BF16)
    wout = w_out[0].astype(BF16)
    wup = w_ffn_up[0].astype(BF16)
    wdn = w_ffn_down[0].astype(BF16)
    wmkv = w_mem_kv[0].astype(BF16)
    cw, cb = conv_w[0], conv_b.reshape(1, C_CONV)
    lng, lnb = conv_ln_g.reshape(1, C_CONV), conv_ln_b.reshape(1, C_CONV)
    fcw, fcb = ffn_conv_w[0], ffn_conv_b.reshape(1, D_FF)

    mem_kv = pl.pallas_call(
        _memkv_kernel, grid=(B,),
        in_specs=[pl.BlockSpec((1, N_MEM, D_MODEL), lambda b: (b, 0, 0)), _const((1, D_MODEL)),
                  _const((D_MODEL, 512)), _const((1, LANES)), _const((LANES, LANES))],
        out_specs=pl.BlockSpec((1, N_MEM, 512), lambda b: (b, 0, 0)),
        out_shape=jax.ShapeDtypeStruct((B, N_MEM, 512), F32),
        compiler_params=_params("arbitrary"), name="mem_kv",
    )(mem_prompt, nm, wmkv, mkn, gmat)

    def tile(width):
        return pl.BlockSpec((1, TM, width), lambda b, t: (b, t, 0))

    n_win_t = WINDOW // TM
    pre_out_shapes = (
        jax.ShapeDtypeStruct((B, T, 512), BF16),
        jax.ShapeDtypeStruct((B, T, 512), F32),
        jax.ShapeDtypeStruct((B, T, 256), BF16),
        jax.ShapeDtypeStruct((B, nt, 256, TM), BF16),
        jax.ShapeDtypeStruct((B, WINDOW, 256), F32),
        jax.ShapeDtypeStruct((B, T, LANES), F32),
        jax.ShapeDtypeStruct((B, T, D_MODEL), BF16),
        jax.ShapeDtypeStruct((B, T, D_MODEL), BF16),
        jax.ShapeDtypeStruct((B, CONV_W - 1, C_CONV), F32),
    )
    q_p, rows_p, kk_p, vt_p, win_p, ga_p, gm0_p, part_p, cst_p = pl.pallas_call(
        _pre_prompt_kernel, grid=(B, nt),
        in_specs=[tile(D_MODEL), _const((1, D_MODEL)), _const((D_MODEL, N_QKV)), _const((D_MODEL, LANES)),
                  _const((D_MODEL, N_REST)), _const((1, LANES)),
                  _const((1, LANES)), _const((1, LANES)), _const((LANES, LANES)),
                  pl.BlockSpec((3, TM, LANES), lambda b, t: (0, t, 0)),
                  _const((CONV_W, C_CONV)), _const((1, C_CONV)), _const((1, C_CONV)), _const((1, C_CONV)),
                  _const((C_CONV, D_MODEL)),
                  pl.BlockSpec((1, N_MEM, 512), lambda b, t: (b, 0, 0)), _const((1, LANES)),
                  _const((256, D_MODEL))],
        out_specs=(tile(512), tile(512), tile(256),
                   pl.BlockSpec((1, 1, 256, TM), lambda b, t: (b, t, 0, 0)),
                   pl.BlockSpec((1, TM, 256), lambda b, t: (b, jnp.maximum(t - (nt - n_win_t), 0), 0)),
                   tile(LANES), tile(D_MODEL), tile(D_MODEL),
                   pl.BlockSpec((1, CONV_W - 1, C_CONV), lambda b, t: (b, 0, 0))),
        out_shape=pre_out_shapes,
        scratch_shapes=[pltpu.VMEM((TM + 32, C_CONV), F32), pltpu.VMEM((7, TM + 24, C_CONV), F32)],
        compiler_params=_params("arbitrary", "arbitrary"), name="pre_prompt",
    )(x_prompt, na, w_qkv, w_gate, w_rest, qn, kn1, kn2, gmat, tab_p, cw, cb, lng, lnb, woc, mem_kv, mqn, wom)

    pe_term = pl.pallas_call(
        _pe_term_kernel, out_shape=jax.ShapeDtypeStruct((8, 256), F32), name="pe_term",
    )(pe2, wcat)
    kc_p, vc_p = pl.pallas_call(
        _compress_prompt_kernel, grid=(B,),
        in_specs=[pl.BlockSpec((1, T, LANES), lambda b: (b, 0, 0)), pl.BlockSpec((1, T, LANES), lambda b: (b, 0, 1)),
                  _const((16, 256, 512)), _const((8, 256)),
                  _const((1, LANES)), _const((LANES, LANES)), _const((3, LANES, LANES))],
        out_specs=(pl.BlockSpec((1, LANES, LANES), lambda b: (b, 0, 0)),) * 2,
        out_shape=(jax.ShapeDtypeStruct((B, LANES, LANES), BF16),) * 2,
        compiler_params=_params("arbitrary"), name="compress_prompt",
    )(rows_p, rows_p, wcat, pe_term, kn0, gmat, ctab)

    o_nsa_p = pl.pallas_call(
        _attn_prompt_kernel, grid=(B, T // TQ),
        in_specs=[pl.BlockSpec((1, TQ, 512), lambda b, t: (b, t, 0)),
                  pl.BlockSpec((1, T, 256), lambda b, t: (b, 0, 0)),
                  pl.BlockSpec((1, nt, 256, TM), lambda b, t: (b, 0, 0, 0)),
                  pl.BlockSpec((1, LANES, LANES), lambda b, t: (b, 0, 0)),
                  pl.BlockSpec((1, LANES, LANES), lambda b, t: (b, 0, 0)),
                  pl.BlockSpec((1, TQ, LANES), lambda b, t: (b, t, 0)),
                  _const((3 * 512, LANES)), _const((LANES, LANES)), _const((T // TQ, TQ, LANES))],
        out_specs=pl.BlockSpec((1, TQ, 512), lambda b, t: (b, t, 0)),
        out_shape=jax.ShapeDtypeStruct((B, T, 512), BF16),
        compiler_params=_params("arbitrary", "arbitrary"), name="attn_prompt",
    )(q_p, kk_p, vt_p, kc_p, vc_p, ga_p, egate_t, overlap.T, expand_ct)

    y_p, fst_p = pl.pallas_call(
        _post_prompt_kernel, grid=(B, nt),
        in_specs=[tile(D_MODEL), tile(512), tile(D_MODEL), tile(D_MODEL), _const((512, D_MODEL)),
                  _const((D_MODEL, D_MODEL)), _const((1, D_MODEL)), _const((D_MODEL, 2 * D_FF)),
                  _const((FFN_CONV_W, D_FF)), _const((1, D_FF)), _const((D_FF, D_MODEL))],
        out_specs=(tile(D_MODEL), pl.BlockSpec((1, FFN_CONV_W - 1, D_FF), lambda b, t: (b, 0, 0))),
        out_shape=(jax.ShapeDtypeStruct((B, T, D_MODEL), F32),
                   jax.ShapeDtypeStruct((B, FFN_CONV_W - 1, D_FF), F32)),
        scratch_shapes=[pltpu.VMEM((TM + 8, D_FF), F32)],
        compiler_params=_params("arbitrary", "arbitrary"), name="post_prompt",
    )(x_prompt, o_nsa_p, gm0_p, part_p, wonsa, wout, nf, wup, fcw, fcb, wdn)

    xs = x_sample.reshape(NS, D_MODEL)
    pre_s_shapes = (
        jax.ShapeDtypeStruct((NS, 512), F32), jax.ShapeDtypeStruct((NS, 512), F32),
        jax.ShapeDtypeStruct((NS, 256), F32), jax.ShapeDtypeStruct((NS, 3 * 512), F32),
        jax.ShapeDtypeStruct((NS, 256), F32), jax.ShapeDtypeStruct((NS, 3 * D_MODEL), F32),
        jax.ShapeDtypeStruct((NS, D_MODEL), F32), jax.ShapeDtypeStruct((CONV_W - 1, NS, C_CONV), F32),
    )
    q_s, rows_s, win_s, gae_s, qm_s, gm_s, yb_s, cst_s = pl.pallas_call(
        _pre_sample_kernel, out_shape=pre_s_shapes,
        compiler_params=pltpu.CompilerParams(vmem_limit_bytes=VMEM_LIMIT), name="pre_sample",
    )(xs, na, w_qkv, w_gate, w_rest, qn, kn1, kn2, gmat, tab_s, cw, cb, lng, lnb, woc,
      cache_conv[0].transpose(1, 0, 2), mqn, egate)
    cst_s = cst_s.transpose(1, 0, 2)

    pages = cache_nsa[0].transpose(0, 2, 3, 4, 1).reshape(cache_nsa.shape[1], 512, PAGE)
    cw_t = cache_win[0].transpose(0, 2, 3, 4, 1).reshape(NS, 256, WINDOW)
    cm_t = cache_mem[0].transpose(0, 2, 3, 4, 1).reshape(NS, 512, N_MEM)
    pt_flat = page_table.reshape(-1)

    def page_spec(tok, p):
        return pl.BlockSpec((1, 512, PAGE), lambda i, pt: (pt[(i * TB + tok) * n_pages + p], 0, 0))

    rr = jnp.arange(PAGE)
    perm = (rr[None, :] == ((rr % 8) * CMP_STRIDE + rr // 8)[:, None]).astype(BF16)

    def tok_spec(width):
        return pl.BlockSpec((TB, 1, width), lambda i, pt: (i, 0, 0))

    def cst_spec(shape):
        nd = len(shape)
        return pl.BlockSpec(shape, lambda i, pt: (0,) * nd, pipeline_mode=pl.Buffered(1))

    grid_spec = pltpu.PrefetchScalarGridSpec(
        num_scalar_prefetch=1, grid=(NS // TB,),
        in_specs=[page_spec(tok, p) for tok in range(TB) for p in range(n_pages)] + [
            tok_spec(512), tok_spec(512), tok_spec(256), tok_spec(3 * 512), tok_spec(256),
            pl.BlockSpec((TB, 256, WINDOW), lambda i, pt: (i, 0, 0)),
            pl.BlockSpec((TB, 512, N_MEM), lambda i, pt: (i, 0, 0)),
            cst_spec((16, 256, 512)), cst_spec((8, 256)), cst_spec((1, LANES)), cst_spec((LANES, LANES)),
            cst_spec((3, LANES, LANES)), cst_spec((LANES, LANES)), cst_spec((LANES, T)),
            cst_spec((PAGE, PAGE))],
        out_specs=(tok_spec(512), tok_spec(256), pl.BlockSpec((TB, 256, WINDOW), lambda i, pt: (i, 0, 0))),
    )
    on_s, om_s, wo_s = pl.pallas_call(
        _attn_sample_kernel, grid_spec=grid_spec,
        out_shape=(jax.ShapeDtypeStruct((NS, 1, 512), F32), jax.ShapeDtypeStruct((NS, 1, 256), F32),
                   jax.ShapeDtypeStruct((NS, 256, WINDOW), F32)),
        compiler_params=_params("arbitrary"), name="attn_sample",
    )(pt_flat, *([pages] * (TB * n_pages)), q_s.reshape(NS, 1, 512), rows_s.reshape(NS, 1, 512),
      win_s.reshape(NS, 1, 256), gae_s.reshape(NS, 1, 3 * 512), qm_s.reshape(NS, 1, 256),
      cw_t, cm_t, wcat, pe_term, kn0, gmat, ctab, overlap, expand, perm)
    wo_s = wo_s.reshape(NS, 2, N_KV_A, HEAD_DIM, WINDOW).transpose(0, 4, 1, 2, 3)

    y_s, fst_s = pl.pallas_call(
        _post_sample_kernel,
        out_shape=(jax.ShapeDtypeStruct((NS, D_MODEL), F32),
                   jax.ShapeDtypeStruct((NS, FFN_CONV_W - 1, D_FF), F32)),
        compiler_params=pltpu.CompilerParams(vmem_limit_bytes=VMEM_LIMIT), name="post_sample",
    )(xs, on_s.reshape(NS, 512), om_s.reshape(NS, 256), gm_s, yb_s, wonsa, wom, wout, nf, wup, fcw, fcb, wdn,
      cache_ffn[0])

    return (y_p, y_s.reshape(NS, 1, D_MODEL),
            rows_p.reshape(1, B, T, 4, N_KV_A, HEAD_DIM), rows_s.reshape(1, NS, 1, 4, N_KV_A, HEAD_DIM),
            win_p.reshape(1, B, WINDOW, 2, N_KV_A, HEAD_DIM), wo_s[None],
            cst_p[None], cst_s[None], fst_p[None], fst_s[None],
            mem_kv.reshape(1, B, N_MEM, 2, N_HEADS_M, HEAD_DIM))
```

```python
import jax
import jax.numpy as jnp
from jax import lax
from jax.experimental import pallas as pl
from jax.experimental.pallas import tpu as pltpu

F32 = jnp.float32
BF16 = jnp.bfloat16

D_MODEL = 1024
HEAD_DIM = 64
N_HEADS_A = 8
N_KV_A = 2
GROUP_A = 4
CMP_BLK = 32
CMP_STRIDE = 16
SLC_BLK = 64
N_SEL = 16
WINDOW = 512
C_CONV = 512
CONV_W = 31
N_MEM = 256
N_HEADS_M = 4
D_FF = 2816
FFN_CONV_W = 3
ROPE_THETA = 500000.0
ROPE_DIM = 16
EPS = 1e-6
BIG = 1e9
NEG = -1e30
PAGE = 128

N_QKV = 1280
N_GATES = 24
C_GLU = 0
C_QM = 1024
C_GM = 1280
N_REST = 4352

LANES = 128
VMEM_LIMIT = 56 * 1024 * 1024

TM = 256
TQ = 256
TB = 2
KB = 128


def _dot(a, b):
    return jnp.dot(a, b, preferred_element_type=F32)


def _dot_t(a, b):
    return lax.dot_general(a, b, (((1,), (1,)), ((), ())), preferred_element_type=F32)


def _dot_hilo(x, m):
    hi = x.astype(BF16)
    lo = (x - hi.astype(F32)).astype(BF16)
    return _dot(hi, m) + _dot(lo, m)


def _rms(x, g):
    return x * lax.rsqrt(jnp.mean(x * x, axis=-1, keepdims=True) + EPS) * g


def _head_norm(blk, gain, gmat):
    ms = _dot_hilo(blk * blk, gmat)
    return blk * lax.rsqrt(ms + EPS) * gain


def _rope(blk, tab):
    c, s1, s2 = tab
    return blk * c + pltpu.roll(blk, 8, 1) * s1 + pltpu.roll(blk, LANES - 8, 1) * s2


def _softmax_rows(s):
    m = jnp.max(s, axis=-1, keepdims=True)
    p = jnp.exp(s - m)
    return p, jnp.sum(p, axis=-1, keepdims=True)


def _project_qkv(h, wa_ref, wg_ref, w_ref):
    return (_dot(h, wa_ref[:, 0:512]), _dot(h, wa_ref[:, 512:N_QKV]), _dot(h, wg_ref[...]),
            _dot(h, w_ref[:, C_GLU:C_GLU + 1024]))


def _project_rest(h, w_ref):
    return _dot(h, w_ref[:, C_QM:C_QM + 256]), _dot(h, w_ref[:, C_GM:C_GM + 3072])


def _in_proj_common(zq, zkv, zga, qn, kn1, kn2, gmat, tab):
    q_blocks = []
    for cb in range(4):
        blk = zq[:, cb * LANES:(cb + 1) * LANES]
        q_blocks.append(_rope(_head_norm(blk, qn, gmat), tab) * (HEAD_DIM ** -0.5))
    q = jnp.concatenate(q_blocks, axis=1)
    k_slc = _rope(_head_norm(zkv[:, 256:384], kn1, gmat), tab)
    k_win = _rope(_head_norm(zkv[:, 512:640], kn2, gmat), tab)
    rows = jnp.concatenate([zkv[:, 0:256], k_slc, zkv[:, 384:512]], axis=1)
    win = jnp.concatenate([k_win, zkv[:, 640:768]], axis=1)
    return q, rows, win, jax.nn.sigmoid(zga)


def _glu(z):
    return z[:, :C_CONV] * jax.nn.sigmoid(z[:, C_CONV:])


def _conv_tail(c, lng, lnb, wo_ref):
    mu = jnp.mean(c, axis=-1, keepdims=True)
    var = jnp.mean(jnp.square(c - mu), axis=-1, keepdims=True)
    y = (c - mu) * lax.rsqrt(var + EPS) * lng + lnb
    return _dot(jax.nn.silu(y).astype(BF16), wo_ref[...])


def _mem_q(z, mqn, gmat):
    return jnp.concatenate(
        [_head_norm(z[:, cb * LANES:(cb + 1) * LANES], mqn, gmat) for cb in range(2)], axis=1) * (HEAD_DIM ** -0.5)


def _ffn_tail(x1, nf, wup_ref, u_prev2, u_prev1_fn, fcw, fcb, wdn_ref):
    h2 = _rms(x1, nf).astype(BF16)
    up = _dot(h2, wup_ref[...])
    u = up[:, :D_FF]
    v = up[:, D_FF:]
    uc = fcw[0:1] * u_prev2(u) + fcw[1:2] * u_prev1_fn(u) + fcw[2:3] * u + fcb
    act = jax.nn.gelu(uc, approximate=True) * v
    return x1 + _dot(act.astype(BF16), wdn_ref[...]), u


def _memkv_kernel(mem_ref, nm_ref, w_ref, mkn_ref, gmat_ref, o_ref):
    h = _rms(mem_ref[0], nm_ref[...]).astype(BF16)
    z = _dot(h, w_ref[...])
    gmat = gmat_ref[...]
    k = [_head_norm(z[:, cb * LANES:(cb + 1) * LANES], mkn_ref[...], gmat) for cb in range(2)]
    o_ref[0] = jnp.concatenate(k + [z[:, 256:512]], axis=1)


def _pre_prompt_kernel(x_ref, na_ref, wa_ref, wg_ref, w_ref, qn_ref, kn1_ref, kn2_ref, gmat_ref, tab_ref,
                       cw_ref, cb_ref, lng_ref, lnb_ref, woc_ref, mkv_ref, mqn_ref, wom_ref,
                       q_ref, rows_ref, kk_ref, vt_ref, win_ref, ga_ref, gm0_ref, part_ref, cst_ref, hbuf, sbuf):
    @pl.when(pl.program_id(1) == 0)
    def _():
        hbuf[0:32, :] = jnp.zeros((32, C_CONV), F32)

    gmat = gmat_ref[...]
    tab = (tab_ref[0], tab_ref[1], tab_ref[2])
    h = _rms(x_ref[0], na_ref[...]).astype(BF16)
    zq, zkv, zga, zglu = _project_qkv(h, wa_ref, wg_ref, w_ref)
    zqm = _dot(h, w_ref[:, C_QM:C_QM + 256])

    glu = _glu(zglu)
    hbuf[32:32 + TM, :] = glu
    cw = cw_ref[...]
    c = jnp.zeros((TM, C_CONV), F32) + cb_ref[...]
    zgm_chunks = []
    for r in range(8):
        if r < 6:
            zgm_chunks.append(_dot(h, w_ref[:, C_GM + r * 512:C_GM + (r + 1) * 512]))
        if r > 0:
            sbuf[r - 1] = hbuf[pl.ds(r, TM + 24), :]
        for a in range(5):
            k = 8 * a + r - 2
            if 0 <= k < CONV_W:
                src = hbuf[8 * a:8 * a + TM, :] if r == 0 else sbuf[r - 1, 8 * a:8 * a + TM, :]
                c = c + cw[k:k + 1] * src
    zgm = jnp.concatenate(zgm_chunks, axis=1)
    cst_ref[0] = hbuf[pl.ds(TM + 2, CONV_W - 1), :]
    hbuf[0:32, :] = hbuf[TM:TM + 32, :]

    q, rows, win, ga = _in_proj_common(zq, zkv, zga, qn_ref[...], kn1_ref[...], kn2_ref[...], gmat, tab)
    q_ref[0] = q.astype(BF16)
    rows_ref[0] = rows
    win_ref[0] = win
    ga_ref[0] = ga
    kk_ref[0] = jnp.concatenate([rows[:, 256:384], win[:, 0:128]], axis=1).astype(BF16)
    vt_ref[0, 0] = jnp.concatenate([rows[:, 384:512], win[:, 128:256]], axis=1).T.astype(BF16)

    y_b = _conv_tail(c, lng_ref[...], lnb_ref[...], woc_ref)

    qm = _mem_q(zqm, mqn_ref[...], gmat).astype(BF16)
    mkv = mkv_ref[0].astype(BF16)
    scores = [_dot_t(qm[:, hh * 64:(hh + 1) * 64], mkv[:, hh * 64:(hh + 1) * 64]) for hh in range(N_HEADS_M)]
    probs = [_softmax_rows(s) for s in scores]
    heads = [_dot(p.astype(BF16), mkv[:, 256 + hh * 64:256 + (hh + 1) * 64]) / den
             for hh, (p, den) in enumerate(probs)]
    y_m = _dot(jnp.concatenate(heads, axis=1).astype(BF16), wom_ref[...])

    gm = jax.nn.sigmoid(zgm)
    gm0_ref[0] = gm[:, 0:1024].astype(BF16)
    part_ref[0] = (gm[:, 1024:2048] * y_b + gm[:, 2048:3072] * y_m).astype(BF16)


def _pe_term_kernel(pe_ref, wcat_ref, o_ref):
    acc = jnp.zeros((8, 256), F32)
    for l in range(16):
        top = jnp.broadcast_to(pe_ref[l:l + 1, :], (8, 256)).astype(BF16)
        bot = jnp.broadcast_to(pe_ref[l + 16:l + 17, :], (8, 256)).astype(BF16)
        acc = acc + _dot(top, wcat_ref[l][:, 0:256]) + _dot(bot, wcat_ref[l][:, 256:512])
    o_ref[...] = acc


def _compress(tap_fn, n_rows, wcat_ref, pe_term, kn0, gmat, ctab):
    acc = jnp.zeros((n_rows, 512), F32)
    for l in range(16):
        acc = acc + _dot(tap_fn(l).astype(BF16), wcat_ref[l])
    top = acc[:, 0:256]
    bot = pltpu.roll(acc[:, 256:512], n_rows - 1, 0)
    kcv = top + bot + pe_term
    kc = _rope(_head_norm(kcv[:, 0:128], kn0, gmat), ctab)
    return kc.astype(BF16), kcv[:, 128:256]


def _compress_prompt_kernel(kr_ref, vr_ref, wcat_ref, pet_ref, kn0_ref, gmat_ref, ctab_ref, kc_ref, vc_ref):
    def tap(l):
        return jnp.concatenate([r[0, pl.ds(l, 128, stride=16), :] for r in (kr_ref, vr_ref)], axis=1)

    kc, vc = _compress(tap, 128, wcat_ref, pet_ref[0:1, :],
                       kn0_ref[...], gmat_ref[...], (ctab_ref[0], ctab_ref[1], ctab_ref[2]))
    kc_ref[0] = kc
    vc_ref[0] = vc.T.astype(BF16)


def _select_cols(imp_t, tpos):
    n_slc = imp_t.shape[0]
    sidx = lax.broadcasted_iota(jnp.int32, imp_t.shape, 0)
    qblk = jnp.right_shift(tpos, 6)
    forced = (sidx == 0) | (sidx == qblk) | (sidx == qblk - 1)
    score = jnp.where(forced, BIG, jnp.where(sidx <= qblk, imp_t, -BIG))
    rank = jnp.zeros(imp_t.shape, F32)
    for s in range(n_slc):
        row = score[s:s + 1, :]
        tie = jnp.where(sidx > s, 1.0, 0.0)
        rank = rank + jnp.where(row > score, 1.0, jnp.where(row == score, tie, 0.0))
    return jnp.where((rank < N_SEL) & (score > -0.5 * BIG), 1.0, 0.0)


def _attn_prompt_kernel(q_ref, kk_ref, vt_ref, kc_ref, vct_ref, ga_ref, egt_ref, ovt_ref, ext_ref, o_ref):
    qt = pl.program_id(1)
    t0 = qt * TQ
    q = q_ref[0]
    tpos = t0 + lax.broadcasted_iota(jnp.int32, (1, TQ), 1)
    n_cmp = lax.broadcasted_iota(jnp.int32, (LANES, TQ), 0)
    cmask = jnp.where(((n_cmp * CMP_STRIDE + (CMP_BLK - 1)) <= tpos) & (n_cmp < 127), 1.0, 0.0)
    cbias = (cmask - 1.0) * (-NEG)
    cw0 = jnp.maximum(qt - 2, 0)
    w0 = pl.multiple_of(cw0 * TQ, TQ)
    wdiff = tpos - (w0 + lax.broadcasted_iota(jnp.int32, (3 * TQ, 1), 0))
    wbias = jnp.where((wdiff >= 0) & (wdiff <= WINDOW), 0.0, NEG)
    causal = jnp.where((t0 + lax.broadcasted_iota(jnp.int32, (TQ, 1), 0)) <= tpos, 0.0, NEG)
    ga_t = ga_ref[0].T
    ga_hi = ga_t.astype(BF16)
    ga_lo = (ga_t - ga_hi.astype(F32)).astype(BF16)
    gates_t = _dot(egt_ref[...], ga_hi) + _dot(egt_ref[...], ga_lo)

    def add4(s, b):
        return jnp.concatenate([s[:, j * TQ:(j + 1) * TQ] + b for j in range(GROUP_A)], axis=1)

    def with_ones(vt):
        return jnp.concatenate([vt, jnp.ones((16, vt.shape[1]), BF16)], axis=0)

    def blk_scores(k_rows, qh, bias):
        cols, maxs = [], []
        for j in range(GROUP_A):
            blocks = [_dot_t(k_rows[i * KB:(i + 1) * KB, :], qh[j]) + bias[i * KB:(i + 1) * KB, :]
                      for i in range(k_rows.shape[0] // KB)]
            mx = jnp.max(blocks[0], axis=0, keepdims=True)
            for b in blocks[1:]:
                mx = jnp.maximum(mx, jnp.max(b, axis=0, keepdims=True))
            cols.append(jnp.concatenate(blocks, axis=0))
            maxs.append(mx)
        return jnp.concatenate(cols, axis=1), jnp.concatenate(maxs, axis=1)

    def blk_pv(vt_blk, s, m):
        cols = []
        for j in range(GROUP_A):
            o = jnp.zeros((80, TQ), F32)
            for i in range(s.shape[0] // KB):
                p = jnp.exp(s[i * KB:(i + 1) * KB, j * TQ:(j + 1) * TQ] - m[:, j * TQ:(j + 1) * TQ])
                o = o + _dot(vt_blk(i), p.astype(BF16))
            cols.append(o)
        return jnp.concatenate(cols, axis=1)

    groups = range(N_KV_A)
    qhs = [[q[:, (4 * g + j) * 64:(4 * g + j + 1) * 64] for j in range(GROUP_A)] for g in groups]
    s_cmps = [add4(_dot_t(kc_ref[0][:, g * 64:(g + 1) * 64], jnp.concatenate(qhs[g], axis=0)), cbias)
              for g in groups]
    wins = [blk_scores(kk_ref[0, pl.ds(w0, 3 * TQ), 128 + g * 64:128 + (g + 1) * 64], qhs[g], wbias)
            for g in groups]

    o_cmps, sel_ts, o_wins = [], [], []
    for g in groups:
        lo, hi = g * 64, (g + 1) * 64
        s_cmp = s_cmps[g]
        p = jnp.exp(s_cmp - jnp.max(s_cmp, axis=0, keepdims=True))
        p = jnp.concatenate([p[:, j * TQ:(j + 1) * TQ] * cmask for j in range(GROUP_A)], axis=1)
        den = jnp.sum(p, axis=0, keepdims=True)
        p = p / jnp.where(den > 0.0, den, 1.0)
        o_cmps.append(_dot(vct_ref[0][lo:hi, :], p.astype(BF16)))
        p4 = p[:, 0:TQ] + p[:, TQ:2 * TQ] + p[:, 2 * TQ:3 * TQ] + p[:, 3 * TQ:4 * TQ]
        p4_hi = p4.astype(BF16)
        p4_lo = (p4 - p4_hi.astype(F32)).astype(BF16)
        imp_t = _dot(ovt_ref[...], p4_hi) + _dot(ovt_ref[...], p4_lo)
        sel32 = _select_cols(imp_t[0:32, :], tpos)
        sel_ts.append(jnp.concatenate([sel32, jnp.zeros((LANES - 32, TQ), F32)], axis=0).astype(BF16))

    for g in groups:
        s_win, m_win = wins[g]
        o_win = blk_pv(lambda i: with_ones(vt_ref[0, cw0 + i // 2, 128 + g * 64:128 + (g + 1) * 64,
                                                  (i % 2) * KB:(i % 2 + 1) * KB]), s_win, m_win)
        o_wins.append(o_win[0:64, :] / o_win[64:65, :])

    def body(c, carry):
        extra = jnp.where(c == qt, causal, 0.0)
        k0 = pl.multiple_of(c * TQ, TQ)
        scored = []
        for g in groups:
            bias = (_dot(ext_ref[c], sel_ts[g]) - 1.0) * (-NEG) + extra
            scored.append(blk_scores(kk_ref[0, pl.ds(k0, TQ), g * 64:(g + 1) * 64], qhs[g], bias))
        new = []
        for g in groups:
            s, smax = scored[g]
            m_i, acc = carry[g]
            m_new = jnp.maximum(m_i, smax)
            pv = blk_pv(lambda i: with_ones(vt_ref[0, c, g * 64:(g + 1) * 64, i * KB:(i + 1) * KB]), s, m_new)
            new.append((m_new, jnp.exp(m_i - m_new) * acc + pv))
        return tuple(new)

    init = tuple((jnp.full((1, GROUP_A * TQ), NEG, F32), jnp.zeros((80, GROUP_A * TQ), F32)) for _ in groups)
    final = lax.fori_loop(0, qt + 1, body, init)

    outs = [[], [], []]
    for g in groups:
        acc = final[g][1]
        o_slc = acc[0:64, :] / acc[64:65, :]
        for c, o in enumerate((o_cmps[g], o_slc, o_wins[g])):
            outs[c] += [o[:, j * TQ:(j + 1) * TQ] for j in range(GROUP_A)]

    o_t = jnp.zeros((512, TQ), F32)
    for c in range(3):
        o_t = o_t + gates_t[c * 512:(c + 1) * 512, :] * jnp.concatenate(outs[c], axis=0)
    o_ref[0] = o_t.T.astype(BF16)


def _post_prompt_kernel(x_ref, on_ref, gm0_ref, part_ref, wonsa_ref, wout_ref, nf_ref, wup_ref, fcw_ref,
                        fcb_ref, wdn_ref, y_ref, fst_ref, ubuf):
    @pl.when(pl.program_id(1) == 0)
    def _():
        ubuf[0:8, :] = jnp.zeros((8, D_FF), F32)

    y_a = _dot(on_ref[0], wonsa_ref[...])
    merged = gm0_ref[0].astype(F32) * y_a + part_ref[0].astype(F32)
    x1 = x_ref[0] + _dot(merged.astype(BF16), wout_ref[...])

    def prev2(u):
        ubuf[8:8 + TM, :] = u
        return ubuf[pl.ds(6, TM), :]

    def prev1(u):
        return ubuf[pl.ds(7, TM), :]

    y, u = _ffn_tail(x1, nf_ref[...], wup_ref, prev2, prev1, fcw_ref[...], fcb_ref[...], wdn_ref)
    y_ref[0] = y
    fst_ref[0] = u[TM - 2:TM, :]
    ubuf[0:8, :] = ubuf[TM:TM + 8, :]


def _pre_sample_kernel(x_ref, na_ref, wa_ref, wg_ref, w_ref, qn_ref, kn1_ref, kn2_ref, gmat_ref, tab_ref,
                       cw_ref, cb_ref, lng_ref, lnb_ref, woc_ref, cc_ref, mqn_ref, eg_ref,
                       q_ref, rows_ref, win_ref, gae_ref, qm_ref, gm_ref, yb_ref, cst_ref):
    gmat = gmat_ref[...]
    tab = (tab_ref[0], tab_ref[1], tab_ref[2])
    h = _rms(x_ref[...], na_ref[...]).astype(BF16)
    zq, zkv, zga, zglu = _project_qkv(h, wa_ref, wg_ref, w_ref)
    q, rows, win, ga = _in_proj_common(zq, zkv, zga, qn_ref[...], kn1_ref[...], kn2_ref[...], gmat, tab)
    zqm, zgm = _project_rest(h, w_ref)
    q_ref[...] = q
    rows_ref[...] = rows
    win_ref[...] = win
    gae_ref[...] = _dot_hilo(ga, eg_ref[...])
    glu = _glu(zglu)
    cw = cw_ref[...]
    c = cw[CONV_W - 1:CONV_W] * glu + cb_ref[...]
    for k in range(CONV_W - 1):
        c = c + cw[k:k + 1] * cc_ref[k]
    for k in range(CONV_W - 2):
        cst_ref[k] = cc_ref[k + 1]
    cst_ref[CONV_W - 2] = glu
    yb_ref[...] = _conv_tail(c, lng_ref[...], lnb_ref[...], woc_ref)
    qm_ref[...] = _mem_q(zqm, mqn_ref[...], gmat)
    gm_ref[...] = jax.nn.sigmoid(zgm)


def _heads_to_lanes(o8):
    lane = lax.broadcasted_iota(jnp.int32, (1, LANES), 1)
    blocks = []
    for cb in range(4):
        a = o8[2 * cb:2 * cb + 1, :]
        b = o8[2 * cb + 1:2 * cb + 2, :]
        if cb // 2 == 1:
            a = pltpu.roll(a, 64, 1)
        else:
            b = pltpu.roll(b, 64, 1)
        blocks.append(jnp.where(lane < 64, a, b))
    return jnp.concatenate(blocks, axis=1)


def _attn_sample_kernel(pt_ref, *refs):
    del pt_ref
    page_refs = refs[:TB * 16]
    (q_ref, rn_ref, wn_ref, gae_ref, qm_ref, cw_ref, cm_ref, wcat_ref, pet_ref, kn0_ref, gmat_ref, ctab_ref,
     ov_ref, ex_ref, pm_ref, on_ref, om_ref, wo_ref) = refs[TB * 16:]
    lane = lax.broadcasted_iota(jnp.int32, (1, LANES), 1)
    row8 = lax.broadcasted_iota(jnp.int32, (8, LANES), 0)
    lane8 = lax.broadcasted_iota(jnp.int32, (8, LANES), 1)

    pm = pm_ref[...]
    taps = [[_dot_t(pm, page_refs[i][0, c * LANES:(c + 1) * LANES, :].astype(BF16)) for c in range(2)]
            for i in range(TB * 16)]

    def tap(l):
        return jnp.concatenate(
            [jnp.concatenate([taps[i][c][l * 8:(l + 1) * 8, :] for i in range(TB * 16)], axis=0) for c in range(2)],
            axis=1)

    kc_all, vc_all = _compress(tap, TB * 128, wcat_ref, pet_ref[0:1, :], kn0_ref[...], gmat_ref[...],
                               tuple(jnp.concatenate([ctab_ref[i]] * TB, axis=0) for i in range(3)))

    toks = range(TB)
    rowm = lax.broadcasted_iota(jnp.int32, (8, 256), 0)
    headm = jnp.right_shift(lax.broadcasted_iota(jnp.int32, (8, 256), 1), 6)

    def bf(x):
        return x.astype(BF16).astype(F32)

    q8s, s_slc, s_win, s_mem = [], [], [], []
    for tok in toks:
        q = q_ref[tok]
        q_rows = []
        for r in range(N_HEADS_A):
            piece = q[:, (r // 2) * LANES:(r // 2 + 1) * LANES]
            if (r % 2) != (r // 4):
                piece = pltpu.roll(piece, 64, 1)
            q_rows.append(jnp.where(jnp.right_shift(lane, 6) == (r // 4), piece, 0.0))
        q8 = jnp.concatenate(q_rows, axis=0).astype(BF16)
        q8s.append(q8)
        s_slc.append(jnp.concatenate(
            [_dot(q8, page_refs[tok * 16 + i][0, 256:384, :].astype(BF16)) for i in range(16)], axis=1))
        s_win.append(_dot(q8, cw_ref[tok, 0:128, :].astype(BF16)))
        qm8 = jnp.where(rowm == headm, jnp.broadcast_to(qm_ref[tok], (8, 256)), 0.0).astype(BF16)
        s_mem.append(_dot(qm8, cm_ref[tok, 0:256, :].astype(BF16)))

    o_cmp, imp8 = [], []
    for tok in toks:
        s = jnp.where(lane8 < 127, _dot_t(q8s[tok], kc_all[tok * 128:(tok + 1) * 128]), NEG)
        p, den = _softmax_rows(s)
        p = p / den
        o_cmp.append(_dot(p.astype(BF16), vc_all[tok * 128:(tok + 1) * 128].astype(BF16)))
        imp8.append(_dot_hilo(p, ov_ref[...]))

    o_win = []
    for tok in toks:
        wn = wn_ref[tok]
        s = s_win[tok]
        s_new = jnp.sum(q8s[tok].astype(F32) * bf(wn[:, 0:128]), axis=-1, keepdims=True)
        m = jnp.maximum(jnp.max(s, axis=-1, keepdims=True), s_new)
        p = jnp.exp(s - m)
        p_new = jnp.exp(s_new - m)
        den = jnp.sum(p, axis=-1, keepdims=True) + p_new
        o_win.append((_dot_t(p.astype(BF16), cw_ref[tok, 128:256, :].astype(BF16))
                      + bf(p_new) * bf(wn[:, 128:256])) / den)
        p, den = _softmax_rows(s_mem[tok])
        o8 = _dot_t(p.astype(BF16), cm_ref[tok, 256:512, :].astype(BF16)) / den
        om_ref[tok] = jnp.sum(jnp.where(rowm == headm, o8, 0.0), axis=0, keepdims=True)
        cw = cw_ref[tok]
        rolled = pltpu.roll(cw, WINDOW - 1, 1)
        last = lax.broadcasted_iota(jnp.int32, (1, WINDOW), 1) == WINDOW - 1
        wn_col = jnp.broadcast_to(wn, (LANES, 256)).T[:, 0:1]
        wo_ref[tok] = jnp.where(last, wn_col, rolled)

    selx = []
    for tok in toks:
        sel_rows = []
        for g in range(N_KV_A):
            imp = jnp.sum(imp8[tok][4 * g:4 * g + 4, :], axis=0, keepdims=True)
            forced = (lane == 0) | (lane == 31) | (lane == 32)
            score = jnp.where(lane < 33, jnp.where(forced, BIG, imp), -3.0 * BIG)
            a = jnp.broadcast_to(score, (LANES, LANES))
            b = a.T
            sub = lax.broadcasted_iota(jnp.int32, (LANES, LANES), 0)
            ln = lax.broadcasted_iota(jnp.int32, (LANES, LANES), 1)
            beats = (b > a) | ((b == a) & (sub < ln))
            rank = jnp.sum(jnp.where(beats, 1.0, 0.0), axis=0, keepdims=True)
            sel_rows.append(jnp.where((rank < N_SEL) & (lane < 33), 1.0, 0.0))
        sel8 = jnp.where(row8 < 4, sel_rows[0], sel_rows[1]).astype(BF16)
        selx.append(_dot(sel8, ex_ref[...]))

    for tok in toks:
        rn = rn_ref[tok]
        gae = gae_ref[tok]
        s = jnp.where(selx[tok] > 0.5, s_slc[tok], NEG)
        s_new = jnp.sum(q8s[tok].astype(F32) * bf(rn[:, 256:384]), axis=-1, keepdims=True)
        m = jnp.maximum(jnp.max(s, axis=-1, keepdims=True), s_new)
        p = jnp.exp(s - m)
        p_new = jnp.exp(s_new - m)
        den = jnp.sum(p, axis=-1, keepdims=True) + p_new
        o_slc = bf(p_new) * bf(rn[:, 384:512])
        for i in range(16):
            o_slc = o_slc + _dot_t(p[:, i * PAGE:(i + 1) * PAGE].astype(BF16),
                                   page_refs[tok * 16 + i][0, 384:512, :].astype(BF16))
        o_slc = o_slc / den
        on_ref[tok] = (gae[:, 0:512] * _heads_to_lanes(o_cmp[tok]) + gae[:, 512:1024] * _heads_to_lanes(o_slc)
                       + gae[:, 1024:1536] * _heads_to_lanes(o_win[tok]))


def _post_sample_kernel(x_ref, on_ref, om_ref, gm_ref, yb_ref, wonsa_ref, wom_ref, wout_ref, nf_ref, wup_ref,
                        fcw_ref, fcb_ref, wdn_ref, cf_ref, y_ref, fst_ref):
    y_a = _dot(on_ref[...].astype(BF16), wonsa_ref[...])
    y_m = _dot(om_ref[...].astype(BF16), wom_ref[...])
    gm = gm_ref[...]
    merged = gm[:, 0:1024] * y_a + gm[:, 1024:2048] * yb_ref[...] + gm[:, 2048:3072] * y_m
    x1 = x_ref[...] + _dot(merged.astype(BF16), wout_ref[...])
    y, u = _ffn_tail(x1, nf_ref[...], wup_ref, lambda u: cf_ref[:, 0, :], lambda u: cf_ref[:, 1, :],
                     fcw_ref[...], fcb_ref[...], wdn_ref)
    y_ref[...] = y
    fst_ref[:, 0, :] = cf_ref[:, 1, :]
    fst_ref[:, 1, :] = u


def _rope_tables(pos):
    inv = ROPE_THETA ** (-jnp.arange(0, ROPE_DIM, 2, dtype=F32) / ROPE_DIM)
    ang = pos.astype(F32)[:, None] * inv
    cos, sin = jnp.cos(ang), jnp.sin(ang)
    n = pos.shape[0]
    one = jnp.ones((n, HEAD_DIM - ROPE_DIM), F32)
    z8 = jnp.zeros((n, 8), F32)
    z48 = jnp.zeros((n, HEAD_DIM - ROPE_DIM), F32)
    c = jnp.concatenate([cos, cos, one], axis=1)
    s1 = jnp.concatenate([z8, sin, z48], axis=1)
    s2 = jnp.concatenate([-sin, z8, z48], axis=1)
    return jnp.stack([jnp.tile(c, (1, 2)), jnp.tile(s1, (1, 2)), jnp.tile(s2, (1, 2))])


def _tile2(v):
    return jnp.tile(v.reshape(1, HEAD_DIM), (1, 2))


def _const(shape):
    nd = len(shape)
    return pl.BlockSpec(shape, lambda *_: (0,) * nd, pipeline_mode=pl.Buffered(1))


def _params(*sem):
    return pltpu.CompilerParams(dimension_semantics=sem, vmem_limit_bytes=VMEM_LIMIT)


def kernel(x_prompt, x_sample, cache_nsa, cache_win, cache_conv, cache_ffn, cache_mem, page_table, mem_prompt,
           norm_attn, w_in, q_norm, k_norm, cmp_pe, w_cmp, w_o_nsa, conv_w, conv_b, conv_ln_g, conv_ln_b, w_o_conv,
           norm_mem, w_mem_kv, mq_norm, mk_norm, w_o_mem, w_out, norm_ffn, w_ffn_up, ffn_conv_w, ffn_conv_b,
           w_ffn_down):
    B, T, _ = x_prompt.shape
    NS = x_sample.shape[0]
    n_pages = page_table.shape[1]
    assert w_in.shape[0] == 1 and T == 2048 and n_pages * PAGE == 2048 and cache_win.shape[2] == WINDOW
    nt = T // TM
    assert TQ == TM

    w_in0 = w_in[0]
    w_qkv = w_in0[:, :N_QKV].astype(BF16)
    w_gate = jnp.pad(w_in0[:, N_QKV:N_QKV + N_GATES], ((0, 0), (0, LANES - N_GATES))).astype(BF16)
    w_rest = w_in0[:, N_QKV + N_GATES:].astype(BF16)
    na = norm_attn.reshape(1, D_MODEL)
    nf = norm_ffn.reshape(1, D_MODEL)
    nm = norm_mem.reshape(1, D_MODEL)
    qn, mqn, mkn = _tile2(q_norm[0]), _tile2(mq_norm[0]), _tile2(mk_norm[0])
    kn0, kn1, kn2 = _tile2(k_norm[0, 0]), _tile2(k_norm[0, 1]), _tile2(k_norm[0, 2])
    ii = jnp.arange(LANES)
    gmat = jnp.where((ii[:, None] // 64) == (ii[None, :] // 64), 1.0 / 64, 0.0).astype(BF16)
    tab_p = _rope_tables(jnp.arange(T))
    tab_s = _rope_tables(jnp.full((1,), n_pages * PAGE))
    ctab = _rope_tables(jnp.arange(128) * CMP_STRIDE + (CMP_BLK - 1))
    wk, wv = w_cmp[0, 0].astype(BF16), w_cmp[0, 1].astype(BF16)
    z = jnp.zeros((CMP_BLK, HEAD_DIM, HEAD_DIM), BF16)
    w_l = jnp.concatenate([jnp.concatenate(r, axis=-1) for r in
                           ([wk, z, z, z], [z, wk, z, z], [z, z, wv, z], [z, z, z, wv])], axis=1)
    wcat = jnp.concatenate([w_l[:16], w_l[16:]], axis=-1)
    pe2 = jnp.broadcast_to(cmp_pe[0].transpose(1, 0, 2)[:, :, None, :], (CMP_BLK, 2, 2, HEAD_DIM)).reshape(
        CMP_BLK, 256)
    col = jnp.arange(3 * 512)
    egate = (jnp.arange(LANES)[:, None] == ((col // 512) * 8 + (col % 512) // 64)[None, :]).astype(BF16)
    egate_t = egate.T
    cs = jnp.arange(LANES)[:, None] * CMP_STRIDE
    ss = jnp.arange(LANES)[None, :] * SLC_BLK
    overlap = ((cs < ss + SLC_BLK) & (cs + CMP_BLK > ss) & (jnp.arange(LANES)[:, None] < 127)
               & (jnp.arange(LANES)[None, :] < 33)).astype(BF16)
    kpos = jnp.arange(T)
    expand = ((jnp.arange(LANES)[:, None] == (kpos // SLC_BLK)[None, :])).astype(BF16)
    expand_ct = expand.T.reshape(T // TQ, TQ, LANES)
    woc = w_o_conv[0].astype(BF16)
    wom = w_o_mem[0].astype(BF16)
    wonsa = w_o_nsa[0].astype(BF16)
    wout = w_out[0].astype(BF16)
    wup = w_ffn_up[0].astype(BF16)
    wdn = w_ffn_down[0].astype(BF16)
    wmkv = w_mem_kv[0].astype(BF16)
    cw, cb = conv_w[0], conv_b.reshape(1, C_CONV)
    lng, lnb = conv_ln_g.reshape(1, C_CONV), conv_ln_b.reshape(1, C_CONV)
    fcw, fcb = ffn_conv_w[0], ffn_conv_b.reshape(1, D_FF)

    mem_kv = pl.pallas_call(
        _memkv_kernel, grid=(B,),
        in_specs=[pl.BlockSpec((1, N_MEM, D_MODEL), lambda b: (b, 0, 0)), _const((1, D_MODEL)),
                  _const((D_MODEL, 512)), _const((1, LANES)), _const((LANES, LANES))],
        out_specs=pl.BlockSpec((1, N_MEM, 512), lambda b: (b, 0, 0)),
        out_shape=jax.ShapeDtypeStruct((B, N_MEM, 512), F32),
        compiler_params=_params("arbitrary"), name="mem_kv",
    )(mem_prompt, nm, wmkv, mkn, gmat)

    def tile(width):
        return pl.BlockSpec((1, TM, width), lambda b, t: (b, t, 0))

    n_win_t = WINDOW // TM
    pre_out_shapes = (
        jax.ShapeDtypeStruct((B, T, 512), BF16),
        jax.ShapeDtypeStruct((B, T, 512), F32),
        jax.ShapeDtypeStruct((B, T, 256), BF16),
        jax.ShapeDtypeStruct((B, nt, 256, TM), BF16),
        jax.ShapeDtypeStruct((B, WINDOW, 256), F32),
        jax.ShapeDtypeStruct((B, T, LANES), F32),
        jax.ShapeDtypeStruct((B, T, D_MODEL), BF16),
        jax.ShapeDtypeStruct((B, T, D_MODEL), BF16),
        jax.ShapeDtypeStruct((B, CONV_W - 1, C_CONV), F32),
    )
    q_p, rows_p, kk_p, vt_p, win_p, ga_p, gm0_p, part_p, cst_p = pl.pallas_call(
        _pre_prompt_kernel, grid=(B, nt),
        in_specs=[tile(D_MODEL), _const((1, D_MODEL)), _const((D_MODEL, N_QKV)), _const((D_MODEL, LANES)),
                  _const((D_MODEL, N_REST)), _const((1, LANES)),
                  _const((1, LANES)), _const((1, LANES)), _const((LANES, LANES)),
                  pl.BlockSpec((3, TM, LANES), lambda b, t: (0, t, 0)),
                  _const((CONV_W, C_CONV)), _const((1, C_CONV)), _const((1, C_CONV)), _const((1, C_CONV)),
                  _const((C_CONV, D_MODEL)),
                  pl.BlockSpec((1, N_MEM, 512), lambda b, t: (b, 0, 0)), _const((1, LANES)),
                  _const((256, D_MODEL))],
        out_specs=(tile(512), tile(512), tile(256),
                   pl.BlockSpec((1, 1, 256, TM), lambda b, t: (b, t, 0, 0)),
                   pl.BlockSpec((1, TM, 256), lambda b, t: (b, jnp.maximum(t - (nt - n_win_t), 0), 0)),
                   tile(LANES), tile(D_MODEL), tile(D_MODEL),
                   pl.BlockSpec((1, CONV_W - 1, C_CONV), lambda b, t: (b, 0, 0))),
        out_shape=pre_out_shapes,
        scratch_shapes=[pltpu.VMEM((TM + 32, C_CONV), F32), pltpu.VMEM((7, TM + 24, C_CONV), F32)],
        compiler_params=_params("arbitrary", "arbitrary"), name="pre_prompt",
    )(x_prompt, na, w_qkv, w_gate, w_rest, qn, kn1, kn2, gmat, tab_p, cw, cb, lng, lnb, woc, mem_kv, mqn, wom)

    pe_term = pl.pallas_call(
        _pe_term_kernel, out_shape=jax.ShapeDtypeStruct((8, 256), F32), name="pe_term",
    )(pe2, wcat)
    kc_p, vc_p = pl.pallas_call(
        _compress_prompt_kernel, grid=(B,),
        in_specs=[pl.BlockSpec((1, T, LANES), lambda b: (b, 0, 0)), pl.BlockSpec((1, T, LANES), lambda b: (b, 0, 1)),
                  _const((16, 256, 512)), _const((8, 256)),
                  _const((1, LANES)), _const((LANES, LANES)), _const((3, LANES, LANES))],
        out_specs=(pl.BlockSpec((1, LANES, LANES), lambda b: (b, 0, 0)),) * 2,
        out_shape=(jax.ShapeDtypeStruct((B, LANES, LANES), BF16),) * 2,
        compiler_params=_params("arbitrary"), name="compress_prompt",
    )(rows_p, rows_p, wcat, pe_term, kn0, gmat, ctab)

    xs = x_sample.reshape(NS, D_MODEL)
    pre_s_shapes = (
        jax.ShapeDtypeStruct((NS, 512), F32), jax.ShapeDtypeStruct((NS, 512), F32),
        jax.ShapeDtypeStruct((NS, 256), F32), jax.ShapeDtypeStruct((NS, 3 * 512), F32),
        jax.ShapeDtypeStruct((NS, 256), F32), jax.ShapeDtypeStruct((NS, 3 * D_MODEL), F32),
        jax.ShapeDtypeStruct((NS, D_MODEL), F32), jax.ShapeDtypeStruct((CONV_W - 1, NS, C_CONV), F32),
    )
    q_s, rows_s, win_s, gae_s, qm_s, gm_s, yb_s, cst_s = pl.pallas_call(
        _pre_sample_kernel, out_shape=pre_s_shapes,
        compiler_params=pltpu.CompilerParams(vmem_limit_bytes=VMEM_LIMIT), name="pre_sample",
    )(xs, na, w_qkv, w_gate, w_rest, qn, kn1, kn2, gmat, tab_s, cw, cb, lng, lnb, woc,
      cache_conv[0].transpose(1, 0, 2), mqn, egate)
    cst_s = cst_s.transpose(1, 0, 2)

    pages = cache_nsa[0].transpose(0, 2, 3, 4, 1).reshape(cache_nsa.shape[1], 512, PAGE)
    cw_t = cache_win[0].transpose(0, 2, 3, 4, 1).reshape(NS, 256, WINDOW)
    cm_t = cache_mem[0].transpose(0, 2, 3, 4, 1).reshape(NS, 512, N_MEM)
    pt_flat = page_table.reshape(-1)

    def page_spec(tok, p):
        return pl.BlockSpec((1, 512, PAGE), lambda i, pt: (pt[(i * TB + tok) * n_pages + p], 0, 0))

    rr = jnp.arange(PAGE)
    perm = (rr[None, :] == ((rr % 8) * CMP_STRIDE + rr // 8)[:, None]).astype(BF16)

    def tok_spec(width):
        return pl.BlockSpec((TB, 1, width), lambda i, pt: (i, 0, 0))

    def cst_spec(shape):
        nd = len(shape)
        return pl.BlockSpec(shape, lambda i, pt: (0,) * nd, pipeline_mode=pl.Buffered(1))

    grid_spec = pltpu.PrefetchScalarGridSpec(
        num_scalar_prefetch=1, grid=(NS // TB,),
        in_specs=[page_spec(tok, p) for tok in range(TB) for p in range(n_pages)] + [
            tok_spec(512), tok_spec(512), tok_spec(256), tok_spec(3 * 512), tok_spec(256),
            pl.BlockSpec((TB, 256, WINDOW), lambda i, pt: (i, 0, 0)),
            pl.BlockSpec((TB, 512, N_MEM), lambda i, pt: (i, 0, 0)),
            cst_spec((16, 256, 512)), cst_spec((8, 256)), cst_spec((1, LANES)), cst_spec((LANES, LANES)),
            cst_spec((3, LANES, LANES)), cst_spec((LANES, LANES)), cst_spec((LANES, T)),
            cst_spec((PAGE, PAGE))],
        out_specs=(tok_spec(512), tok_spec(256), pl.BlockSpec((TB, 256, WINDOW), lambda i, pt: (i, 0, 0))),
    )
    on_s, om_s, wo_s = pl.pallas_call(
        _attn_sample_kernel, grid_spec=grid_spec,
        out_shape=(jax.ShapeDtypeStruct((NS, 1, 512), F32), jax.ShapeDtypeStruct((NS, 1, 256), F32),
                   jax.ShapeDtypeStruct((NS, 256, WINDOW), F32)),
        compiler_params=_params("arbitrary"), name="attn_sample",
    )(pt_flat, *([pages] * (TB * n_pages)), q_s.reshape(NS, 1, 512), rows_s.reshape(NS, 1, 512),
      win_s.reshape(NS, 1, 256), gae_s.reshape(NS, 1, 3 * 512), qm_s.reshape(NS, 1, 256),
      cw_t, cm_t, wcat, pe_term, kn0, gmat, ctab, overlap, expand, perm)
    wo_s = wo_s.reshape(NS, 2, N_KV_A, HEAD_DIM, WINDOW).transpose(0, 4, 1, 2, 3)

    o_nsa_p = pl.pallas_call(
        _attn_prompt_kernel, grid=(B, T // TQ),
        in_specs=[pl.BlockSpec((1, TQ, 512), lambda b, t: (b, t, 0)),
                  pl.BlockSpec((1, T, 256), lambda b, t: (b, 0, 0)),
                  pl.BlockSpec((1, nt, 256, TM), lambda b, t: (b, 0, 0, 0)),
                  pl.BlockSpec((1, LANES, LANES), lambda b, t: (b, 0, 0)),
                  pl.BlockSpec((1, LANES, LANES), lambda b, t: (b, 0, 0)),
                  pl.BlockSpec((1, TQ, LANES), lambda b, t: (b, t, 0)),
                  _const((3 * 512, LANES)), _const((LANES, LANES)), _const((T // TQ, TQ, LANES))],
        out_specs=pl.BlockSpec((1, TQ, 512), lambda b, t: (b, t, 0)),
        out_shape=jax.ShapeDtypeStruct((B, T, 512), BF16),
        compiler_params=_params("arbitrary", "arbitrary"), name="attn_prompt",
    )(q_p, kk_p, vt_p, kc_p, vc_p, ga_p, egate_t, overlap.T, expand_ct)

    y_p, fst_p = pl.pallas_call(
        _post_prompt_kernel, grid=(B, nt),
        in_specs=[tile(D_MODEL), tile(512), tile(D_MODEL), tile(D_MODEL), _const((512, D_MODEL)),
                  _const((D_MODEL, D_MODEL)), _const((1, D_MODEL)), _const((D_MODEL, 2 * D_FF)),
                  _const((FFN_CONV_W, D_FF)), _const((1, D_FF)), _const((D_FF, D_MODEL))],
        out_specs=(tile(D_MODEL), pl.BlockSpec((1, FFN_CONV_W - 1, D_FF), lambda b, t: (b, 0, 0))),
        out_shape=(jax.ShapeDtypeStruct((B, T, D_MODEL), F32),
                   jax.ShapeDtypeStruct((B, FFN_CONV_W - 1, D_FF), F32)),
        scratch_shapes=[pltpu.VMEM((TM + 8, D_FF), F32)],
        compiler_params=_params("arbitrary", "arbitrary"), name="post_prompt",
    )(x_prompt, o_nsa_p, gm0_p, part_p, wonsa, wout, nf, wup, fcw, fcb, wdn)

    y_s, fst_s = pl.pallas_call(
        _post_sample_kernel,
        out_shape=(jax.ShapeDtypeStruct((NS, D_MODEL), F32),
                   jax.ShapeDtypeStruct((NS, FFN_CONV_W - 1, D_FF), F32)),
        compiler_params=pltpu.CompilerParams(vmem_limit_bytes=VMEM_LIMIT), name="post_sample",
    )(xs, on_s.reshape(NS, 512), om_s.reshape(NS, 256), gm_s, yb_s, wonsa, wom, wout, nf, wup, fcw, fcb, wdn,
      cache_ffn[0])

    return (y_p, y_s.reshape(NS, 1, D_MODEL),
            rows_p.reshape(1, B, T, 4, N_KV_A, HEAD_DIM), rows_s.reshape(1, NS, 1, 4, N_KV_A, HEAD_DIM),
            win_p.reshape(1, B, WINDOW, 2, N_KV_A, HEAD_DIM), wo_s[None],
            cst_p[None], cst_s[None], fst_p[None], fst_s[None],
            mem_kv.reshape(1, B, N_MEM, 2, N_HEADS_M, HEAD_DIM))
```

```python
import jax
import jax.numpy as jnp
from jax import lax
from jax.experimental import pallas as pl
from jax.experimental.pallas import tpu as pltpu

F32 = jnp.float32
BF16 = jnp.bfloat16

D_MODEL = 1024
HEAD_DIM = 64
N_HEADS_A = 8
N_KV_A = 2
GROUP_A = 4
CMP_BLK = 32
CMP_STRIDE = 16
SLC_BLK = 64
N_SEL = 16
WINDOW = 512
C_CONV = 512
CONV_W = 31
N_MEM = 256
N_HEADS_M = 4
D_FF = 2816
FFN_CONV_W = 3
ROPE_THETA = 500000.0
ROPE_DIM = 16
EPS = 1e-6
BIG = 1e9
NEG = -1e30
PAGE = 128

N_QKV = 1280
N_GATES = 24
C_GLU = 0
C_QM = 1024
C_GM = 1280
N_REST = 4352

LANES = 128
VMEM_LIMIT = 56 * 1024 * 1024

TM = 256
TQ = 256
TB = 4
KB = 128
T_SLC = 32


def _dot(a, b):
    return jnp.dot(a, b, preferred_element_type=F32)


def _dot_t(a, b):
    return lax.dot_general(a, b, (((1,), (1,)), ((), ())), preferred_element_type=F32)


def _dot_hilo(x, m):
    hi = x.astype(BF16)
    lo = (x - hi.astype(F32)).astype(BF16)
    return _dot(hi, m) + _dot(lo, m)


def _rms(x, g):
    return x * lax.rsqrt(jnp.mean(x * x, axis=-1, keepdims=True) + EPS) * g


def _head_norm(blk, gain, gmat):
    ms = _dot_hilo(blk * blk, gmat)
    return blk * lax.rsqrt(ms + EPS) * gain


def _rope(blk, tab):
    c, s1, s2 = tab
    return blk * c + pltpu.roll(blk, 8, 1) * s1 + pltpu.roll(blk, LANES - 8, 1) * s2


def _softmax_rows(s):
    m = jnp.max(s, axis=-1, keepdims=True)
    p = jnp.exp(s - m)
    return p, jnp.sum(p, axis=-1, keepdims=True)


def _project_qkv(h, wa_ref, wg_ref, w_ref):
    return (_dot(h, wa_ref[:, 0:512]), _dot(h, wa_ref[:, 512:N_QKV]), _dot(h, wg_ref[...]),
            _dot(h, w_ref[:, C_GLU:C_GLU + 1024]))


def _project_rest(h, w_ref):
    return _dot(h, w_ref[:, C_QM:C_QM + 256]), _dot(h, w_ref[:, C_GM:C_GM + 3072])


def _in_proj_common(zq, zkv, zga, qn, kn1, kn2, gmat, tab):
    q_blocks = []
    for cb in range(4):
        blk = zq[:, cb * LANES:(cb + 1) * LANES]
        q_blocks.append(_rope(_head_norm(blk, qn, gmat), tab) * (HEAD_DIM ** -0.5))
    q = jnp.concatenate(q_blocks, axis=1)
    k_slc = _rope(_head_norm(zkv[:, 256:384], kn1, gmat), tab)
    k_win = _rope(_head_norm(zkv[:, 512:640], kn2, gmat), tab)
    rows = jnp.concatenate([zkv[:, 0:256], k_slc, zkv[:, 384:512]], axis=1)
    win = jnp.concatenate([k_win, zkv[:, 640:768]], axis=1)
    return q, rows, win, jax.nn.sigmoid(zga)


def _glu(z):
    return z[:, :C_CONV] * jax.nn.sigmoid(z[:, C_CONV:])


def _conv_tail(c, lng, lnb, wo_ref):
    mu = jnp.mean(c, axis=-1, keepdims=True)
    var = jnp.mean(jnp.square(c - mu), axis=-1, keepdims=True)
    y = (c - mu) * lax.rsqrt(var + EPS) * lng + lnb
    return _dot(jax.nn.silu(y).astype(BF16), wo_ref[...])


def _mem_q(z, mqn, gmat):
    return jnp.concatenate(
        [_head_norm(z[:, cb * LANES:(cb + 1) * LANES], mqn, gmat) for cb in range(2)], axis=1) * (HEAD_DIM ** -0.5)


def _ffn_tail(x1, nf, wup_ref, u_prev2, u_prev1_fn, fcw, fcb, wdn_ref):
    h2 = _rms(x1, nf).astype(BF16)
    up = _dot(h2, wup_ref[...])
    u = up[:, :D_FF]
    v = up[:, D_FF:]
    uc = fcw[0:1] * u_prev2(u) + fcw[1:2] * u_prev1_fn(u) + fcw[2:3] * u + fcb
    act = jax.nn.gelu(uc, approximate=True) * v
    return x1 + _dot(act.astype(BF16), wdn_ref[...]), u


def _memkv_kernel(mem_ref, nm_ref, w_ref, mkn_ref, gmat_ref, o_ref):
    h = _rms(mem_ref[0], nm_ref[...]).astype(BF16)
    z = _dot(h, w_ref[...])
    gmat = gmat_ref[...]
    k = [_head_norm(z[:, cb * LANES:(cb + 1) * LANES], mkn_ref[...], gmat) for cb in range(2)]
    o_ref[0] = jnp.concatenate(k + [z[:, 256:512]], axis=1)


def _pre_prompt_kernel(x_ref, na_ref, wa_ref, wg_ref, w_ref, qn_ref, kn1_ref, kn2_ref, gmat_ref, tab_ref, oh_ref,
                       cw_ref, cb_ref, lng_ref, lnb_ref, woc_ref, mkv_ref, mqn_ref, wom_ref,
                       q_ref, rows_ref, kk_ref, vt_ref, win_ref, ga_ref, gm0_ref, part_ref, cst_ref, hbuf, sbuf):
    @pl.when(pl.program_id(1) == 0)
    def _():
        hbuf[0:32, :] = jnp.zeros((32, C_CONV), F32)

    gmat = gmat_ref[...]
    tab = (tab_ref[0], tab_ref[1], tab_ref[2])
    h = _rms(x_ref[0], na_ref[...]).astype(BF16)
    zq, zkv, zga, zglu = _project_qkv(h, wa_ref, wg_ref, w_ref)
    zqm = _dot(h, w_ref[:, C_QM:C_QM + 256])

    glu = _glu(zglu)
    hbuf[32:32 + TM, :] = glu
    cw = cw_ref[...]
    c = jnp.zeros((TM, C_CONV), F32) + cb_ref[...]
    zgm_chunks = []
    for r in range(8):
        if r < 6:
            zgm_chunks.append(_dot(h, w_ref[:, C_GM + r * 512:C_GM + (r + 1) * 512]))
        if r > 0:
            sbuf[r - 1] = hbuf[pl.ds(r, TM + 24), :]
        for a in range(5):
            k = 8 * a + r - 2
            if 0 <= k < CONV_W:
                src = hbuf[8 * a:8 * a + TM, :] if r == 0 else sbuf[r - 1, 8 * a:8 * a + TM, :]
                c = c + cw[k:k + 1] * src
    zgm = jnp.concatenate(zgm_chunks, axis=1)
    cst_ref[0] = hbuf[pl.ds(TM + 2, CONV_W - 1), :]
    hbuf[0:32, :] = hbuf[TM:TM + 32, :]

    q, rows, win, ga = _in_proj_common(zq, zkv, zga, qn_ref[...], kn1_ref[...], kn2_ref[...], gmat, tab)
    q_ref[0] = q.astype(BF16)
    rows_ref[0] = rows
    win_ref[0] = win
    ga_ref[0] = ga
    oh = oh_ref[...]
    kk_ref[0] = jnp.concatenate([rows[:, 256:320].astype(BF16), oh, rows[:, 320:384].astype(BF16), oh,
                                 win[:, 0:128].astype(BF16)], axis=1)
    vt_ref[0, 0] = jnp.concatenate([rows[:, 384:512], win[:, 128:256]], axis=1).T.astype(BF16)

    y_b = _conv_tail(c, lng_ref[...], lnb_ref[...], woc_ref)

    qm = _mem_q(zqm, mqn_ref[...], gmat).astype(BF16)
    mkv = mkv_ref[0].astype(BF16)
    scores = [_dot_t(qm[:, hh * 64:(hh + 1) * 64], mkv[:, hh * 64:(hh + 1) * 64]) for hh in range(N_HEADS_M)]
    probs = [_softmax_rows(s) for s in scores]
    heads = [_dot(p.astype(BF16), mkv[:, 256 + hh * 64:256 + (hh + 1) * 64]) / den
             for hh, (p, den) in enumerate(probs)]
    y_m = _dot(jnp.concatenate(heads, axis=1).astype(BF16), wom_ref[...])

    gm = jax.nn.sigmoid(zgm)
    gm0_ref[0] = gm[:, 0:1024].astype(BF16)
    part_ref[0] = (gm[:, 1024:2048] * y_b + gm[:, 2048:3072] * y_m).astype(BF16)


def _pe_term_kernel(pe_ref, wcat_ref, o_ref):
    acc = jnp.zeros((8, 256), F32)
    for l in range(16):
        top = jnp.broadcast_to(pe_ref[l:l + 1, :], (8, 256)).astype(BF16)
        bot = jnp.broadcast_to(pe_ref[l + 16:l + 17, :], (8, 256)).astype(BF16)
        acc = acc + _dot(top, wcat_ref[l][:, 0:256]) + _dot(bot, wcat_ref[l][:, 256:512])
    o_ref[...] = acc


def _compress(tap_fn, n_rows, wcat_ref, pe_term, kn0, gmat, ctab):
    acc = jnp.zeros((n_rows, 512), F32)
    for l in range(16):
        acc = acc + _dot(tap_fn(l).astype(BF16), wcat_ref[l])
    top = acc[:, 0:256]
    bot = pltpu.roll(acc[:, 256:512], n_rows - 1, 0)
    kcv = top + bot + pe_term
    kc = _rope(_head_norm(kcv[:, 0:128], kn0, gmat), ctab)
    return kc.astype(BF16), kcv[:, 128:256]


def _compress_prompt_kernel(kr_ref, vr_ref, wcat_ref, pet_ref, kn0_ref, gmat_ref, ctab_ref, kc_ref, vc_ref):
    def tap(l):
        return jnp.concatenate([r[0, pl.ds(l, 128, stride=16), :] for r in (kr_ref, vr_ref)], axis=1)

    kc, vc = _compress(tap, 128, wcat_ref, pet_ref[0:1, :],
                       kn0_ref[...], gmat_ref[...], (ctab_ref[0], ctab_ref[1], ctab_ref[2]))
    kc_ref[0] = kc
    vc_ref[0] = vc.T.astype(BF16)


def _select_cols(imp_t, tpos):
    n_slc = imp_t.shape[0]
    sidx = lax.broadcasted_iota(jnp.int32, imp_t.shape, 0)
    qblk = jnp.right_shift(tpos, 6)
    forced = (sidx == 0) | (sidx == qblk) | (sidx == qblk - 1)
    score = jnp.where(forced, BIG, jnp.where(sidx <= qblk, imp_t, -BIG))
    rank = jnp.zeros(imp_t.shape, F32)
    for s in range(n_slc):
        row = score[s:s + 1, :]
        tie = jnp.where(sidx > s, 1.0, 0.0)
        rank = rank + jnp.where(row > score, 1.0, jnp.where(row == score, tie, 0.0))
    return jnp.where((rank < N_SEL) & (score > -0.5 * BIG), 1.0, 0.0)


def _attn_prompt_kernel(q_ref, kk_ref, vt_ref, kc_ref, vct_ref, ga_ref, egt_ref, ovt_ref, o_ref):
    qt = pl.program_id(1)
    t0 = qt * TQ
    q = q_ref[0]
    tpos = t0 + lax.broadcasted_iota(jnp.int32, (1, TQ), 1)
    n_cmp = lax.broadcasted_iota(jnp.int32, (LANES, TQ), 0)
    cmask = jnp.where(((n_cmp * CMP_STRIDE + (CMP_BLK - 1)) <= tpos) & (n_cmp < 127), 1.0, 0.0)
    cbias = (cmask - 1.0) * (-NEG)
    cw0 = jnp.maximum(qt - 2, 0)
    w0 = pl.multiple_of(cw0 * TQ, TQ)
    wdiff = tpos - (w0 + lax.broadcasted_iota(jnp.int32, (3 * TQ, 1), 0))
    wbias = jnp.where((wdiff >= 0) & (wdiff <= WINDOW), 0.0, NEG)
    causal = jnp.where((t0 + lax.broadcasted_iota(jnp.int32, (TQ, 1), 0)) <= tpos, 0.0, NEG)
    ga_t = ga_ref[0].T
    ga_hi = ga_t.astype(BF16)
    ga_lo = (ga_t - ga_hi.astype(F32)).astype(BF16)
    gates_t = _dot(egt_ref[...], ga_hi) + _dot(egt_ref[...], ga_lo)

    def add4(s, b):
        return jnp.concatenate([s[:, j * TQ:(j + 1) * TQ] + b for j in range(GROUP_A)], axis=1)

    def with_ones(vt):
        return jnp.concatenate([vt, jnp.ones((16, vt.shape[1]), BF16)], axis=0)

    def blk_scores(k_rows, qh, add_bias):
        cols, maxs = [], []
        for j in range(GROUP_A):
            blocks = [add_bias(i, _dot_t(k_rows[i * KB:(i + 1) * KB, :], qh[j]))
                      for i in range(k_rows.shape[0] // KB)]
            mx = jnp.max(blocks[0], axis=0, keepdims=True)
            for b in blocks[1:]:
                mx = jnp.maximum(mx, jnp.max(b, axis=0, keepdims=True))
            cols.append(jnp.concatenate(blocks, axis=0))
            maxs.append(mx)
        return jnp.concatenate(cols, axis=1), jnp.concatenate(maxs, axis=1)

    def blk_pv(vt_blk, s, m):
        cols = []
        for j in range(GROUP_A):
            o = jnp.zeros((80, TQ), F32)
            for i in range(s.shape[0] // KB):
                p = jnp.exp(s[i * KB:(i + 1) * KB, j * TQ:(j + 1) * TQ] - m[:, j * TQ:(j + 1) * TQ])
                o = o + _dot(vt_blk(i), p.astype(BF16))
            cols.append(o)
        return jnp.concatenate(cols, axis=1)

    groups = range(N_KV_A)
    qhs = [[q[:, (4 * g + j) * 64:(4 * g + j + 1) * 64] for j in range(GROUP_A)] for g in groups]
    s_cmps = [add4(_dot_t(kc_ref[0][:, g * 64:(g + 1) * 64], jnp.concatenate(qhs[g], axis=0)), cbias)
              for g in groups]
    wins = [blk_scores(kk_ref[0, pl.ds(w0, 3 * TQ), 256 + g * 64:256 + (g + 1) * 64], qhs[g],
                       lambda i, blk: blk + wbias[i * KB:(i + 1) * KB, :])
            for g in groups]

    o_cmps, o_wins, qps = [], [], []
    for g in groups:
        lo, hi = g * 64, (g + 1) * 64
        s_cmp = s_cmps[g]
        p = jnp.exp(s_cmp - jnp.max(s_cmp, axis=0, keepdims=True))
        p = jnp.concatenate([p[:, j * TQ:(j + 1) * TQ] * cmask for j in range(GROUP_A)], axis=1)
        den = jnp.sum(p, axis=0, keepdims=True)
        p = p / jnp.where(den > 0.0, den, 1.0)
        o_cmps.append(_dot(vct_ref[0][lo:hi, :], p.astype(BF16)))
        p4 = p[:, 0:TQ] + p[:, TQ:2 * TQ] + p[:, 2 * TQ:3 * TQ] + p[:, 3 * TQ:4 * TQ]
        p4_hi = p4.astype(BF16)
        p4_lo = (p4 - p4_hi.astype(F32)).astype(BF16)
        imp_t = _dot(ovt_ref[...], p4_hi) + _dot(ovt_ref[...], p4_lo)
        sel_bias = (_select_cols(imp_t[0:32, :], tpos) - 1.0) * (-NEG)
        bias_t = jnp.concatenate([sel_bias, jnp.zeros((LANES - T_SLC, TQ), F32)], axis=0).T
        bias_t = bias_t[:, 0:64].astype(BF16)
        qps.append([jnp.concatenate([qhs[g][j], bias_t], axis=1) for j in range(GROUP_A)])

    for g in groups:
        s_win, m_win = wins[g]
        o_win = blk_pv(lambda i: with_ones(vt_ref[0, cw0 + i // 2, 128 + g * 64:128 + (g + 1) * 64,
                                                  (i % 2) * KB:(i % 2 + 1) * KB]), s_win, m_win)
        o_wins.append(o_win[0:64, :] / o_win[64:65, :])

    def step(c, carry, diagonal):
        k0 = pl.multiple_of(c * TQ, TQ)
        scored = []
        for g in groups:
            def add_bias(i, blk):
                return blk + causal[i * KB:(i + 1) * KB, :] if diagonal else blk

            scored.append(blk_scores(kk_ref[0, pl.ds(k0, TQ), g * LANES:(g + 1) * LANES], qps[g], add_bias))
        new = []
        for g in groups:
            s, smax = scored[g]
            m_i, acc = carry[g]
            m_new = jnp.maximum(m_i, smax)
            pv = blk_pv(lambda i: with_ones(vt_ref[0, c, g * 64:(g + 1) * 64, i * KB:(i + 1) * KB]), s, m_new)
            new.append((m_new, jnp.exp(m_i - m_new) * acc + pv))
        return tuple(new)

    init = tuple((jnp.full((1, GROUP_A * TQ), NEG, F32), jnp.zeros((80, GROUP_A * TQ), F32)) for _ in groups)
    final = step(qt, lax.fori_loop(0, qt, lambda c, carry: step(c, carry, False), init), True)

    outs = [[], [], []]
    for g in groups:
        acc = final[g][1]
        o_slc = acc[0:64, :] / acc[64:65, :]
        for c, o in enumerate((o_cmps[g], o_slc, o_wins[g])):
            outs[c] += [o[:, j * TQ:(j + 1) * TQ] for j in range(GROUP_A)]

    o_t = jnp.zeros((512, TQ), F32)
    for c in range(3):
        o_t = o_t + gates_t[c * 512:(c + 1) * 512, :] * jnp.concatenate(outs[c], axis=0)
    o_ref[0] = o_t.T.astype(BF16)


def _post_prompt_kernel(x_ref, on_ref, gm0_ref, part_ref, wonsa_ref, wout_ref, nf_ref, wup_ref, fcw_ref,
                        fcb_ref, wdn_ref, y_ref, fst_ref, ubuf):
    @pl.when(pl.program_id(1) == 0)
    def _():
        ubuf[0:8, :] = jnp.zeros((8, D_FF), F32)

    y_a = _dot(on_ref[0], wonsa_ref[...])
    merged = gm0_ref[0].astype(F32) * y_a + part_ref[0].astype(F32)
    x1 = x_ref[0] + _dot(merged.astype(BF16), wout_ref[...])

    def prev2(u):
        ubuf[8:8 + TM, :] = u
        return ubuf[pl.ds(6, TM), :]

    def prev1(u):
        return ubuf[pl.ds(7, TM), :]

    y, u = _ffn_tail(x1, nf_ref[...], wup_ref, prev2, prev1, fcw_ref[...], fcb_ref[...], wdn_ref)
    y_ref[0] = y
    fst_ref[0] = u[TM - 2:TM, :]
    ubuf[0:8, :] = ubuf[TM:TM + 8, :]


def _pre_sample_kernel(x_ref, na_ref, wa_ref, wg_ref, w_ref, qn_ref, kn1_ref, kn2_ref, gmat_ref, tab_ref,
                       cw_ref, cb_ref, lng_ref, lnb_ref, woc_ref, cc_ref, mqn_ref, eg_ref,
                       q_ref, rows_ref, win_ref, gae_ref, qm_ref, gm_ref, yb_ref, cst_ref):
    gmat = gmat_ref[...]
    tab = (tab_ref[0], tab_ref[1], tab_ref[2])
    h = _rms(x_ref[...], na_ref[...]).astype(BF16)
    zq, zkv, zga, zglu = _project_qkv(h, wa_ref, wg_ref, w_ref)
    q, rows, win, ga = _in_proj_common(zq, zkv, zga, qn_ref[...], kn1_ref[...], kn2_ref[...], gmat, tab)
    zqm, zgm = _project_rest(h, w_ref)
    q_ref[...] = q
    rows_ref[...] = rows
    win_ref[...] = win
    gae_ref[...] = _dot_hilo(ga, eg_ref[...])
    glu = _glu(zglu)
    cw = cw_ref[...]
    c = cw[CONV_W - 1:CONV_W] * glu + cb_ref[...]
    for k in range(CONV_W - 1):
        c = c + cw[k:k + 1] * cc_ref[k]
    for k in range(CONV_W - 2):
        cst_ref[k] = cc_ref[k + 1]
    cst_ref[CONV_W - 2] = glu
    yb_ref[...] = _conv_tail(c, lng_ref[...], lnb_ref[...], woc_ref)
    qm_ref[...] = _mem_q(zqm, mqn_ref[...], gmat)
    gm_ref[...] = jax.nn.sigmoid(zgm)


def _heads_to_lanes(o8):
    lane = lax.broadcasted_iota(jnp.int32, (1, LANES), 1)
    blocks = []
    for cb in range(4):
        a = o8[2 * cb:2 * cb + 1, :]
        b = o8[2 * cb + 1:2 * cb + 2, :]
        if cb // 2 == 1:
            a = pltpu.roll(a, 64, 1)
        else:
            b = pltpu.roll(b, 64, 1)
        blocks.append(jnp.where(lane < 64, a, b))
    return jnp.concatenate(blocks, axis=1)


def _attn_sample_kernel(pt_ref, *refs):
    del pt_ref
    page_refs = refs[:TB * 16]
    (q_ref, rn_ref, wn_ref, gae_ref, qm_ref, cw_ref, cm_ref, wcat_ref, pet_ref, kn0_ref, gmat_ref, ctab_ref,
     ov_ref, ex_ref, pm_ref, on_ref, om_ref, wo_ref) = refs[TB * 16:]
    lane = lax.broadcasted_iota(jnp.int32, (1, LANES), 1)
    row8 = lax.broadcasted_iota(jnp.int32, (8, LANES), 0)
    lane8 = lax.broadcasted_iota(jnp.int32, (8, LANES), 1)

    pm = pm_ref[...]
    taps = [_dot_t(pm, page_refs[i][0, 0:256, :].astype(BF16)) for i in range(TB * 16)]

    def tap(l):
        return jnp.concatenate([taps[i][l * 8:(l + 1) * 8, :] for i in range(TB * 16)], axis=0)

    kc_all, vc_all = _compress(tap, TB * 128, wcat_ref, pet_ref[0:1, :], kn0_ref[...], gmat_ref[...],
                               tuple(jnp.concatenate([ctab_ref[i]] * TB, axis=0) for i in range(3)))

    toks = range(TB)
    rowm = lax.broadcasted_iota(jnp.int32, (8, 256), 0)
    headm = jnp.right_shift(lax.broadcasted_iota(jnp.int32, (8, 256), 1), 6)

    def bf(x):
        return x.astype(BF16).astype(F32)

    q8s, s_slc, s_win, s_mem = [], [], [], []
    for tok in toks:
        q = q_ref[tok]
        q_rows = []
        for r in range(N_HEADS_A):
            piece = q[:, (r // 2) * LANES:(r // 2 + 1) * LANES]
            if (r % 2) != (r // 4):
                piece = pltpu.roll(piece, 64, 1)
            q_rows.append(jnp.where(jnp.right_shift(lane, 6) == (r // 4), piece, 0.0))
        q8 = jnp.concatenate(q_rows, axis=0).astype(BF16)
        q8s.append(q8)
        s_slc.append(jnp.concatenate(
            [_dot(q8, page_refs[tok * 16 + i][0, 256:384, :].astype(BF16)) for i in range(16)], axis=1))
        s_win.append(_dot(q8, cw_ref[tok, 0:128, :].astype(BF16)))
        qm8 = jnp.where(rowm == headm, jnp.broadcast_to(qm_ref[tok], (8, 256)), 0.0).astype(BF16)
        s_mem.append(_dot(qm8, cm_ref[tok, 0:256, :].astype(BF16)))

    o_cmp, imp8 = [], []
    for tok in toks:
        s = jnp.where(lane8 < 127, _dot_t(q8s[tok], kc_all[tok * 128:(tok + 1) * 128]), NEG)
        p, den = _softmax_rows(s)
        p = p / den
        o_cmp.append(_dot(p.astype(BF16), vc_all[tok * 128:(tok + 1) * 128].astype(BF16)))
        imp8.append(_dot_hilo(p, ov_ref[...]))

    o_win = []
    for tok in toks:
        wn = wn_ref[tok]
        s = s_win[tok]
        s_new = jnp.sum(q8s[tok].astype(F32) * bf(wn[:, 0:128]), axis=-1, keepdims=True)
        m = jnp.maximum(jnp.max(s, axis=-1, keepdims=True), s_new)
        p = jnp.exp(s - m)
        p_new = jnp.exp(s_new - m)
        den = jnp.sum(p, axis=-1, keepdims=True) + p_new
        o_win.append((_dot_t(p.astype(BF16), cw_ref[tok, 128:256, :].astype(BF16))
                      + bf(p_new) * bf(wn[:, 128:256])) / den)
        p, den = _softmax_rows(s_mem[tok])
        o8 = _dot_t(p.astype(BF16), cm_ref[tok, 256:512, :].astype(BF16)) / den
        om_ref[tok] = jnp.sum(jnp.where(rowm == headm, o8, 0.0), axis=0, keepdims=True)
        cw = cw_ref[tok]
        rolled = pltpu.roll(cw, WINDOW - 1, 1)
        last = lax.broadcasted_iota(jnp.int32, (1, WINDOW), 1) == WINDOW - 1
        wn_col = jnp.broadcast_to(wn, (LANES, 256)).T[:, 0:1]
        wo_ref[tok] = jnp.where(last, wn_col, rolled)

    selx = []
    for tok in toks:
        sel_rows = []
        for g in range(N_KV_A):
            imp = jnp.sum(imp8[tok][4 * g:4 * g + 4, :], axis=0, keepdims=True)
            forced = (lane == 0) | (lane == 31) | (lane == 32)
            score = jnp.where(lane < 33, jnp.where(forced, BIG, imp), -3.0 * BIG)
            a = jnp.broadcast_to(score, (LANES, LANES))
            b = a.T
            sub = lax.broadcasted_iota(jnp.int32, (LANES, LANES), 0)
            ln = lax.broadcasted_iota(jnp.int32, (LANES, LANES), 1)
            beats = (b > a) | ((b == a) & (sub < ln))
            rank = jnp.sum(jnp.where(beats, 1.0, 0.0), axis=0, keepdims=True)
            sel_rows.append(jnp.where((rank < N_SEL) & (lane < 33), 1.0, 0.0))
        sel8 = jnp.where(row8 < 4, sel_rows[0], sel_rows[1]).astype(BF16)
        selx.append(_dot(sel8, ex_ref[...]))

    for tok in toks:
        rn = rn_ref[tok]
        gae = gae_ref[tok]
        s = jnp.where(selx[tok] > 0.5, s_slc[tok], NEG)
        s_new = jnp.sum(q8s[tok].astype(F32) * bf(rn[:, 256:384]), axis=-1, keepdims=True)
        m = jnp.maximum(jnp.max(s, axis=-1, keepdims=True), s_new)
        p = jnp.exp(s - m)
        p_new = jnp.exp(s_new - m)
        den = jnp.sum(p, axis=-1, keepdims=True) + p_new
        o_slc = bf(p_new) * bf(rn[:, 384:512])
        for i in range(16):
            o_slc = o_slc + _dot_t(p[:, i * PAGE:(i + 1) * PAGE].astype(BF16),
                                   page_refs[tok * 16 + i][0, 384:512, :].astype(BF16))
        o_slc = o_slc / den
        on_ref[tok] = (gae[:, 0:512] * _heads_to_lanes(o_cmp[tok]) + gae[:, 512:1024] * _heads_to_lanes(o_slc)
                       + gae[:, 1024:1536] * _heads_to_lanes(o_win[tok]))


def _post_sample_kernel(x_ref, on_ref, om_ref, gm_ref, yb_ref, wonsa_ref, wom_ref, wout_ref, nf_ref, wup_ref,
                        fcw_ref, fcb_ref, wdn_ref, cf_ref, y_ref, fst_ref):
    y_a = _dot(on_ref[...].astype(BF16), wonsa_ref[...])
    y_m = _dot(om_ref[...].astype(BF16), wom_ref[...])
    gm = gm_ref[...]
    merged = gm[:, 0:1024] * y_a + gm[:, 1024:2048] * yb_ref[...] + gm[:, 2048:3072] * y_m
    x1 = x_ref[...] + _dot(merged.astype(BF16), wout_ref[...])
    y, u = _ffn_tail(x1, nf_ref[...], wup_ref, lambda u: cf_ref[:, 0, :], lambda u: cf_ref[:, 1, :],
                     fcw_ref[...], fcb_ref[...], wdn_ref)
    y_ref[...] = y
    fst_ref[:, 0, :] = cf_ref[:, 1, :]
    fst_ref[:, 1, :] = u


def _rope_tables(pos):
    inv = ROPE_THETA ** (-jnp.arange(0, ROPE_DIM, 2, dtype=F32) / ROPE_DIM)
    ang = pos.astype(F32)[:, None] * inv
    cos, sin = jnp.cos(ang), jnp.sin(ang)
    n = pos.shape[0]
    one = jnp.ones((n, HEAD_DIM - ROPE_DIM), F32)
    z8 = jnp.zeros((n, 8), F32)
    z48 = jnp.zeros((n, HEAD_DIM - ROPE_DIM), F32)
    c = jnp.concatenate([cos, cos, one], axis=1)
    s1 = jnp.concatenate([z8, sin, z48], axis=1)
    s2 = jnp.concatenate([-sin, z8, z48], axis=1)
    return jnp.stack([jnp.tile(c, (1, 2)), jnp.tile(s1, (1, 2)), jnp.tile(s2, (1, 2))])


def _tile2(v):
    return jnp.tile(v.reshape(1, HEAD_DIM), (1, 2))


def _const(shape):
    nd = len(shape)
    return pl.BlockSpec(shape, lambda *_: (0,) * nd, pipeline_mode=pl.Buffered(1))


def _params(*sem):
    return pltpu.CompilerParams(dimension_semantics=sem, vmem_limit_bytes=VMEM_LIMIT)


def kernel(x_prompt, x_sample, cache_nsa, cache_win, cache_conv, cache_ffn, cache_mem, page_table, mem_prompt,
           norm_attn, w_in, q_norm, k_norm, cmp_pe, w_cmp, w_o_nsa, conv_w, conv_b, conv_ln_g, conv_ln_b, w_o_conv,
           norm_mem, w_mem_kv, mq_norm, mk_norm, w_o_mem, w_out, norm_ffn, w_ffn_up, ffn_conv_w, ffn_conv_b,
           w_ffn_down):
    B, T, _ = x_prompt.shape
    NS = x_sample.shape[0]
    n_pages = page_table.shape[1]
    assert w_in.shape[0] == 1 and T == 2048 and n_pages * PAGE == 2048 and cache_win.shape[2] == WINDOW
    nt = T // TM
    assert TQ == TM

    w_in0 = w_in[0]
    w_qkv = w_in0[:, :N_QKV].astype(BF16)
    w_gate = jnp.pad(w_in0[:, N_QKV:N_QKV + N_GATES], ((0, 0), (0, LANES - N_GATES))).astype(BF16)
    w_rest = w_in0[:, N_QKV + N_GATES:].astype(BF16)
    na = norm_attn.reshape(1, D_MODEL)
    nf = norm_ffn.reshape(1, D_MODEL)
    nm = norm_mem.reshape(1, D_MODEL)
    qn, mqn, mkn = _tile2(q_norm[0]), _tile2(mq_norm[0]), _tile2(mk_norm[0])
    kn0, kn1, kn2 = _tile2(k_norm[0, 0]), _tile2(k_norm[0, 1]), _tile2(k_norm[0, 2])
    ii = jnp.arange(LANES)
    gmat = jnp.where((ii[:, None] // 64) == (ii[None, :] // 64), 1.0 / 64, 0.0).astype(BF16)
    tab_p = _rope_tables(jnp.arange(T))
    tab_s = _rope_tables(jnp.full((1,), n_pages * PAGE))
    ctab = _rope_tables(jnp.arange(128) * CMP_STRIDE + (CMP_BLK - 1))
    wk, wv = w_cmp[0, 0].astype(BF16), w_cmp[0, 1].astype(BF16)
    z = jnp.zeros((CMP_BLK, HEAD_DIM, HEAD_DIM), BF16)
    w_l = jnp.concatenate([jnp.concatenate(r, axis=-1) for r in
                           ([wk, z, z, z], [z, wk, z, z], [z, z, wv, z], [z, z, z, wv])], axis=1)
    wcat = jnp.concatenate([w_l[:16], w_l[16:]], axis=-1)
    pe2 = jnp.broadcast_to(cmp_pe[0].transpose(1, 0, 2)[:, :, None, :], (CMP_BLK, 2, 2, HEAD_DIM)).reshape(
        CMP_BLK, 256)
    col = jnp.arange(3 * 512)
    egate = (jnp.arange(LANES)[:, None] == ((col // 512) * 8 + (col % 512) // 64)[None, :]).astype(BF16)
    egate_t = egate.T
    cs = jnp.arange(LANES)[:, None] * CMP_STRIDE
    ss = jnp.arange(LANES)[None, :] * SLC_BLK
    overlap = ((cs < ss + SLC_BLK) & (cs + CMP_BLK > ss) & (jnp.arange(LANES)[:, None] < 127)
               & (jnp.arange(LANES)[None, :] < 33)).astype(BF16)
    kpos = jnp.arange(T)
    expand = ((jnp.arange(LANES)[:, None] == (kpos // SLC_BLK)[None, :])).astype(BF16)
    blk_onehot = expand[0:64, :].T
    woc = w_o_conv[0].astype(BF16)
    wom = w_o_mem[0].astype(BF16)
    wonsa = w_o_nsa[0].astype(BF16)
    wout = w_out[0].astype(BF16)
    wup = w_ffn_up[0].astype(BF16)
    wdn = w_ffn_down[0].astype(BF16)
    wmkv = w_mem_kv[0].astype(BF16)
    cw, cb = conv_w[0], conv_b.reshape(1, C_CONV)
    lng, lnb = conv_ln_g.reshape(1, C_CONV), conv_ln_b.reshape(1, C_CONV)
    fcw, fcb = ffn_conv_w[0], ffn_conv_b.reshape(1, D_FF)

    mem_kv = pl.pallas_call(
        _memkv_kernel, grid=(B,),
        in_specs=[pl.BlockSpec((1, N_MEM, D_MODEL), lambda b: (b, 0, 0)), _const((1, D_MODEL)),
                  _const((D_MODEL, 512)), _const((1, LANES)), _const((LANES, LANES))],
        out_specs=pl.BlockSpec((1, N_MEM, 512), lambda b: (b, 0, 0)),
        out_shape=jax.ShapeDtypeStruct((B, N_MEM, 512), F32),
        compiler_params=_params("arbitrary"), name="mem_kv",
    )(mem_prompt, nm, wmkv, mkn, gmat)

    def tile(width):
        return pl.BlockSpec((1, TM, width), lambda b, t: (b, t, 0))

    n_win_t = WINDOW // TM
    pre_out_shapes = (
        jax.ShapeDtypeStruct((B, T, 512), BF16),
        jax.ShapeDtypeStruct((B, T, 512), F32),
        jax.ShapeDtypeStruct((B, T, 384), BF16),
        jax.ShapeDtypeStruct((B, nt, 256, TM), BF16),
        jax.ShapeDtypeStruct((B, WINDOW, 256), F32),
        jax.ShapeDtypeStruct((B, T, LANES), F32),
        jax.ShapeDtypeStruct((B, T, D_MODEL), BF16),
        jax.ShapeDtypeStruct((B, T, D_MODEL), BF16),
        jax.ShapeDtypeStruct((B, CONV_W - 1, C_CONV), F32),
    )
    q_p, rows_p, kk_p, vt_p, win_p, ga_p, gm0_p, part_p, cst_p = pl.pallas_call(
        _pre_prompt_kernel, grid=(B, nt),
        in_specs=[tile(D_MODEL), _const((1, D_MODEL)), _const((D_MODEL, N_QKV)), _const((D_MODEL, LANES)),
                  _const((D_MODEL, N_REST)), _const((1, LANES)),
                  _const((1, LANES)), _const((1, LANES)), _const((LANES, LANES)),
                  pl.BlockSpec((3, TM, LANES), lambda b, t: (0, t, 0)),
                  pl.BlockSpec((TM, 64), lambda b, t: (t, 0)),
                  _const((CONV_W, C_CONV)), _const((1, C_CONV)), _const((1, C_CONV)), _const((1, C_CONV)),
                  _const((C_CONV, D_MODEL)),
                  pl.BlockSpec((1, N_MEM, 512), lambda b, t: (b, 0, 0)), _const((1, LANES)),
                  _const((256, D_MODEL))],
        out_specs=(tile(512), tile(512), tile(384),
                   pl.BlockSpec((1, 1, 256, TM), lambda b, t: (b, t, 0, 0)),
                   pl.BlockSpec((1, TM, 256), lambda b, t: (b, jnp.maximum(t - (nt - n_win_t), 0), 0)),
                   tile(LANES), tile(D_MODEL), tile(D_MODEL),
                   pl.BlockSpec((1, CONV_W - 1, C_CONV), lambda b, t: (b, 0, 0))),
        out_shape=pre_out_shapes,
        scratch_shapes=[pltpu.VMEM((TM + 32, C_CONV), F32), pltpu.VMEM((7, TM + 24, C_CONV), F32)],
        compiler_params=_params("arbitrary", "arbitrary"), name="pre_prompt",
    )(x_prompt, na, w_qkv, w_gate, w_rest, qn, kn1, kn2, gmat, tab_p, blk_onehot, cw, cb, lng, lnb, woc, mem_kv,
      mqn, wom)

    pe_term = pl.pallas_call(
        _pe_term_kernel, out_shape=jax.ShapeDtypeStruct((8, 256), F32), name="pe_term",
    )(pe2, wcat)
    kc_p, vc_p = pl.pallas_call(
        _compress_prompt_kernel, grid=(B,),
        in_specs=[pl.BlockSpec((1, T, LANES), lambda b: (b, 0, 0)), pl.BlockSpec((1, T, LANES), lambda b: (b, 0, 1)),
                  _const((16, 256, 512)), _const((8, 256)),
                  _const((1, LANES)), _const((LANES, LANES)), _const((3, LANES, LANES))],
        out_specs=(pl.BlockSpec((1, LANES, LANES), lambda b: (b, 0, 0)),) * 2,
        out_shape=(jax.ShapeDtypeStruct((B, LANES, LANES), BF16),) * 2,
        compiler_params=_params("arbitrary"), name="compress_prompt",
    )(rows_p, rows_p, wcat, pe_term, kn0, gmat, ctab)

    xs = x_sample.reshape(NS, D_MODEL)
    pre_s_shapes = (
        jax.ShapeDtypeStruct((NS, 512), F32), jax.ShapeDtypeStruct((NS, 512), F32),
        jax.ShapeDtypeStruct((NS, 256), F32), jax.ShapeDtypeStruct((NS, 3 * 512), F32),
        jax.ShapeDtypeStruct((NS, 256), F32), jax.ShapeDtypeStruct((NS, 3 * D_MODEL), F32),
        jax.ShapeDtypeStruct((NS, D_MODEL), F32), jax.ShapeDtypeStruct((CONV_W - 1, NS, C_CONV), F32),
    )
    q_s, rows_s, win_s, gae_s, qm_s, gm_s, yb_s, cst_s = pl.pallas_call(
        _pre_sample_kernel, out_shape=pre_s_shapes,
        compiler_params=pltpu.CompilerParams(vmem_limit_bytes=VMEM_LIMIT), name="pre_sample",
    )(xs, na, w_qkv, w_gate, w_rest, qn, kn1, kn2, gmat, tab_s, cw, cb, lng, lnb, woc,
      cache_conv[0].transpose(1, 0, 2), mqn, egate)
    cst_s = cst_s.transpose(1, 0, 2)

    pages = cache_nsa[0].transpose(0, 2, 3, 4, 1).reshape(cache_nsa.shape[1], 512, PAGE)
    cw_t = cache_win[0].transpose(0, 2, 3, 4, 1).reshape(NS, 256, WINDOW)
    cm_t = cache_mem[0].transpose(0, 2, 3, 4, 1).reshape(NS, 512, N_MEM)
    pt_flat = page_table.reshape(-1)

    def page_spec(tok, p):
        return pl.BlockSpec((1, 512, PAGE), lambda i, pt: (pt[(i * TB + tok) * n_pages + p], 0, 0))

    rr = jnp.arange(PAGE)
    perm = (rr[None, :] == ((rr % 8) * CMP_STRIDE + rr // 8)[:, None]).astype(BF16)

    def tok_spec(width):
        return pl.BlockSpec((TB, 1, width), lambda i, pt: (i, 0, 0))

    def cst_spec(shape):
        nd = len(shape)
        return pl.BlockSpec(shape, lambda i, pt: (0,) * nd, pipeline_mode=pl.Buffered(1))

    grid_spec = pltpu.PrefetchScalarGridSpec(
        num_scalar_prefetch=1, grid=(NS // TB,),
        in_specs=[page_spec(tok, p) for tok in range(TB) for p in range(n_pages)] + [
            tok_spec(512), tok_spec(512), tok_spec(256), tok_spec(3 * 512), tok_spec(256),
            pl.BlockSpec((TB, 256, WINDOW), lambda i, pt: (i, 0, 0)),
            pl.BlockSpec((TB, 512, N_MEM), lambda i, pt: (i, 0, 0)),
            cst_spec((16, 256, 512)), cst_spec((8, 256)), cst_spec((1, LANES)), cst_spec((LANES, LANES)),
            cst_spec((3, LANES, LANES)), cst_spec((LANES, LANES)), cst_spec((LANES, T)),
            cst_spec((PAGE, PAGE))],
        out_specs=(tok_spec(512), tok_spec(256), pl.BlockSpec((TB, 256, WINDOW), lambda i, pt: (i, 0, 0))),
    )
    on_s, om_s, wo_s = pl.pallas_call(
        _attn_sample_kernel, grid_spec=grid_spec,
        out_shape=(jax.ShapeDtypeStruct((NS, 1, 512), F32), jax.ShapeDtypeStruct((NS, 1, 256), F32),
                   jax.ShapeDtypeStruct((NS, 256, WINDOW), F32)),
        compiler_params=_params("arbitrary"), name="attn_sample",
    )(pt_flat, *([pages] * (TB * n_pages)), q_s.reshape(NS, 1, 512), rows_s.reshape(NS, 1, 512),
      win_s.reshape(NS, 1, 256), gae_s.reshape(NS, 1, 3 * 512), qm_s.reshape(NS, 1, 256),
      cw_t, cm_t, wcat, pe_term, kn0, gmat, ctab, overlap, expand, perm)
    wo_s = wo_s.reshape(NS, 2, N_KV_A, HEAD_DIM, WINDOW).transpose(0, 4, 1, 2, 3)

    o_nsa_p = pl.pallas_call(
        _attn_prompt_kernel, grid=(B, T // TQ),
        in_specs=[pl.BlockSpec((1, TQ, 512), lambda b, t: (b, t, 0)),
                  pl.BlockSpec((1, T, 384), lambda b, t: (b, 0, 0)),
                  pl.BlockSpec((1, nt, 256, TM), lambda b, t: (b, 0, 0, 0)),
                  pl.BlockSpec((1, LANES, LANES), lambda b, t: (b, 0, 0)),
                  pl.BlockSpec((1, LANES, LANES), lambda b, t: (b, 0, 0)),
                  pl.BlockSpec((1, TQ, LANES), lambda b, t: (b, t, 0)),
                  _const((3 * 512, LANES)), _const((LANES, LANES))],
        out_specs=pl.BlockSpec((1, TQ, 512), lambda b, t: (b, t, 0)),
        out_shape=jax.ShapeDtypeStruct((B, T, 512), BF16),
        compiler_params=_params("arbitrary", "arbitrary"), name="attn_prompt",
    )(q_p, kk_p, vt_p, kc_p, vc_p, ga_p, egate_t, overlap.T)

    y_p, fst_p = pl.pallas_call(
        _post_prompt_kernel, grid=(B, nt),
        in_specs=[tile(D_MODEL), tile(512), tile(D_MODEL), tile(D_MODEL), _const((512, D_MODEL)),
                  _const((D_MODEL, D_MODEL)), _const((1, D_MODEL)), _const((D_MODEL, 2 * D_FF)),
                  _const((FFN_CONV_W, D_FF)), _const((1, D_FF)), _const((D_FF, D_MODEL))],
        out_specs=(tile(D_MODEL), pl.BlockSpec((1, FFN_CONV_W - 1, D_FF), lambda b, t: (b, 0, 0))),
        out_shape=(jax.ShapeDtypeStruct((B, T, D_MODEL), F32),
                   jax.ShapeDtypeStruct((B, FFN_CONV_W - 1, D_FF), F32)),
        scratch_shapes=[pltpu.VMEM((TM + 8, D_FF), F32)],
        compiler_params=_params("arbitrary", "arbitrary"), name="post_prompt",
    )(x_prompt, o_nsa_p, gm0_p, part_p, wonsa, wout, nf, wup, fcw, fcb, wdn)

    y_s, fst_s = pl.pallas_call(
        _post_sample_kernel,
        out_shape=(jax.ShapeDtypeStruct((NS, D_MODEL), F32),
                   jax.ShapeDtypeStruct((NS, FFN_CONV_W - 1, D_FF), F32)),
        compiler_params=pltpu.CompilerParams(vmem_limit_bytes=VMEM_LIMIT), name="post_sample",
    )(xs, on_s.reshape(NS, 512), om_s.reshape(NS, 256), gm_s, yb_s, wonsa, wom, wout, nf, wup, fcw, fcb, wdn,
      cache_ffn[0])

    return (y_p, y_s.reshape(NS, 1, D_MODEL),
            rows_p.reshape(1, B, T, 4, N_KV_A, HEAD_DIM), rows_s.reshape(1, NS, 1, 4, N_KV_A, HEAD_DIM),
            win_p.reshape(1, B, WINDOW, 2, N_KV_A, HEAD_DIM), wo_s[None],
            cst_p[None], cst_s[None], fst_p[None], fst_s[None],
            mem_kv.reshape(1, B, N_MEM, 2, N_HEADS_M, HEAD_DIM))
```

```python
import jax
import jax.numpy as jnp
from jax import lax
from jax.experimental import pallas as pl
from jax.experimental.pallas import tpu as pltpu

F32 = jnp.float32
BF16 = jnp.bfloat16

D_MODEL = 1024
HEAD_DIM = 64
N_HEADS_A = 8
N_KV_A = 2
GROUP_A = 4
CMP_BLK = 32
CMP_STRIDE = 16
SLC_BLK = 64
N_SEL = 16
WINDOW = 512
C_CONV = 512
CONV_W = 31
N_MEM = 256
N_HEADS_M = 4
D_FF = 2816
FFN_CONV_W = 3
ROPE_THETA = 500000.0
ROPE_DIM = 16
EPS = 1e-6
BIG = 1e9
NEG = -1e30
LOG2_E = 1.4426950408889634
PAGE = 128

N_QKV = 1280
N_GATES = 24
C_GLU = 0
C_QM = 1024
C_GM = 1280
N_REST = 4352

LANES = 128
VMEM_LIMIT = 56 * 1024 * 1024

TM = 256
TQ = 256
TB = 4
KB = 128
T_SLC = 32


def _dot(a, b):
    return jnp.dot(a, b, preferred_element_type=F32)


def _dot_t(a, b):
    return lax.dot_general(a, b, (((1,), (1,)), ((), ())), preferred_element_type=F32)


def _dot_hilo(x, m):
    hi = x.astype(BF16)
    lo = (x - hi.astype(F32)).astype(BF16)
    return _dot(hi, m) + _dot(lo, m)


def _rms(x, g):
    return x * lax.rsqrt(jnp.mean(x * x, axis=-1, keepdims=True) + EPS) * g


def _head_norm(blk, gain, gmat):
    ms = _dot_hilo(blk * blk, gmat)
    return blk * lax.rsqrt(ms + EPS) * gain


def _rope(blk, tab):
    c, s1, s2 = tab
    return blk * c + pltpu.roll(blk, 8, 1) * s1 + pltpu.roll(blk, LANES - 8, 1) * s2


def _softmax_rows(s):
    m = jnp.max(s, axis=-1, keepdims=True)
    p = jnp.exp(s - m)
    return p, jnp.sum(p, axis=-1, keepdims=True)


def _project_qkv(h, wa_ref, wg_ref, w_ref):
    return (_dot(h, wa_ref[:, 0:512]), _dot(h, wa_ref[:, 512:N_QKV]), _dot(h, wg_ref[...]),
            _dot(h, w_ref[:, C_GLU:C_GLU + 1024]))


def _project_rest(h, w_ref):
    return _dot(h, w_ref[:, C_QM:C_QM + 256]), _dot(h, w_ref[:, C_GM:C_GM + 3072])


def _in_proj_common(zq, zkv, zga, qn, kn1, kn2, gmat, tab, q_scale):
    q_blocks = []
    for cb in range(4):
        blk = zq[:, cb * LANES:(cb + 1) * LANES]
        q_blocks.append(_rope(_head_norm(blk, qn, gmat), tab) * q_scale)
    q = jnp.concatenate(q_blocks, axis=1)
    k_slc = _rope(_head_norm(zkv[:, 256:384], kn1, gmat), tab)
    k_win = _rope(_head_norm(zkv[:, 512:640], kn2, gmat), tab)
    rows = jnp.concatenate([zkv[:, 0:256], k_slc, zkv[:, 384:512]], axis=1)
    win = jnp.concatenate([k_win, zkv[:, 640:768]], axis=1)
    return q, rows, win, jax.nn.sigmoid(zga)


def _glu(z):
    return z[:, :C_CONV] * jax.nn.sigmoid(z[:, C_CONV:])


def _conv_tail(c, lng, lnb, wo_ref):
    mu = jnp.mean(c, axis=-1, keepdims=True)
    var = jnp.mean(jnp.square(c - mu), axis=-1, keepdims=True)
    y = (c - mu) * lax.rsqrt(var + EPS) * lng + lnb
    return _dot(jax.nn.silu(y).astype(BF16), wo_ref[...])


def _mem_q(z, mqn, gmat):
    return jnp.concatenate(
        [_head_norm(z[:, cb * LANES:(cb + 1) * LANES], mqn, gmat) for cb in range(2)], axis=1) * (HEAD_DIM ** -0.5)


def _ffn_tail(x1, nf, wup_ref, u_prev2, u_prev1_fn, fcw, fcb, wdn_ref):
    h2 = _rms(x1, nf).astype(BF16)
    up = _dot(h2, wup_ref[...])
    u = up[:, :D_FF]
    v = up[:, D_FF:]
    uc = fcw[0:1] * u_prev2(u) + fcw[1:2] * u_prev1_fn(u) + fcw[2:3] * u + fcb
    act = jax.nn.gelu(uc, approximate=True) * v
    return x1 + _dot(act.astype(BF16), wdn_ref[...]), u


def _memkv_kernel(mem_ref, nm_ref, w_ref, mkn_ref, gmat_ref, o_ref):
    h = _rms(mem_ref[0], nm_ref[...]).astype(BF16)
    z = _dot(h, w_ref[...])
    gmat = gmat_ref[...]
    k = [_head_norm(z[:, cb * LANES:(cb + 1) * LANES], mkn_ref[...], gmat) for cb in range(2)]
    o_ref[0] = jnp.concatenate(k + [z[:, 256:512]], axis=1)


def _pre_prompt_kernel(x_ref, na_ref, wa_ref, wg_ref, w_ref, qn_ref, kn1_ref, kn2_ref, gmat_ref, tab_ref, oh_ref,
                       cw_ref, cb_ref, lng_ref, lnb_ref, woc_ref, mkv_ref, mqn_ref, wom_ref,
                       q_ref, rows_ref, kk_ref, vt_ref, win_ref, ga_ref, gm0_ref, part_ref, cst_ref, hbuf, sbuf):
    @pl.when(pl.program_id(1) == 0)
    def _():
        hbuf[0:32, :] = jnp.zeros((32, C_CONV), F32)

    gmat = gmat_ref[...]
    tab = (tab_ref[0], tab_ref[1], tab_ref[2])
    h = _rms(x_ref[0], na_ref[...]).astype(BF16)
    zq, zkv, zga, zglu = _project_qkv(h, wa_ref, wg_ref, w_ref)
    zqm = _dot(h, w_ref[:, C_QM:C_QM + 256])

    glu = _glu(zglu)
    hbuf[32:32 + TM, :] = glu
    cw = cw_ref[...]
    c = jnp.zeros((TM, C_CONV), F32) + cb_ref[...]
    zgm_chunks = []
    for r in range(8):
        if r < 6:
            zgm_chunks.append(_dot(h, w_ref[:, C_GM + r * 512:C_GM + (r + 1) * 512]))
        if r > 0:
            sbuf[r - 1] = hbuf[pl.ds(r, TM + 24), :]
        for a in range(5):
            k = 8 * a + r - 2
            if 0 <= k < CONV_W:
                src = hbuf[8 * a:8 * a + TM, :] if r == 0 else sbuf[r - 1, 8 * a:8 * a + TM, :]
                c = c + cw[k:k + 1] * src
    zgm = jnp.concatenate(zgm_chunks, axis=1)
    cst_ref[0] = hbuf[pl.ds(TM + 2, CONV_W - 1), :]
    hbuf[0:32, :] = hbuf[TM:TM + 32, :]

    q, rows, win, ga = _in_proj_common(zq, zkv, zga, qn_ref[...], kn1_ref[...], kn2_ref[...], gmat, tab,
                                       HEAD_DIM ** -0.5 * LOG2_E)
    q_ref[0] = q.astype(BF16)
    rows_ref[0] = rows
    win_ref[0] = win
    ga_ref[0] = ga
    oh = oh_ref[...]
    kk_ref[0] = jnp.concatenate([rows[:, 256:320].astype(BF16), oh, rows[:, 320:384].astype(BF16), oh,
                                 win[:, 0:128].astype(BF16)], axis=1)
    vt_ref[0, 0] = jnp.concatenate([rows[:, 384:512], win[:, 128:256]], axis=1).T.astype(BF16)

    y_b = _conv_tail(c, lng_ref[...], lnb_ref[...], woc_ref)

    qm = _mem_q(zqm, mqn_ref[...], gmat).astype(BF16)
    mkv = mkv_ref[0].astype(BF16)
    scores = [_dot_t(qm[:, hh * 64:(hh + 1) * 64], mkv[:, hh * 64:(hh + 1) * 64]) for hh in range(N_HEADS_M)]
    probs = [_softmax_rows(s) for s in scores]
    heads = [_dot(p.astype(BF16), mkv[:, 256 + hh * 64:256 + (hh + 1) * 64]) / den
             for hh, (p, den) in enumerate(probs)]
    y_m = _dot(jnp.concatenate(heads, axis=1).astype(BF16), wom_ref[...])

    gm = jax.nn.sigmoid(zgm)
    gm0_ref[0] = gm[:, 0:1024].astype(BF16)
    part_ref[0] = (gm[:, 1024:2048] * y_b + gm[:, 2048:3072] * y_m).astype(BF16)


def _pe_term_kernel(pe_ref, wcat_ref, o_ref):
    acc = jnp.zeros((8, 256), F32)
    for l in range(16):
        top = jnp.broadcast_to(pe_ref[l:l + 1, :], (8, 256)).astype(BF16)
        bot = jnp.broadcast_to(pe_ref[l + 16:l + 17, :], (8, 256)).astype(BF16)
        acc = acc + _dot(top, wcat_ref[l][:, 0:256]) + _dot(bot, wcat_ref[l][:, 256:512])
    o_ref[...] = acc


def _compress(tap_fn, n_rows, wcat_ref, pe_term, kn0, gmat, ctab):
    acc = jnp.zeros((n_rows, 512), F32)
    for l in range(16):
        acc = acc + _dot(tap_fn(l).astype(BF16), wcat_ref[l])
    top = acc[:, 0:256]
    bot = pltpu.roll(acc[:, 256:512], n_rows - 1, 0)
    kcv = top + bot + pe_term
    kc = _rope(_head_norm(kcv[:, 0:128], kn0, gmat), ctab)
    return kc.astype(BF16), kcv[:, 128:256]


def _compress_prompt_kernel(kr_ref, vr_ref, wcat_ref, pet_ref, kn0_ref, gmat_ref, ctab_ref, kc_ref, vc_ref):
    def tap(l):
        return jnp.concatenate([r[0, pl.ds(l, 128, stride=16), :] for r in (kr_ref, vr_ref)], axis=1)

    kc, vc = _compress(tap, 128, wcat_ref, pet_ref[0:1, :],
                       kn0_ref[...], gmat_ref[...], (ctab_ref[0], ctab_ref[1], ctab_ref[2]))
    kc_ref[0] = kc
    vc_ref[0] = vc.T.astype(BF16)


def _select_cols(imp_t, tpos):
    n_slc = imp_t.shape[0]
    sidx = lax.broadcasted_iota(jnp.int32, imp_t.shape, 0)
    qblk = jnp.right_shift(tpos, 6)
    forced = (sidx == 0) | (sidx == qblk) | (sidx == qblk - 1)
    score = jnp.where(forced, BIG, jnp.where(sidx <= qblk, imp_t, -BIG))
    rank = jnp.zeros(imp_t.shape, F32)
    for s in range(n_slc):
        row = score[s:s + 1, :]
        tie = jnp.where(sidx > s, 1.0, 0.0)
        rank = rank + jnp.where(row > score, 1.0, jnp.where(row == score, tie, 0.0))
    return jnp.where((rank < N_SEL) & (score > -0.5 * BIG), 1.0, 0.0)


def _attn_prompt_kernel(q_ref, kk_ref, vt_ref, kc_ref, vct_ref, ga_ref, egt_ref, ovt_ref, o_ref):
    qt = pl.program_id(1)
    t0 = qt * TQ
    q = q_ref[0]
    tpos = t0 + lax.broadcasted_iota(jnp.int32, (1, TQ), 1)
    n_cmp = lax.broadcasted_iota(jnp.int32, (LANES, TQ), 0)
    cmask = jnp.where(((n_cmp * CMP_STRIDE + (CMP_BLK - 1)) <= tpos) & (n_cmp < 127), 1.0, 0.0)
    cbias = (cmask - 1.0) * (-NEG)
    cw0 = jnp.maximum(qt - 2, 0)
    w0 = pl.multiple_of(cw0 * TQ, TQ)
    wdiff = tpos - (w0 + lax.broadcasted_iota(jnp.int32, (3 * TQ, 1), 0))
    wbias = jnp.where((wdiff >= 0) & (wdiff <= WINDOW), 0.0, NEG)
    causal = jnp.where((t0 + lax.broadcasted_iota(jnp.int32, (TQ, 1), 0)) <= tpos, 0.0, NEG)
    ga_t = ga_ref[0].T
    ga_hi = ga_t.astype(BF16)
    ga_lo = (ga_t - ga_hi.astype(F32)).astype(BF16)
    gates_t = _dot(egt_ref[...], ga_hi) + _dot(egt_ref[...], ga_lo)

    def add4(s, b):
        return jnp.concatenate([s[:, j * TQ:(j + 1) * TQ] + b for j in range(GROUP_A)], axis=1)

    def with_ones(vt):
        return jnp.concatenate([vt, jnp.ones((16, vt.shape[1]), BF16)], axis=0)

    def blk_scores(k_rows, qh, add_bias):
        cols, maxs = [], []
        for j in range(GROUP_A):
            blocks = [add_bias(i, _dot_t(k_rows[i * KB:(i + 1) * KB, :], qh[j]))
                      for i in range(k_rows.shape[0] // KB)]
            mx = jnp.max(blocks[0], axis=0, keepdims=True)
            for b in blocks[1:]:
                mx = jnp.maximum(mx, jnp.max(b, axis=0, keepdims=True))
            cols.append(jnp.concatenate(blocks, axis=0))
            maxs.append(mx)
        return jnp.concatenate(cols, axis=1), jnp.concatenate(maxs, axis=1)

    def blk_pv(vt_blk, s, m):
        cols = []
        for j in range(GROUP_A):
            o = jnp.zeros((80, TQ), F32)
            for i in range(s.shape[0] // KB):
                p = jnp.exp2(s[i * KB:(i + 1) * KB, j * TQ:(j + 1) * TQ] - m[:, j * TQ:(j + 1) * TQ])
                o = o + _dot(vt_blk(i), p.astype(BF16))
            cols.append(o)
        return jnp.concatenate(cols, axis=1)

    groups = range(N_KV_A)
    qhs = [[q[:, (4 * g + j) * 64:(4 * g + j + 1) * 64] for j in range(GROUP_A)] for g in groups]
    s_cmps = [add4(_dot_t(kc_ref[0][:, g * 64:(g + 1) * 64], jnp.concatenate(qhs[g], axis=0)), cbias)
              for g in groups]
    wins = [blk_scores(kk_ref[0, pl.ds(w0, 3 * TQ), 256 + g * 64:256 + (g + 1) * 64], qhs[g],
                       lambda i, blk: blk + wbias[i * KB:(i + 1) * KB, :])
            for g in groups]

    o_cmps, o_wins, qps = [], [], []
    for g in groups:
        lo, hi = g * 64, (g + 1) * 64
        s_cmp = s_cmps[g]
        p = jnp.exp2(s_cmp - jnp.max(s_cmp, axis=0, keepdims=True))
        p = jnp.concatenate([p[:, j * TQ:(j + 1) * TQ] * cmask for j in range(GROUP_A)], axis=1)
        den = jnp.sum(p, axis=0, keepdims=True)
        p = p / jnp.where(den > 0.0, den, 1.0)
        o_cmps.append(_dot(vct_ref[0][lo:hi, :], p.astype(BF16)))
        p4 = p[:, 0:TQ] + p[:, TQ:2 * TQ] + p[:, 2 * TQ:3 * TQ] + p[:, 3 * TQ:4 * TQ]
        p4_hi = p4.astype(BF16)
        p4_lo = (p4 - p4_hi.astype(F32)).astype(BF16)
        imp_t = _dot(ovt_ref[...], p4_hi) + _dot(ovt_ref[...], p4_lo)
        sel_bias = (_select_cols(imp_t[0:32, :], tpos) - 1.0) * (-NEG)
        bias_t = jnp.concatenate([sel_bias, jnp.zeros((LANES - T_SLC, TQ), F32)], axis=0).T
        bias_t = bias_t[:, 0:64].astype(BF16)
        qps.append([jnp.concatenate([qhs[g][j], bias_t], axis=1) for j in range(GROUP_A)])

    for g in groups:
        s_win, m_win = wins[g]
        o_win = blk_pv(lambda i: with_ones(vt_ref[0, cw0 + i // 2, 128 + g * 64:128 + (g + 1) * 64,
                                                  (i % 2) * KB:(i % 2 + 1) * KB]), s_win, m_win)
        o_wins.append(o_win[0:64, :] / o_win[64:65, :])

    def step(c, carry, diagonal):
        k0 = pl.multiple_of(c * TQ, TQ)
        scored = []
        for g in groups:
            def add_bias(i, blk):
                return blk + causal[i * KB:(i + 1) * KB, :] if diagonal else blk

            scored.append(blk_scores(kk_ref[0, pl.ds(k0, TQ), g * LANES:(g + 1) * LANES], qps[g], add_bias))
        new = []
        for g in groups:
            s, smax = scored[g]
            m_i, acc = carry[g]
            m_new = jnp.maximum(m_i, smax)
            pv = blk_pv(lambda i: with_ones(vt_ref[0, c, g * 64:(g + 1) * 64, i * KB:(i + 1) * KB]), s, m_new)
            new.append((m_new, jnp.exp2(m_i - m_new) * acc + pv))
        return tuple(new)

    units = [(i, g, j) for i in range(TQ // KB) for g in groups for j in range(GROUP_A)]

    def loop_step(c, carry):
        k0 = pl.multiple_of(c * TQ, TQ)
        ms = [[carry[g][0][:, j * TQ:(j + 1) * TQ] for j in range(GROUP_A)] for g in groups]
        accs = [[carry[g][1][:, j * TQ:(j + 1) * TQ] for j in range(GROUP_A)] for g in groups]
        scores = [_dot_t(kk_ref[0, pl.ds(k0 + i * KB, KB), g * LANES:(g + 1) * LANES], qps[g][j])
                  for i, g, j in units]
        for s, (i, g, j) in zip(scores, units):
            m_new = jnp.maximum(ms[g][j], jnp.max(s, axis=0, keepdims=True))
            p = jnp.exp2(s - m_new).astype(BF16)
            vt = with_ones(vt_ref[0, c, g * 64:(g + 1) * 64, i * KB:(i + 1) * KB])
            accs[g][j] = jnp.exp2(ms[g][j] - m_new) * accs[g][j] + _dot(vt, p)
            ms[g][j] = m_new
        return tuple((jnp.concatenate(ms[g], axis=1), jnp.concatenate(accs[g], axis=1)) for g in groups)

    init = tuple((jnp.full((1, GROUP_A * TQ), NEG, F32), jnp.zeros((80, GROUP_A * TQ), F32)) for _ in groups)
    final = step(qt, lax.fori_loop(0, qt, loop_step, init), True)

    outs = [[], [], []]
    for g in groups:
        acc = final[g][1]
        o_slc = acc[0:64, :] / acc[64:65, :]
        for c, o in enumerate((o_cmps[g], o_slc, o_wins[g])):
            outs[c] += [o[:, j * TQ:(j + 1) * TQ] for j in range(GROUP_A)]

    o_t = jnp.zeros((512, TQ), F32)
    for c in range(3):
        o_t = o_t + gates_t[c * 512:(c + 1) * 512, :] * jnp.concatenate(outs[c], axis=0)
    o_ref[0] = o_t.T.astype(BF16)


def _post_prompt_kernel(x_ref, on_ref, gm0_ref, part_ref, wonsa_ref, wout_ref, nf_ref, wup_ref, fcw_ref,
                        fcb_ref, wdn_ref, y_ref, fst_ref, ubuf):
    @pl.when(pl.program_id(1) == 0)
    def _():
        ubuf[0:8, :] = jnp.zeros((8, D_FF), F32)

    y_a = _dot(on_ref[0], wonsa_ref[...])
    merged = gm0_ref[0].astype(F32) * y_a + part_ref[0].astype(F32)
    x1 = x_ref[0] + _dot(merged.astype(BF16), wout_ref[...])

    def prev2(u):
        ubuf[8:8 + TM, :] = u
        return ubuf[pl.ds(6, TM), :]

    def prev1(u):
        return ubuf[pl.ds(7, TM), :]

    y, u = _ffn_tail(x1, nf_ref[...], wup_ref, prev2, prev1, fcw_ref[...], fcb_ref[...], wdn_ref)
    y_ref[0] = y
    fst_ref[0] = u[TM - 2:TM, :]
    ubuf[0:8, :] = ubuf[TM:TM + 8, :]


def _pre_sample_kernel(x_ref, na_ref, wa_ref, wg_ref, w_ref, qn_ref, kn1_ref, kn2_ref, gmat_ref, tab_ref,
                       cw_ref, cb_ref, lng_ref, lnb_ref, woc_ref, cc_ref, mqn_ref, eg_ref,
                       q_ref, rows_ref, win_ref, gae_ref, qm_ref, gm_ref, yb_ref, cst_ref):
    gmat = gmat_ref[...]
    tab = (tab_ref[0], tab_ref[1], tab_ref[2])
    h = _rms(x_ref[...], na_ref[...]).astype(BF16)
    zq, zkv, zga, zglu = _project_qkv(h, wa_ref, wg_ref, w_ref)
    q, rows, win, ga = _in_proj_common(zq, zkv, zga, qn_ref[...], kn1_ref[...], kn2_ref[...], gmat, tab,
                                       HEAD_DIM ** -0.5)
    zqm, zgm = _project_rest(h, w_ref)
    q_ref[...] = q
    rows_ref[...] = rows
    win_ref[...] = win
    gae_ref[...] = _dot_hilo(ga, eg_ref[...])
    glu = _glu(zglu)
    cw = cw_ref[...]
    c = cw[CONV_W - 1:CONV_W] * glu + cb_ref[...]
    for k in range(CONV_W - 1):
        c = c + cw[k:k + 1] * cc_ref[k]
    for k in range(CONV_W - 2):
        cst_ref[k] = cc_ref[k + 1]
    cst_ref[CONV_W - 2] = glu
    yb_ref[...] = _conv_tail(c, lng_ref[...], lnb_ref[...], woc_ref)
    qm_ref[...] = _mem_q(zqm, mqn_ref[...], gmat)
    gm_ref[...] = jax.nn.sigmoid(zgm)


def _heads_to_lanes(o8):
    lane = lax.broadcasted_iota(jnp.int32, (1, LANES), 1)
    blocks = []
    for cb in range(4):
        a = o8[2 * cb:2 * cb + 1, :]
        b = o8[2 * cb + 1:2 * cb + 2, :]
        if cb // 2 == 1:
            a = pltpu.roll(a, 64, 1)
        else:
            b = pltpu.roll(b, 64, 1)
        blocks.append(jnp.where(lane < 64, a, b))
    return jnp.concatenate(blocks, axis=1)


def _attn_sample_kernel(pt_ref, *refs):
    del pt_ref
    page_refs = refs[:TB * 16]
    (q_ref, rn_ref, wn_ref, gae_ref, qm_ref, cw_ref, cm_ref, wcat_ref, pet_ref, kn0_ref, gmat_ref, ctab_ref,
     ov_ref, ex_ref, pm_ref, on_ref, om_ref, wo_ref) = refs[TB * 16:]
    lane = lax.broadcasted_iota(jnp.int32, (1, LANES), 1)
    row8 = lax.broadcasted_iota(jnp.int32, (8, LANES), 0)
    lane8 = lax.broadcasted_iota(jnp.int32, (8, LANES), 1)

    pm = pm_ref[...]
    taps = [_dot_t(pm, page_refs[i][0, 0:256, :].astype(BF16)) for i in range(TB * 16)]

    def tap(l):
        return jnp.concatenate([taps[i][l * 8:(l + 1) * 8, :] for i in range(TB * 16)], axis=0)

    kc_all, vc_all = _compress(tap, TB * 128, wcat_ref, pet_ref[0:1, :], kn0_ref[...], gmat_ref[...],
                               tuple(jnp.concatenate([ctab_ref[i]] * TB, axis=0) for i in range(3)))

    toks = range(TB)
    rowm = lax.broadcasted_iota(jnp.int32, (8, 256), 0)
    headm = jnp.right_shift(lax.broadcasted_iota(jnp.int32, (8, 256), 1), 6)

    def bf(x):
        return x.astype(BF16).astype(F32)

    q8s, s_slc, s_win, s_mem = [], [], [], []
    for tok in toks:
        q = q_ref[tok]
        q_rows = []
        for r in range(N_HEADS_A):
            piece = q[:, (r // 2) * LANES:(r // 2 + 1) * LANES]
            if (r % 2) != (r // 4):
                piece = pltpu.roll(piece, 64, 1)
            q_rows.append(jnp.where(jnp.right_shift(lane, 6) == (r // 4), piece, 0.0))
        q8 = jnp.concatenate(q_rows, axis=0).astype(BF16)
        q8s.append(q8)
        s_slc.append(jnp.concatenate(
            [_dot(q8, page_refs[tok * 16 + i][0, 256:384, :].astype(BF16)) for i in range(16)], axis=1))
        s_win.append(_dot(q8, cw_ref[tok, 0:128, :].astype(BF16)))
        qm8 = jnp.where(rowm == headm, jnp.broadcast_to(qm_ref[tok], (8, 256)), 0.0).astype(BF16)
        s_mem.append(_dot(qm8, cm_ref[tok, 0:256, :].astype(BF16)))

    o_cmp, imp8 = [], []
    for tok in toks:
        s = jnp.where(lane8 < 127, _dot_t(q8s[tok], kc_all[tok * 128:(tok + 1) * 128]), NEG)
        p, den = _softmax_rows(s)
        p = p / den
        o_cmp.append(_dot(p.astype(BF16), vc_all[tok * 128:(tok + 1) * 128].astype(BF16)))
        imp8.append(_dot_hilo(p, ov_ref[...]))

    o_win = []
    for tok in toks:
        wn = wn_ref[tok]
        s = s_win[tok]
        s_new = jnp.sum(q8s[tok].astype(F32) * bf(wn[:, 0:128]), axis=-1, keepdims=True)
        m = jnp.maximum(jnp.max(s, axis=-1, keepdims=True), s_new)
        p = jnp.exp(s - m)
        p_new = jnp.exp(s_new - m)
        den = jnp.sum(p, axis=-1, keepdims=True) + p_new
        o_win.append((_dot_t(p.astype(BF16), cw_ref[tok, 128:256, :].astype(BF16))
                      + bf(p_new) * bf(wn[:, 128:256])) / den)
        p, den = _softmax_rows(s_mem[tok])
        o8 = _dot_t(p.astype(BF16), cm_ref[tok, 256:512, :].astype(BF16)) / den
        om_ref[tok] = jnp.sum(jnp.where(rowm == headm, o8, 0.0), axis=0, keepdims=True)
        cw = cw_ref[tok]
        rolled = pltpu.roll(cw, WINDOW - 1, 1)
        last = lax.broadcasted_iota(jnp.int32, (1, WINDOW), 1) == WINDOW - 1
        wn_col = jnp.broadcast_to(wn, (LANES, 256)).T[:, 0:1]
        wo_ref[tok] = jnp.where(last, wn_col, rolled)

    selx = []
    for tok in toks:
        sel_rows = []
        for g in range(N_KV_A):
            imp = jnp.sum(imp8[tok][4 * g:4 * g + 4, :], axis=0, keepdims=True)
            forced = (lane == 0) | (lane == 31) | (lane == 32)
            score = jnp.where(lane < 33, jnp.where(forced, BIG, imp), -3.0 * BIG)
            a = jnp.broadcast_to(score, (LANES, LANES))
            b = a.T
            sub = lax.broadcasted_iota(jnp.int32, (LANES, LANES), 0)
            ln = lax.broadcasted_iota(jnp.int32, (LANES, LANES), 1)
            beats = (b > a) | ((b == a) & (sub < ln))
            rank = jnp.sum(jnp.where(beats, 1.0, 0.0), axis=0, keepdims=True)
            sel_rows.append(jnp.where((rank < N_SEL) & (lane < 33), 1.0, 0.0))
        sel8 = jnp.where(row8 < 4, sel_rows[0], sel_rows[1]).astype(BF16)
        selx.append(_dot(sel8, ex_ref[...]))

    for tok in toks:
        rn = rn_ref[tok]
        gae = gae_ref[tok]
        s = jnp.where(selx[tok] > 0.5, s_slc[tok], NEG)
        s_new = jnp.sum(q8s[tok].astype(F32) * bf(rn[:, 256:384]), axis=-1, keepdims=True)
        m = jnp.maximum(jnp.max(s, axis=-1, keepdims=True), s_new)
        p = jnp.exp(s - m)
        p_new = jnp.exp(s_new - m)
        den = jnp.sum(p, axis=-1, keepdims=True) + p_new
        o_slc = bf(p_new) * bf(rn[:, 384:512])
        for i in range(16):
            o_slc = o_slc + _dot_t(p[:, i * PAGE:(i + 1) * PAGE].astype(BF16),
                                   page_refs[tok * 16 + i][0, 384:512, :].astype(BF16))
        o_slc = o_slc / den
        on_ref[tok] = (gae[:, 0:512] * _heads_to_lanes(o_cmp[tok]) + gae[:, 512:1024] * _heads_to_lanes(o_slc)
                       + gae[:, 1024:1536] * _heads_to_lanes(o_win[tok]))


def _post_sample_kernel(x_ref, on_ref, om_ref, gm_ref, yb_ref, wonsa_ref, wom_ref, wout_ref, nf_ref, wup_ref,
                        fcw_ref, fcb_ref, wdn_ref, cf_ref, y_ref, fst_ref):
    y_a = _dot(on_ref[...].astype(BF16), wonsa_ref[...])
    y_m = _dot(om_ref[...].astype(BF16), wom_ref[...])
    gm = gm_ref[...]
    merged = gm[:, 0:1024] * y_a + gm[:, 1024:2048] * yb_ref[...] + gm[:, 2048:3072] * y_m
    x1 = x_ref[...] + _dot(merged.astype(BF16), wout_ref[...])
    y, u = _ffn_tail(x1, nf_ref[...], wup_ref, lambda u: cf_ref[:, 0, :], lambda u: cf_ref[:, 1, :],
                     fcw_ref[...], fcb_ref[...], wdn_ref)
    y_ref[...] = y
    fst_ref[:, 0, :] = cf_ref[:, 1, :]
    fst_ref[:, 1, :] = u


def _rope_tables(pos):
    inv = ROPE_THETA ** (-jnp.arange(0, ROPE_DIM, 2, dtype=F32) / ROPE_DIM)
    ang = pos.astype(F32)[:, None] * inv
    cos, sin = jnp.cos(ang), jnp.sin(ang)
    n = pos.shape[0]
    one = jnp.ones((n, HEAD_DIM - ROPE_DIM), F32)
    z8 = jnp.zeros((n, 8), F32)
    z48 = jnp.zeros((n, HEAD_DIM - ROPE_DIM), F32)
    c = jnp.concatenate([cos, cos, one], axis=1)
    s1 = jnp.concatenate([z8, sin, z48], axis=1)
    s2 = jnp.concatenate([-sin, z8, z48], axis=1)
    return jnp.stack([jnp.tile(c, (1, 2)), jnp.tile(s1, (1, 2)), jnp.tile(s2, (1, 2))])


def _tile2(v):
    return jnp.tile(v.reshape(1, HEAD_DIM), (1, 2))


def _const(shape):
    nd = len(shape)
    return pl.BlockSpec(shape, lambda *_: (0,) * nd, pipeline_mode=pl.Buffered(1))


def _params(*sem):
    return pltpu.CompilerParams(dimension_semantics=sem, vmem_limit_bytes=VMEM_LIMIT)


def kernel(x_prompt, x_sample, cache_nsa, cache_win, cache_conv, cache_ffn, cache_mem, page_table, mem_prompt,
           norm_attn, w_in, q_norm, k_norm, cmp_pe, w_cmp, w_o_nsa, conv_w, conv_b, conv_ln_g, conv_ln_b, w_o_conv,
           norm_mem, w_mem_kv, mq_norm, mk_norm, w_o_mem, w_out, norm_ffn, w_ffn_up, ffn_conv_w, ffn_conv_b,
           w_ffn_down):
    B, T, _ = x_prompt.shape
    NS = x_sample.shape[0]
    n_pages = page_table.shape[1]
    assert w_in.shape[0] == 1 and T == 2048 and n_pages * PAGE == 2048 and cache_win.shape[2] == WINDOW
    nt = T // TM
    assert TQ == TM

    w_in0 = w_in[0]
    w_qkv = w_in0[:, :N_QKV].astype(BF16)
    w_gate = jnp.pad(w_in0[:, N_QKV:N_QKV + N_GATES], ((0, 0), (0, LANES - N_GATES))).astype(BF16)
    w_rest = w_in0[:, N_QKV + N_GATES:].astype(BF16)
    na = norm_attn.reshape(1, D_MODEL)
    nf = norm_ffn.reshape(1, D_MODEL)
    nm = norm_mem.reshape(1, D_MODEL)
    qn, mqn, mkn = _tile2(q_norm[0]), _tile2(mq_norm[0]), _tile2(mk_norm[0])
    kn0, kn1, kn2 = _tile2(k_norm[0, 0]), _tile2(k_norm[0, 1]), _tile2(k_norm[0, 2])
    ii = jnp.arange(LANES)
    gmat = jnp.where((ii[:, None] // 64) == (ii[None, :] // 64), 1.0 / 64, 0.0).astype(BF16)
    tab_p = _rope_tables(jnp.arange(T))
    tab_s = _rope_tables(jnp.full((1,), n_pages * PAGE))
    ctab = _rope_tables(jnp.arange(128) * CMP_STRIDE + (CMP_BLK - 1))
    wk, wv = w_cmp[0, 0].astype(BF16), w_cmp[0, 1].astype(BF16)
    z = jnp.zeros((CMP_BLK, HEAD_DIM, HEAD_DIM), BF16)
    w_l = jnp.concatenate([jnp.concatenate(r, axis=-1) for r in
                           ([wk, z, z, z], [z, wk, z, z], [z, z, wv, z], [z, z, z, wv])], axis=1)
    wcat = jnp.concatenate([w_l[:16], w_l[16:]], axis=-1)
    pe2 = jnp.broadcast_to(cmp_pe[0].transpose(1, 0, 2)[:, :, None, :], (CMP_BLK, 2, 2, HEAD_DIM)).reshape(
        CMP_BLK, 256)
    col = jnp.arange(3 * 512)
    egate = (jnp.arange(LANES)[:, None] == ((col // 512) * 8 + (col % 512) // 64)[None, :]).astype(BF16)
    egate_t = egate.T
    cs = jnp.arange(LANES)[:, None] * CMP_STRIDE
    ss = jnp.arange(LANES)[None, :] * SLC_BLK
    overlap = ((cs < ss + SLC_BLK) & (cs + CMP_BLK > ss) & (jnp.arange(LANES)[:, None] < 127)
               & (jnp.arange(LANES)[None, :] < 33)).astype(BF16)
    kpos = jnp.arange(T)
    expand = ((jnp.arange(LANES)[:, None] == (kpos // SLC_BLK)[None, :])).astype(BF16)
    blk_onehot = expand[0:64, :].T
    woc = w_o_conv[0].astype(BF16)
    wom = w_o_mem[0].astype(BF16)
    wonsa = w_o_nsa[0].astype(BF16)
    wout = w_out[0].astype(BF16)
    wup = w_ffn_up[0].astype(BF16)
    wdn = w_ffn_down[0].astype(BF16)
    wmkv = w_mem_kv[0].astype(BF16)
    cw, cb = conv_w[0], conv_b.reshape(1, C_CONV)
    lng, lnb = conv_ln_g.reshape(1, C_CONV), conv_ln_b.reshape(1, C_CONV)
    fcw, fcb = ffn_conv_w[0], ffn_conv_b.reshape(1, D_FF)

    mem_kv = pl.pallas_call(
        _memkv_kernel, grid=(B,),
        in_specs=[pl.BlockSpec((1, N_MEM, D_MODEL), lambda b: (b, 0, 0)), _const((1, D_MODEL)),
                  _const((D_MODEL, 512)), _const((1, LANES)), _const((LANES, LANES))],
        out_specs=pl.BlockSpec((1, N_MEM, 512), lambda b: (b, 0, 0)),
        out_shape=jax.ShapeDtypeStruct((B, N_MEM, 512), F32),
        compiler_params=_params("arbitrary"), name="mem_kv",
    )(mem_prompt, nm, wmkv, mkn, gmat)

    def tile(width):
        return pl.BlockSpec((1, TM, width), lambda b, t: (b, t, 0))

    n_win_t = WINDOW // TM
    pre_out_shapes = (
        jax.ShapeDtypeStruct((B, T, 512), BF16),
        jax.ShapeDtypeStruct((B, T, 512), F32),
        jax.ShapeDtypeStruct((B, T, 384), BF16),
        jax.ShapeDtypeStruct((B, nt, 256, TM), BF16),
        jax.ShapeDtypeStruct((B, WINDOW, 256), F32),
        jax.ShapeDtypeStruct((B, T, LANES), F32),
        jax.ShapeDtypeStruct((B, T, D_MODEL), BF16),
        jax.ShapeDtypeStruct((B, T, D_MODEL), BF16),
        jax.ShapeDtypeStruct((B, CONV_W - 1, C_CONV), F32),
    )
    q_p, rows_p, kk_p, vt_p, win_p, ga_p, gm0_p, part_p, cst_p = pl.pallas_call(
        _pre_prompt_kernel, grid=(B, nt),
        in_specs=[tile(D_MODEL), _const((1, D_MODEL)), _const((D_MODEL, N_QKV)), _const((D_MODEL, LANES)),
                  _const((D_MODEL, N_REST)), _const((1, LANES)),
                  _const((1, LANES)), _const((1, LANES)), _const((LANES, LANES)),
                  pl.BlockSpec((3, TM, LANES), lambda b, t: (0, t, 0)),
                  pl.BlockSpec((TM, 64), lambda b, t: (t, 0)),
                  _const((CONV_W, C_CONV)), _const((1, C_CONV)), _const((1, C_CONV)), _const((1, C_CONV)),
                  _const((C_CONV, D_MODEL)),
                  pl.BlockSpec((1, N_MEM, 512), lambda b, t: (b, 0, 0)), _const((1, LANES)),
                  _const((256, D_MODEL))],
        out_specs=(tile(512), tile(512), tile(384),
                   pl.BlockSpec((1, 1, 256, TM), lambda b, t: (b, t, 0, 0)),
                   pl.BlockSpec((1, TM, 256), lambda b, t: (b, jnp.maximum(t - (nt - n_win_t), 0), 0)),
                   tile(LANES), tile(D_MODEL), tile(D_MODEL),
                   pl.BlockSpec((1, CONV_W - 1, C_CONV), lambda b, t: (b, 0, 0))),
        out_shape=pre_out_shapes,
        scratch_shapes=[pltpu.VMEM((TM + 32, C_CONV), F32), pltpu.VMEM((7, TM + 24, C_CONV), F32)],
        compiler_params=_params("arbitrary", "arbitrary"), name="pre_prompt",
    )(x_prompt, na, w_qkv, w_gate, w_rest, qn, kn1, kn2, gmat, tab_p, blk_onehot, cw, cb, lng, lnb, woc, mem_kv,
      mqn, wom)

    pe_term = pl.pallas_call(
        _pe_term_kernel, out_shape=jax.ShapeDtypeStruct((8, 256), F32), name="pe_term",
    )(pe2, wcat)
    kc_p, vc_p = pl.pallas_call(
        _compress_prompt_kernel, grid=(B,),
        in_specs=[pl.BlockSpec((1, T, LANES), lambda b: (b, 0, 0)), pl.BlockSpec((1, T, LANES), lambda b: (b, 0, 1)),
                  _const((16, 256, 512)), _const((8, 256)),
                  _const((1, LANES)), _const((LANES, LANES)), _const((3, LANES, LANES))],
        out_specs=(pl.BlockSpec((1, LANES, LANES), lambda b: (b, 0, 0)),) * 2,
        out_shape=(jax.ShapeDtypeStruct((B, LANES, LANES), BF16),) * 2,
        compiler_params=_params("arbitrary"), name="compress_prompt",
    )(rows_p, rows_p, wcat, pe_term, kn0, gmat, ctab)

    xs = x_sample.reshape(NS, D_MODEL)
    pre_s_shapes = (
        jax.ShapeDtypeStruct((NS, 512), F32), jax.ShapeDtypeStruct((NS, 512), F32),
        jax.ShapeDtypeStruct((NS, 256), F32), jax.ShapeDtypeStruct((NS, 3 * 512), F32),
        jax.ShapeDtypeStruct((NS, 256), F32), jax.ShapeDtypeStruct((NS, 3 * D_MODEL), F32),
        jax.ShapeDtypeStruct((NS, D_MODEL), F32), jax.ShapeDtypeStruct((CONV_W - 1, NS, C_CONV), F32),
    )
    q_s, rows_s, win_s, gae_s, qm_s, gm_s, yb_s, cst_s = pl.pallas_call(
        _pre_sample_kernel, out_shape=pre_s_shapes,
        compiler_params=pltpu.CompilerParams(vmem_limit_bytes=VMEM_LIMIT), name="pre_sample",
    )(xs, na, w_qkv, w_gate, w_rest, qn, kn1, kn2, gmat, tab_s, cw, cb, lng, lnb, woc,
      cache_conv[0].transpose(1, 0, 2), mqn, egate)
    cst_s = cst_s.transpose(1, 0, 2)

    pages = cache_nsa[0].transpose(0, 2, 3, 4, 1).reshape(cache_nsa.shape[1], 512, PAGE)
    cw_t = cache_win[0].transpose(0, 2, 3, 4, 1).reshape(NS, 256, WINDOW)
    cm_t = cache_mem[0].transpose(0, 2, 3, 4, 1).reshape(NS, 512, N_MEM)
    pt_flat = page_table.reshape(-1)

    def page_spec(tok, p):
        return pl.BlockSpec((1, 512, PAGE), lambda i, pt: (pt[(i * TB + tok) * n_pages + p], 0, 0))

    rr = jnp.arange(PAGE)
    perm = (rr[None, :] == ((rr % 8) * CMP_STRIDE + rr // 8)[:, None]).astype(BF16)

    def tok_spec(width):
        return pl.BlockSpec((TB, 1, width), lambda i, pt: (i, 0, 0))

    def cst_spec(shape):
        nd = len(shape)
        return pl.BlockSpec(shape, lambda i, pt: (0,) * nd, pipeline_mode=pl.Buffered(1))

    grid_spec = pltpu.PrefetchScalarGridSpec(
        num_scalar_prefetch=1, grid=(NS // TB,),
        in_specs=[page_spec(tok, p) for tok in range(TB) for p in range(n_pages)] + [
            tok_spec(512), tok_spec(512), tok_spec(256), tok_spec(3 * 512), tok_spec(256),
            pl.BlockSpec((TB, 256, WINDOW), lambda i, pt: (i, 0, 0)),
            pl.BlockSpec((TB, 512, N_MEM), lambda i, pt: (i, 0, 0)),
            cst_spec((16, 256, 512)), cst_spec((8, 256)), cst_spec((1, LANES)), cst_spec((LANES, LANES)),
            cst_spec((3, LANES, LANES)), cst_spec((LANES, LANES)), cst_spec((LANES, T)),
            cst_spec((PAGE, PAGE))],
        out_specs=(tok_spec(512), tok_spec(256), pl.BlockSpec((TB, 256, WINDOW), lambda i, pt: (i, 0, 0))),
    )
    on_s, om_s, wo_s = pl.pallas_call(
        _attn_sample_kernel, grid_spec=grid_spec,
        out_shape=(jax.ShapeDtypeStruct((NS, 1, 512), F32), jax.ShapeDtypeStruct((NS, 1, 256), F32),
                   jax.ShapeDtypeStruct((NS, 256, WINDOW), F32)),
        compiler_params=_params("arbitrary"), name="attn_sample",
    )(pt_flat, *([pages] * (TB * n_pages)), q_s.reshape(NS, 1, 512), rows_s.reshape(NS, 1, 512),
      win_s.reshape(NS, 1, 256), gae_s.reshape(NS, 1, 3 * 512), qm_s.reshape(NS, 1, 256),
      cw_t, cm_t, wcat, pe_term, kn0, gmat, ctab, overlap, expand, perm)
    wo_s = wo_s.reshape(NS, 2, N_KV_A, HEAD_DIM, WINDOW).transpose(0, 4, 1, 2, 3)

    o_nsa_p = pl.pallas_call(
        _attn_prompt_kernel, grid=(B, T // TQ),
        in_specs=[pl.BlockSpec((1, TQ, 512), lambda b, t: (b, t, 0)),
                  pl.BlockSpec((1, T, 384), lambda b, t: (b, 0, 0)),
                  pl.BlockSpec((1, nt, 256, TM), lambda b, t: (b, 0, 0, 0)),
                  pl.BlockSpec((1, LANES, LANES), lambda b, t: (b, 0, 0)),
                  pl.BlockSpec((1, LANES, LANES), lambda b, t: (b, 0, 0)),
                  pl.BlockSpec((1, TQ, LANES), lambda b, t: (b, t, 0)),
                  _const((3 * 512, LANES)), _const((LANES, LANES))],
        out_specs=pl.BlockSpec((1, TQ, 512), lambda b, t: (b, t, 0)),
        out_shape=jax.ShapeDtypeStruct((B, T, 512), BF16),
        compiler_params=_params("arbitrary", "arbitrary"), name="attn_prompt",
    )(q_p, kk_p, vt_p, kc_p, vc_p, ga_p, egate_t, overlap.T)

    y_p, fst_p = pl.pallas_call(
        _post_prompt_kernel, grid=(B, nt),
        in_specs=[tile(D_MODEL), tile(512), tile(D_MODEL), tile(D_MODEL), _const((512, D_MODEL)),
                  _const((D_MODEL, D_MODEL)), _const((1, D_MODEL)), _const((D_MODEL, 2 * D_FF)),
                  _const((FFN_CONV_W, D_FF)), _const((1, D_FF)), _const((D_FF, D_MODEL))],
        out_specs=(tile(D_MODEL), pl.BlockSpec((1, FFN_CONV_W - 1, D_FF), lambda b, t: (b, 0, 0))),
        out_shape=(jax.ShapeDtypeStruct((B, T, D_MODEL), F32),
                   jax.ShapeDtypeStruct((B, FFN_CONV_W - 1, D_FF), F32)),
        scratch_shapes=[pltpu.VMEM((TM + 8, D_FF), F32)],
        compiler_params=_params("arbitrary", "arbitrary"), name="post_prompt",
    )(x_prompt, o_nsa_p, gm0_p, part_p, wonsa, wout, nf, wup, fcw, fcb, wdn)

    y_s, fst_s = pl.pallas_call(
        _post_sample_kernel,
        out_shape=(jax.ShapeDtypeStruct((NS, D_MODEL), F32),
                   jax.ShapeDtypeStruct((NS, FFN_CONV_W - 1, D_FF), F32)),
        compiler_params=pltpu.CompilerParams(vmem_limit_bytes=VMEM_LIMIT), name="post_sample",
    )(xs, on_s.reshape(NS, 512), om_s.reshape(NS, 256), gm_s, yb_s, wonsa, wom, wout, nf, wup, fcw, fcb, wdn,
      cache_ffn[0])

    return (y_p, y_s.reshape(NS, 1, D_MODEL),
            rows_p.reshape(1, B, T, 4, N_KV_A, HEAD_DIM), rows_s.reshape(1, NS, 1, 4, N_KV_A, HEAD_DIM),
            win_p.reshape(1, B, WINDOW, 2, N_KV_A, HEAD_DIM), wo_s[None],
            cst_p[None], cst_s[None], fst_p[None], fst_s[None],
            mem_kv.reshape(1, B, N_MEM, 2, N_HEADS_M, HEAD_DIM))
```

```python
import jax
import jax.numpy as jnp
from jax import lax
from jax.experimental import pallas as pl
from jax.experimental.pallas import tpu as pltpu

F32 = jnp.float32
BF16 = jnp.bfloat16

D_MODEL = 1024
HEAD_DIM = 64
N_HEADS_A = 8
N_KV_A = 2
GROUP_A = 4
CMP_BLK = 32
CMP_STRIDE = 16
SLC_BLK = 64
N_SEL = 16
WINDOW = 512
C_CONV = 512
CONV_W = 31
N_MEM = 256
N_HEADS_M = 4
D_FF = 2816
FFN_CONV_W = 3
ROPE_THETA = 500000.0
ROPE_DIM = 16
EPS = 1e-6
BIG = 1e9
NEG = -1e30
LOG2_E = 1.4426950408889634
PAGE = 128

N_QKV = 1280
N_GATES = 24
C_GLU = 0
C_QM = 1024
C_GM = 1280
N_REST = 4352

LANES = 128
VMEM_LIMIT = 56 * 1024 * 1024

TM = 256
TQ = 256
TB = 4
KB = 128
T_SLC = 32


def _dot(a, b):
    return jnp.dot(a, b, preferred_element_type=F32)


def _dot_t(a, b):
    return lax.dot_general(a, b, (((1,), (1,)), ((), ())), preferred_element_type=F32)


def _dot_hilo(x, m):
    hi = x.astype(BF16)
    lo = (x - hi.astype(F32)).astype(BF16)
    return _dot(hi, m) + _dot(lo, m)


def _rms(x, g):
    return x * lax.rsqrt(jnp.mean(x * x, axis=-1, keepdims=True) + EPS) * g


def _head_norm(blk, gain, gmat):
    ms = _dot_hilo(blk * blk, gmat)
    return blk * lax.rsqrt(ms + EPS) * gain


def _rope(blk, tab):
    c, s1, s2 = tab
    return blk * c + pltpu.roll(blk, 8, 1) * s1 + pltpu.roll(blk, LANES - 8, 1) * s2


def _softmax_rows(s):
    m = jnp.max(s, axis=-1, keepdims=True)
    p = jnp.exp(s - m)
    return p, jnp.sum(p, axis=-1, keepdims=True)


def _project_qkv(h, wa_ref, wg_ref, w_ref):
    return (_dot(h, wa_ref[:, 0:512]), _dot(h, wa_ref[:, 512:N_QKV]), _dot(h, wg_ref[...]),
            _dot(h, w_ref[:, C_GLU:C_GLU + 1024]))


def _project_rest(h, w_ref):
    return _dot(h, w_ref[:, C_QM:C_QM + 256]), _dot(h, w_ref[:, C_GM:C_GM + 3072])


def _in_proj_common(zq, zkv, zga, qn, kn1, kn2, gmat, tab, q_scale):
    q_blocks = []
    for cb in range(4):
        blk = zq[:, cb * LANES:(cb + 1) * LANES]
        q_blocks.append(_rope(_head_norm(blk, qn, gmat), tab) * q_scale)
    q = jnp.concatenate(q_blocks, axis=1)
    k_slc = _rope(_head_norm(zkv[:, 256:384], kn1, gmat), tab)
    k_win = _rope(_head_norm(zkv[:, 512:640], kn2, gmat), tab)
    rows = jnp.concatenate([zkv[:, 0:256], k_slc, zkv[:, 384:512]], axis=1)
    win = jnp.concatenate([k_win, zkv[:, 640:768]], axis=1)
    return q, rows, win, jax.nn.sigmoid(zga)


def _glu(z):
    return z[:, :C_CONV] * jax.nn.sigmoid(z[:, C_CONV:])


def _conv_tail(c, lng, lnb, wo_ref):
    mu = jnp.mean(c, axis=-1, keepdims=True)
    var = jnp.mean(jnp.square(c - mu), axis=-1, keepdims=True)
    y = (c - mu) * lax.rsqrt(var + EPS) * lng + lnb
    return _dot(jax.nn.silu(y).astype(BF16), wo_ref[...])


def _mem_q(z, mqn, gmat):
    return jnp.concatenate(
        [_head_norm(z[:, cb * LANES:(cb + 1) * LANES], mqn, gmat) for cb in range(2)], axis=1) * (HEAD_DIM ** -0.5)


def _ffn_tail(x1, nf, wup_ref, u_prev2, u_prev1_fn, fcw, fcb, wdn_ref):
    h2 = _rms(x1, nf).astype(BF16)
    up = _dot(h2, wup_ref[...])
    u = up[:, :D_FF]
    v = up[:, D_FF:]
    uc = fcw[0:1] * u_prev2(u) + fcw[1:2] * u_prev1_fn(u) + fcw[2:3] * u + fcb
    act = jax.nn.gelu(uc, approximate=True) * v
    return x1 + _dot(act.astype(BF16), wdn_ref[...]), u


def _memkv_kernel(mem_ref, nm_ref, w_ref, mkn_ref, gmat_ref, o_ref):
    h = _rms(mem_ref[0], nm_ref[...]).astype(BF16)
    z = _dot(h, w_ref[...])
    gmat = gmat_ref[...]
    k = [_head_norm(z[:, cb * LANES:(cb + 1) * LANES], mkn_ref[...], gmat) for cb in range(2)]
    o_ref[0] = jnp.concatenate(k + [z[:, 256:512]], axis=1)


def _pre_prompt_kernel(x_ref, na_ref, wa_ref, wg_ref, w_ref, qn_ref, kn1_ref, kn2_ref, gmat_ref, tab_ref, oh_ref,
                       cw_ref, cb_ref, lng_ref, lnb_ref, woc_ref, mkv_ref, mqn_ref, wom_ref,
                       q_ref, rows_ref, kk_ref, vt_ref, win_ref, ga_ref, gm0_ref, part_ref, cst_ref, hbuf, sbuf):
    @pl.when(pl.program_id(1) == 0)
    def _():
        hbuf[0:32, :] = jnp.zeros((32, C_CONV), F32)

    gmat = gmat_ref[...]
    tab = (tab_ref[0], tab_ref[1], tab_ref[2])
    h = _rms(x_ref[0], na_ref[...]).astype(BF16)
    zq, zkv, zga, zglu = _project_qkv(h, wa_ref, wg_ref, w_ref)
    zqm = _dot(h, w_ref[:, C_QM:C_QM + 256])

    glu = _glu(zglu)
    hbuf[32:32 + TM, :] = glu
    cw = cw_ref[...]
    c = jnp.zeros((TM, C_CONV), F32) + cb_ref[...]
    zgm_chunks = []
    for r in range(8):
        if r < 6:
            zgm_chunks.append(_dot(h, w_ref[:, C_GM + r * 512:C_GM + (r + 1) * 512]))
        if r > 0:
            sbuf[r - 1] = hbuf[pl.ds(r, TM + 24), :]
        for a in range(5):
            k = 8 * a + r - 2
            if 0 <= k < CONV_W:
                src = hbuf[8 * a:8 * a + TM, :] if r == 0 else sbuf[r - 1, 8 * a:8 * a + TM, :]
                c = c + cw[k:k + 1] * src
    zgm = jnp.concatenate(zgm_chunks, axis=1)
    cst_ref[0] = hbuf[pl.ds(TM + 2, CONV_W - 1), :]
    hbuf[0:32, :] = hbuf[TM:TM + 32, :]

    q, rows, win, ga = _in_proj_common(zq, zkv, zga, qn_ref[...], kn1_ref[...], kn2_ref[...], gmat, tab,
                                       HEAD_DIM ** -0.5 * LOG2_E)
    q_ref[0] = q.astype(BF16)
    rows_ref[0] = rows
    win_ref[0] = win
    ga_ref[0] = ga
    oh = oh_ref[...]
    kk_ref[0] = jnp.concatenate([rows[:, 256:320].astype(BF16), oh, rows[:, 320:384].astype(BF16), oh,
                                 win[:, 0:128].astype(BF16)], axis=1)
    vt_ref[0, 0] = jnp.concatenate([rows[:, 384:512], win[:, 128:256]], axis=1).T.astype(BF16)

    y_b = _conv_tail(c, lng_ref[...], lnb_ref[...], woc_ref)

    qm = _mem_q(zqm, mqn_ref[...], gmat).astype(BF16)
    mkv = mkv_ref[0].astype(BF16)
    scores = [_dot_t(qm[:, hh * 64:(hh + 1) * 64], mkv[:, hh * 64:(hh + 1) * 64]) for hh in range(N_HEADS_M)]
    probs = [_softmax_rows(s) for s in scores]
    heads = [_dot(p.astype(BF16), mkv[:, 256 + hh * 64:256 + (hh + 1) * 64]) / den
             for hh, (p, den) in enumerate(probs)]
    y_m = _dot(jnp.concatenate(heads, axis=1).astype(BF16), wom_ref[...])

    gm = jax.nn.sigmoid(zgm)
    gm0_ref[0] = gm[:, 0:1024].astype(BF16)
    part_ref[0] = (gm[:, 1024:2048] * y_b + gm[:, 2048:3072] * y_m).astype(BF16)


def _pe_term_kernel(pe_ref, wcat_ref, o_ref):
    acc = jnp.zeros((8, 256), F32)
    for l in range(16):
        top = jnp.broadcast_to(pe_ref[l:l + 1, :], (8, 256)).astype(BF16)
        bot = jnp.broadcast_to(pe_ref[l + 16:l + 17, :], (8, 256)).astype(BF16)
        acc = acc + _dot(top, wcat_ref[l][:, 0:256]) + _dot(bot, wcat_ref[l][:, 256:512])
    o_ref[...] = acc


def _compress(tap_fn, n_rows, wcat_ref, pe_term, kn0, gmat, ctab):
    acc = jnp.zeros((n_rows, 512), F32)
    for l in range(16):
        acc = acc + _dot(tap_fn(l).astype(BF16), wcat_ref[l])
    top = acc[:, 0:256]
    bot = pltpu.roll(acc[:, 256:512], n_rows - 1, 0)
    kcv = top + bot + pe_term
    kc = _rope(_head_norm(kcv[:, 0:128], kn0, gmat), ctab)
    return kc.astype(BF16), kcv[:, 128:256]


def _compress_prompt_kernel(kr_ref, vr_ref, wcat_ref, pet_ref, kn0_ref, gmat_ref, ctab_ref, kc_ref, vc_ref):
    def tap(l):
        return jnp.concatenate([r[0, pl.ds(l, 128, stride=16), :] for r in (kr_ref, vr_ref)], axis=1)

    kc, vc = _compress(tap, 128, wcat_ref, pet_ref[0:1, :],
                       kn0_ref[...], gmat_ref[...], (ctab_ref[0], ctab_ref[1], ctab_ref[2]))
    kc_ref[0] = kc
    vc_ref[0] = vc.T.astype(BF16)


def _select_cols(imp_t, tpos):
    n_slc = imp_t.shape[0]
    sidx = lax.broadcasted_iota(jnp.int32, imp_t.shape, 0)
    qblk = jnp.right_shift(tpos, 6)
    forced = (sidx == 0) | (sidx == qblk) | (sidx == qblk - 1)
    score = jnp.where(forced, BIG, jnp.where(sidx <= qblk, imp_t, -BIG))
    rank = jnp.zeros(imp_t.shape, F32)
    for s in range(n_slc):
        row = score[s:s + 1, :]
        tie = jnp.where(sidx > s, 1.0, 0.0)
        rank = rank + jnp.where(row > score, 1.0, jnp.where(row == score, tie, 0.0))
    return jnp.where((rank < N_SEL) & (score > -0.5 * BIG), 1.0, 0.0)


def _attn_prompt_kernel(q_ref, kk_ref, vt_ref, kc_ref, vct_ref, ga_ref, egt_ref, ovt_ref, o_ref):
    qt = pl.program_id(1)
    t0 = qt * TQ
    q = q_ref[0]
    tpos = t0 + lax.broadcasted_iota(jnp.int32, (1, TQ), 1)
    n_cmp = lax.broadcasted_iota(jnp.int32, (LANES, TQ), 0)
    cmask = jnp.where(((n_cmp * CMP_STRIDE + (CMP_BLK - 1)) <= tpos) & (n_cmp < 127), 1.0, 0.0)
    cbias = (cmask - 1.0) * (-NEG)
    cw0 = jnp.maximum(qt - 2, 0)
    w0 = pl.multiple_of(cw0 * TQ, TQ)
    wdiff = tpos - (w0 + lax.broadcasted_iota(jnp.int32, (3 * TQ, 1), 0))
    wbias = jnp.where((wdiff >= 0) & (wdiff <= WINDOW), 0.0, NEG)
    causal = jnp.where((t0 + lax.broadcasted_iota(jnp.int32, (TQ, 1), 0)) <= tpos, 0.0, NEG)
    ga_t = ga_ref[0].T
    ga_hi = ga_t.astype(BF16)
    ga_lo = (ga_t - ga_hi.astype(F32)).astype(BF16)
    gates_t = _dot(egt_ref[...], ga_hi) + _dot(egt_ref[...], ga_lo)

    def add4(s, b):
        return jnp.concatenate([s[:, j * TQ:(j + 1) * TQ] + b for j in range(GROUP_A)], axis=1)

    def with_ones(vt):
        return jnp.concatenate([vt, jnp.ones((16, vt.shape[1]), BF16)], axis=0)

    def blk_scores(k_rows, qh, add_bias):
        cols, maxs = [], []
        for j in range(GROUP_A):
            blocks = [add_bias(i, _dot_t(k_rows[i * KB:(i + 1) * KB, :], qh[j]))
                      for i in range(k_rows.shape[0] // KB)]
            mx = jnp.max(blocks[0], axis=0, keepdims=True)
            for b in blocks[1:]:
                mx = jnp.maximum(mx, jnp.max(b, axis=0, keepdims=True))
            cols.append(jnp.concatenate(blocks, axis=0))
            maxs.append(mx)
        return jnp.concatenate(cols, axis=1), jnp.concatenate(maxs, axis=1)

    def blk_pv(vt_blk, s, m):
        cols = []
        for j in range(GROUP_A):
            o = jnp.zeros((80, TQ), F32)
            for i in range(s.shape[0] // KB):
                p = jnp.exp2(s[i * KB:(i + 1) * KB, j * TQ:(j + 1) * TQ] - m[:, j * TQ:(j + 1) * TQ])
                o = o + _dot(vt_blk(i), p.astype(BF16))
            cols.append(o)
        return jnp.concatenate(cols, axis=1)

    groups = range(N_KV_A)
    qhs = [[q[:, (4 * g + j) * 64:(4 * g + j + 1) * 64] for j in range(GROUP_A)] for g in groups]
    s_cmps = [add4(_dot_t(kc_ref[0][:, g * 64:(g + 1) * 64], jnp.concatenate(qhs[g], axis=0)), cbias)
              for g in groups]
    wins = [[[_dot_t(kk_ref[0, pl.ds(w0 + i * KB, KB), 256 + g * 64:256 + (g + 1) * 64], qhs[g][j])
              + wbias[i * KB:(i + 1) * KB, :] for i in range(3 * TQ // KB)] for j in range(GROUP_A)]
            for g in groups]

    o_cmps, o_wins, qps = [], [], []
    for g in groups:
        lo, hi = g * 64, (g + 1) * 64
        s_cmp = s_cmps[g]
        p = jnp.exp2(s_cmp - jnp.max(s_cmp, axis=0, keepdims=True))
        p = jnp.concatenate([p[:, j * TQ:(j + 1) * TQ] * cmask for j in range(GROUP_A)], axis=1)
        den = jnp.sum(p, axis=0, keepdims=True)
        p = p / jnp.where(den > 0.0, den, 1.0)
        o_cmps.append(_dot(vct_ref[0][lo:hi, :], p.astype(BF16)))
        p4 = p[:, 0:TQ] + p[:, TQ:2 * TQ] + p[:, 2 * TQ:3 * TQ] + p[:, 3 * TQ:4 * TQ]
        p4_hi = p4.astype(BF16)
        p4_lo = (p4 - p4_hi.astype(F32)).astype(BF16)
        imp_t = _dot(ovt_ref[...], p4_hi) + _dot(ovt_ref[...], p4_lo)
        sel_bias = (_select_cols(imp_t[0:32, :], tpos) - 1.0) * (-NEG)
        bias_t = jnp.concatenate([sel_bias, jnp.zeros((LANES - T_SLC, TQ), F32)], axis=0).T
        bias_t = bias_t[:, 0:64].astype(BF16)
        qps.append([jnp.concatenate([qhs[g][j], bias_t], axis=1) for j in range(GROUP_A)])

    for g in groups:
        cols = []
        for j in range(GROUP_A):
            m_i = jnp.full((1, TQ), NEG, F32)
            acc = jnp.zeros((80, TQ), F32)
            for i in range(3 * TQ // KB):
                s = wins[g][j][i]
                m_new = jnp.maximum(m_i, jnp.max(s, axis=0, keepdims=True))
                p = jnp.exp2(s - m_new).astype(BF16)
                vt = with_ones(vt_ref[0, cw0 + i // 2, 128 + g * 64:128 + (g + 1) * 64,
                                      (i % 2) * KB:(i % 2 + 1) * KB])
                acc = jnp.exp2(m_i - m_new) * acc + _dot(vt, p)
                m_i = m_new
            cols.append(acc[0:64, :] / acc[64:65, :])
        o_wins.append(jnp.concatenate(cols, axis=1))

    def step(c, carry, diagonal):
        k0 = pl.multiple_of(c * TQ, TQ)
        scored = []
        for g in groups:
            def add_bias(i, blk):
                return blk + causal[i * KB:(i + 1) * KB, :] if diagonal else blk

            scored.append(blk_scores(kk_ref[0, pl.ds(k0, TQ), g * LANES:(g + 1) * LANES], qps[g], add_bias))
        new = []
        for g in groups:
            s, smax = scored[g]
            m_i, acc = carry[g]
            m_new = jnp.maximum(m_i, smax)
            pv = blk_pv(lambda i: with_ones(vt_ref[0, c, g * 64:(g + 1) * 64, i * KB:(i + 1) * KB]), s, m_new)
            new.append((m_new, jnp.exp2(m_i - m_new) * acc + pv))
        return tuple(new)

    units = [(i, g, j) for i in range(TQ // KB) for g in groups for j in range(GROUP_A)]

    def loop_step(c, carry):
        k0 = pl.multiple_of(c * TQ, TQ)
        ms = [[carry[g][0][:, j * TQ:(j + 1) * TQ] for j in range(GROUP_A)] for g in groups]
        accs = [[carry[g][1][:, j * TQ:(j + 1) * TQ] for j in range(GROUP_A)] for g in groups]
        scores = [_dot_t(kk_ref[0, pl.ds(k0 + i * KB, KB), g * LANES:(g + 1) * LANES], qps[g][j])
                  for i, g, j in units]
        for s, (i, g, j) in zip(scores, units):
            m_new = jnp.maximum(ms[g][j], jnp.max(s, axis=0, keepdims=True))
            p = jnp.exp2(s - m_new).astype(BF16)
            vt = with_ones(vt_ref[0, c, g * 64:(g + 1) * 64, i * KB:(i + 1) * KB])
            accs[g][j] = jnp.exp2(ms[g][j] - m_new) * accs[g][j] + _dot(vt, p)
            ms[g][j] = m_new
        return tuple((jnp.concatenate(ms[g], axis=1), jnp.concatenate(accs[g], axis=1)) for g in groups)

    init = tuple((jnp.full((1, GROUP_A * TQ), NEG, F32), jnp.zeros((80, GROUP_A * TQ), F32)) for _ in groups)
    final = step(qt, lax.fori_loop(0, qt, loop_step, init), True)

    outs = [[], [], []]
    for g in groups:
        acc = final[g][1]
        o_slc = acc[0:64, :] / acc[64:65, :]
        for c, o in enumerate((o_cmps[g], o_slc, o_wins[g])):
            outs[c] += [o[:, j * TQ:(j + 1) * TQ] for j in range(GROUP_A)]

    o_t = jnp.zeros((512, TQ), F32)
    for c in range(3):
        o_t = o_t + gates_t[c * 512:(c + 1) * 512, :] * jnp.concatenate(outs[c], axis=0)
    o_ref[0] = o_t.T.astype(BF16)


def _post_prompt_kernel(x_ref, on_ref, gm0_ref, part_ref, wonsa_ref, wout_ref, nf_ref, wup_ref, fcw_ref,
                        fcb_ref, wdn_ref, y_ref, fst_ref, ubuf):
    @pl.when(pl.program_id(1) == 0)
    def _():
        ubuf[0:8, :] = jnp.zeros((8, D_FF), F32)

    y_a = _dot(on_ref[0], wonsa_ref[...])
    merged = gm0_ref[0].astype(F32) * y_a + part_ref[0].astype(F32)
    x1 = x_ref[0] + _dot(merged.astype(BF16), wout_ref[...])

    def prev2(u):
        ubuf[8:8 + TM, :] = u
        return ubuf[pl.ds(6, TM), :]

    def prev1(u):
        return ubuf[pl.ds(7, TM), :]

    y, u = _ffn_tail(x1, nf_ref[...], wup_ref, prev2, prev1, fcw_ref[...], fcb_ref[...], wdn_ref)
    y_ref[0] = y
    fst_ref[0] = u[TM - 2:TM, :]
    ubuf[0:8, :] = ubuf[TM:TM + 8, :]


def _pre_sample_kernel(x_ref, na_ref, wa_ref, wg_ref, w_ref, qn_ref, kn1_ref, kn2_ref, gmat_ref, tab_ref,
                       cw_ref, cb_ref, lng_ref, lnb_ref, woc_ref, cc_ref, mqn_ref, eg_ref,
                       q_ref, rows_ref, win_ref, gae_ref, qm_ref, gm_ref, yb_ref, cst_ref):
    gmat = gmat_ref[...]
    tab = (tab_ref[0], tab_ref[1], tab_ref[2])
    h = _rms(x_ref[...], na_ref[...]).astype(BF16)
    zq, zkv, zga, zglu = _project_qkv(h, wa_ref, wg_ref, w_ref)
    q, rows, win, ga = _in_proj_common(zq, zkv, zga, qn_ref[...], kn1_ref[...], kn2_ref[...], gmat, tab,
                                       HEAD_DIM ** -0.5)
    zqm, zgm = _project_rest(h, w_ref)
    q_ref[...] = q
    rows_ref[...] = rows
    win_ref[...] = win
    gae_ref[...] = _dot_hilo(ga, eg_ref[...])
    glu = _glu(zglu)
    cw = cw_ref[...]
    c = cw[CONV_W - 1:CONV_W] * glu + cb_ref[...]
    for k in range(CONV_W - 1):
        c = c + cw[k:k + 1] * cc_ref[k]
    for k in range(CONV_W - 2):
        cst_ref[k] = cc_ref[k + 1]
    cst_ref[CONV_W - 2] = glu
    yb_ref[...] = _conv_tail(c, lng_ref[...], lnb_ref[...], woc_ref)
    qm_ref[...] = _mem_q(zqm, mqn_ref[...], gmat)
    gm_ref[...] = jax.nn.sigmoid(zgm)


def _heads_to_lanes(o8):
    lane = lax.broadcasted_iota(jnp.int32, (1, LANES), 1)
    blocks = []
    for cb in range(4):
        a = o8[2 * cb:2 * cb + 1, :]
        b = o8[2 * cb + 1:2 * cb + 2, :]
        if cb // 2 == 1:
            a = pltpu.roll(a, 64, 1)
        else:
            b = pltpu.roll(b, 64, 1)
        blocks.append(jnp.where(lane < 64, a, b))
    return jnp.concatenate(blocks, axis=1)


def _attn_sample_kernel(pt_ref, *refs):
    del pt_ref
    page_refs = refs[:TB * 16]
    (q_ref, rn_ref, wn_ref, gae_ref, qm_ref, cw_ref, cm_ref, wcat_ref, pet_ref, kn0_ref, gmat_ref, ctab_ref,
     ov_ref, ex_ref, pm_ref, on_ref, om_ref, wo_ref) = refs[TB * 16:]
    lane = lax.broadcasted_iota(jnp.int32, (1, LANES), 1)
    row8 = lax.broadcasted_iota(jnp.int32, (8, LANES), 0)
    lane8 = lax.broadcasted_iota(jnp.int32, (8, LANES), 1)

    pm = pm_ref[...]
    taps = [_dot_t(pm, page_refs[i][0, 0:256, :].astype(BF16)) for i in range(TB * 16)]

    def tap(l):
        return jnp.concatenate([taps[i][l * 8:(l + 1) * 8, :] for i in range(TB * 16)], axis=0)

    kc_all, vc_all = _compress(tap, TB * 128, wcat_ref, pet_ref[0:1, :], kn0_ref[...], gmat_ref[...],
                               tuple(jnp.concatenate([ctab_ref[i]] * TB, axis=0) for i in range(3)))

    toks = range(TB)
    rowm = lax.broadcasted_iota(jnp.int32, (8, 256), 0)
    headm = jnp.right_shift(lax.broadcasted_iota(jnp.int32, (8, 256), 1), 6)

    def bf(x):
        return x.astype(BF16).astype(F32)

    q8s, s_slc, s_win, s_mem = [], [], [], []
    for tok in toks:
        q = q_ref[tok]
        q_rows = []
        for r in range(N_HEADS_A):
            piece = q[:, (r // 2) * LANES:(r // 2 + 1) * LANES]
            if (r % 2) != (r // 4):
                piece = pltpu.roll(piece, 64, 1)
            q_rows.append(jnp.where(jnp.right_shift(lane, 6) == (r // 4), piece, 0.0))
        q8 = jnp.concatenate(q_rows, axis=0).astype(BF16)
        q8s.append(q8)
        s_slc.append(jnp.concatenate(
            [_dot(q8, page_refs[tok * 16 + i][0, 256:384, :].astype(BF16)) for i in range(16)], axis=1))
        s_win.append(_dot(q8, cw_ref[tok, 0:128, :].astype(BF16)))
        qm8 = jnp.where(rowm == headm, jnp.broadcast_to(qm_ref[tok], (8, 256)), 0.0).astype(BF16)
        s_mem.append(_dot(qm8, cm_ref[tok, 0:256, :].astype(BF16)))

    o_cmp, imp8 = [], []
    for tok in toks:
        s = jnp.where(lane8 < 127, _dot_t(q8s[tok], kc_all[tok * 128:(tok + 1) * 128]), NEG)
        p, den = _softmax_rows(s)
        p = p / den
        o_cmp.append(_dot(p.astype(BF16), vc_all[tok * 128:(tok + 1) * 128].astype(BF16)))
        imp8.append(_dot_hilo(p, ov_ref[...]))

    o_win = []
    for tok in toks:
        wn = wn_ref[tok]
        s = s_win[tok]
        s_new = jnp.sum(q8s[tok].astype(F32) * bf(wn[:, 0:128]), axis=-1, keepdims=True)
        m = jnp.maximum(jnp.max(s, axis=-1, keepdims=True), s_new)
        p = jnp.exp(s - m)
        p_new = jnp.exp(s_new - m)
        den = jnp.sum(p, axis=-1, keepdims=True) + p_new
        o_win.append((_dot_t(p.astype(BF16), cw_ref[tok, 128:256, :].astype(BF16))
                      + bf(p_new) * bf(wn[:, 128:256])) / den)
        p, den = _softmax_rows(s_mem[tok])
        o8 = _dot_t(p.astype(BF16), cm_ref[tok, 256:512, :].astype(BF16)) / den
        om_ref[tok] = jnp.sum(jnp.where(rowm == headm, o8, 0.0), axis=0, keepdims=True)
        cw = cw_ref[tok]
        rolled = pltpu.roll(cw, WINDOW - 1, 1)
        last = lax.broadcasted_iota(jnp.int32, (1, WINDOW), 1) == WINDOW - 1
        wn_col = jnp.broadcast_to(wn, (LANES, 256)).T[:, 0:1]
        wo_ref[tok] = jnp.where(last, wn_col, rolled)

    selx = []
    for tok in toks:
        sel_rows = []
        for g in range(N_KV_A):
            imp = jnp.sum(imp8[tok][4 * g:4 * g + 4, :], axis=0, keepdims=True)
            forced = (lane == 0) | (lane == 31) | (lane == 32)
            score = jnp.where(lane < 33, jnp.where(forced, BIG, imp), -3.0 * BIG)
            a = jnp.broadcast_to(score, (LANES, LANES))
            b = a.T
            sub = lax.broadcasted_iota(jnp.int32, (LANES, LANES), 0)
            ln = lax.broadcasted_iota(jnp.int32, (LANES, LANES), 1)
            beats = (b > a) | ((b == a) & (sub < ln))
            rank = jnp.sum(jnp.where(beats, 1.0, 0.0), axis=0, keepdims=True)
            sel_rows.append(jnp.where((rank < N_SEL) & (lane < 33), 1.0, 0.0))
        sel8 = jnp.where(row8 < 4, sel_rows[0], sel_rows[1]).astype(BF16)
        selx.append(_dot(sel8, ex_ref[...]))

    for tok in toks:
        rn = rn_ref[tok]
        gae = gae_ref[tok]
        s = jnp.where(selx[tok] > 0.5, s_slc[tok], NEG)
        s_new = jnp.sum(q8s[tok].astype(F32) * bf(rn[:, 256:384]), axis=-1, keepdims=True)
        m = jnp.maximum(jnp.max(s, axis=-1, keepdims=True), s_new)
        p = jnp.exp(s - m)
        p_new = jnp.exp(s_new - m)
        den = jnp.sum(p, axis=-1, keepdims=True) + p_new
        o_slc = bf(p_new) * bf(rn[:, 384:512])
        for i in range(16):
            o_slc = o_slc + _dot_t(p[:, i * PAGE:(i + 1) * PAGE].astype(BF16),
                                   page_refs[tok * 16 + i][0, 384:512, :].astype(BF16))
        o_slc = o_slc / den
        on_ref[tok] = (gae[:, 0:512] * _heads_to_lanes(o_cmp[tok]) + gae[:, 512:1024] * _heads_to_lanes(o_slc)
                       + gae[:, 1024:1536] * _heads_to_lanes(o_win[tok]))


def _post_sample_kernel(x_ref, on_ref, om_ref, gm_ref, yb_ref, wonsa_ref, wom_ref, wout_ref, nf_ref, wup_ref,
                        fcw_ref, fcb_ref, wdn_ref, cf_ref, y_ref, fst_ref):
    y_a = _dot(on_ref[...].astype(BF16), wonsa_ref[...])
    y_m = _dot(om_ref[...].astype(BF16), wom_ref[...])
    gm = gm_ref[...]
    merged = gm[:, 0:1024] * y_a + gm[:, 1024:2048] * yb_ref[...] + gm[:, 2048:3072] * y_m
    x1 = x_ref[...] + _dot(merged.astype(BF16), wout_ref[...])
    y, u = _ffn_tail(x1, nf_ref[...], wup_ref, lambda u: cf_ref[:, 0, :], lambda u: cf_ref[:, 1, :],
                     fcw_ref[...], fcb_ref[...], wdn_ref)
    y_ref[...] = y
    fst_ref[:, 0, :] = cf_ref[:, 1, :]
    fst_ref[:, 1, :] = u


def _rope_tables(pos):
    inv = ROPE_THETA ** (-jnp.arange(0, ROPE_DIM, 2, dtype=F32) / ROPE_DIM)
    ang = pos.astype(F32)[:, None] * inv
    cos, sin = jnp.cos(ang), jnp.sin(ang)
    n = pos.shape[0]
    one = jnp.ones((n, HEAD_DIM - ROPE_DIM), F32)
    z8 = jnp.zeros((n, 8), F32)
    z48 = jnp.zeros((n, HEAD_DIM - ROPE_DIM), F32)
    c = jnp.concatenate([cos, cos, one], axis=1)
    s1 = jnp.concatenate([z8, sin, z48], axis=1)
    s2 = jnp.concatenate([-sin, z8, z48], axis=1)
    return jnp.stack([jnp.tile(c, (1, 2)), jnp.tile(s1, (1, 2)), jnp.tile(s2, (1, 2))])


def _tile2(v):
    return jnp.tile(v.reshape(1, HEAD_DIM), (1, 2))


def _const(shape):
    nd = len(shape)
    return pl.BlockSpec(shape, lambda *_: (0,) * nd, pipeline_mode=pl.Buffered(1))


def _params(*sem):
    return pltpu.CompilerParams(dimension_semantics=sem, vmem_limit_bytes=VMEM_LIMIT)


def kernel(x_prompt, x_sample, cache_nsa, cache_win, cache_conv, cache_ffn, cache_mem, page_table, mem_prompt,
           norm_attn, w_in, q_norm, k_norm, cmp_pe, w_cmp, w_o_nsa, conv_w, conv_b, conv_ln_g, conv_ln_b, w_o_conv,
           norm_mem, w_mem_kv, mq_norm, mk_norm, w_o_mem, w_out, norm_ffn, w_ffn_up, ffn_conv_w, ffn_conv_b,
           w_ffn_down):
    B, T, _ = x_prompt.shape
    NS = x_sample.shape[0]
    n_pages = page_table.shape[1]
    assert w_in.shape[0] == 1 and T == 2048 and n_pages * PAGE == 2048 and cache_win.shape[2] == WINDOW
    nt = T // TM
    assert TQ == TM

    w_in0 = w_in[0]
    w_qkv = w_in0[:, :N_QKV].astype(BF16)
    w_gate = jnp.pad(w_in0[:, N_QKV:N_QKV + N_GATES], ((0, 0), (0, LANES - N_GATES))).astype(BF16)
    w_rest = w_in0[:, N_QKV + N_GATES:].astype(BF16)
    na = norm_attn.reshape(1, D_MODEL)
    nf = norm_ffn.reshape(1, D_MODEL)
    nm = norm_mem.reshape(1, D_MODEL)
    qn, mqn, mkn = _tile2(q_norm[0]), _tile2(mq_norm[0]), _tile2(mk_norm[0])
    kn0, kn1, kn2 = _tile2(k_norm[0, 0]), _tile2(k_norm[0, 1]), _tile2(k_norm[0, 2])
    ii = jnp.arange(LANES)
    gmat = jnp.where((ii[:, None] // 64) == (ii[None, :] // 64), 1.0 / 64, 0.0).astype(BF16)
    tab_p = _rope_tables(jnp.arange(T))
    tab_s = _rope_tables(jnp.full((1,), n_pages * PAGE))
    ctab = _rope_tables(jnp.arange(128) * CMP_STRIDE + (CMP_BLK - 1))
    wk, wv = w_cmp[0, 0].astype(BF16), w_cmp[0, 1].astype(BF16)
    z = jnp.zeros((CMP_BLK, HEAD_DIM, HEAD_DIM), BF16)
    w_l = jnp.concatenate([jnp.concatenate(r, axis=-1) for r in
                           ([wk, z, z, z], [z, wk, z, z], [z, z, wv, z], [z, z, z, wv])], axis=1)
    wcat = jnp.concatenate([w_l[:16], w_l[16:]], axis=-1)
    pe2 = jnp.broadcast_to(cmp_pe[0].transpose(1, 0, 2)[:, :, None, :], (CMP_BLK, 2, 2, HEAD_DIM)).reshape(
        CMP_BLK, 256)
    col = jnp.arange(3 * 512)
    egate = (jnp.arange(LANES)[:, None] == ((col // 512) * 8 + (col % 512) // 64)[None, :]).astype(BF16)
    egate_t = egate.T
    cs = jnp.arange(LANES)[:, None] * CMP_STRIDE
    ss = jnp.arange(LANES)[None, :] * SLC_BLK
    overlap = ((cs < ss + SLC_BLK) & (cs + CMP_BLK > ss) & (jnp.arange(LANES)[:, None] < 127)
               & (jnp.arange(LANES)[None, :] < 33)).astype(BF16)
    kpos = jnp.arange(T)
    expand = ((jnp.arange(LANES)[:, None] == (kpos // SLC_BLK)[None, :])).astype(BF16)
    blk_onehot = expand[0:64, :].T
    woc = w_o_conv[0].astype(BF16)
    wom = w_o_mem[0].astype(BF16)
    wonsa = w_o_nsa[0].astype(BF16)
    wout = w_out[0].astype(BF16)
    wup = w_ffn_up[0].astype(BF16)
    wdn = w_ffn_down[0].astype(BF16)
    wmkv = w_mem_kv[0].astype(BF16)
    cw, cb = conv_w[0], conv_b.reshape(1, C_CONV)
    lng, lnb = conv_ln_g.reshape(1, C_CONV), conv_ln_b.reshape(1, C_CONV)
    fcw, fcb = ffn_conv_w[0], ffn_conv_b.reshape(1, D_FF)

    mem_kv = pl.pallas_call(
        _memkv_kernel, grid=(B,),
        in_specs=[pl.BlockSpec((1, N_MEM, D_MODEL), lambda b: (b, 0, 0)), _const((1, D_MODEL)),
                  _const((D_MODEL, 512)), _const((1, LANES)), _const((LANES, LANES))],
        out_specs=pl.BlockSpec((1, N_MEM, 512), lambda b: (b, 0, 0)),
        out_shape=jax.ShapeDtypeStruct((B, N_MEM, 512), F32),
        compiler_params=_params("arbitrary"), name="mem_kv",
    )(mem_prompt, nm, wmkv, mkn, gmat)

    def tile(width):
        return pl.BlockSpec((1, TM, width), lambda b, t: (b, t, 0))

    n_win_t = WINDOW // TM
    pre_out_shapes = (
        jax.ShapeDtypeStruct((B, T, 512), BF16),
        jax.ShapeDtypeStruct((B, T, 512), F32),
        jax.ShapeDtypeStruct((B, T, 384), BF16),
        jax.ShapeDtypeStruct((B, nt, 256, TM), BF16),
        jax.ShapeDtypeStruct((B, WINDOW, 256), F32),
        jax.ShapeDtypeStruct((B, T, LANES), F32),
        jax.ShapeDtypeStruct((B, T, D_MODEL), BF16),
        jax.ShapeDtypeStruct((B, T, D_MODEL), BF16),
        jax.ShapeDtypeStruct((B, CONV_W - 1, C_CONV), F32),
    )
    q_p, rows_p, kk_p, vt_p, win_p, ga_p, gm0_p, part_p, cst_p = pl.pallas_call(
        _pre_prompt_kernel, grid=(B, nt),
        in_specs=[tile(D_MODEL), _const((1, D_MODEL)), _const((D_MODEL, N_QKV)), _const((D_MODEL, LANES)),
                  _const((D_MODEL, N_REST)), _const((1, LANES)),
                  _const((1, LANES)), _const((1, LANES)), _const((LANES, LANES)),
                  pl.BlockSpec((3, TM, LANES), lambda b, t: (0, t, 0)),
                  pl.BlockSpec((TM, 64), lambda b, t: (t, 0)),
                  _const((CONV_W, C_CONV)), _const((1, C_CONV)), _const((1, C_CONV)), _const((1, C_CONV)),
                  _const((C_CONV, D_MODEL)),
                  pl.BlockSpec((1, N_MEM, 512), lambda b, t: (b, 0, 0)), _const((1, LANES)),
                  _const((256, D_MODEL))],
        out_specs=(tile(512), tile(512), tile(384),
                   pl.BlockSpec((1, 1, 256, TM), lambda b, t: (b, t, 0, 0)),
                   pl.BlockSpec((1, TM, 256), lambda b, t: (b, jnp.maximum(t - (nt - n_win_t), 0), 0)),
                   tile(LANES), tile(D_MODEL), tile(D_MODEL),
                   pl.BlockSpec((1, CONV_W - 1, C_CONV), lambda b, t: (b, 0, 0))),
        out_shape=pre_out_shapes,
        scratch_shapes=[pltpu.VMEM((TM + 32, C_CONV), F32), pltpu.VMEM((7, TM + 24, C_CONV), F32)],
        compiler_params=_params("arbitrary", "arbitrary"), name="pre_prompt",
    )(x_prompt, na, w_qkv, w_gate, w_rest, qn, kn1, kn2, gmat, tab_p, blk_onehot, cw, cb, lng, lnb, woc, mem_kv,
      mqn, wom)

    pe_term = pl.pallas_call(
        _pe_term_kernel, out_shape=jax.ShapeDtypeStruct((8, 256), F32), name="pe_term",
    )(pe2, wcat)
    kc_p, vc_p = pl.pallas_call(
        _compress_prompt_kernel, grid=(B,),
        in_specs=[pl.BlockSpec((1, T, LANES), lambda b: (b, 0, 0)), pl.BlockSpec((1, T, LANES), lambda b: (b, 0, 1)),
                  _const((16, 256, 512)), _const((8, 256)),
                  _const((1, LANES)), _const((LANES, LANES)), _const((3, LANES, LANES))],
        out_specs=(pl.BlockSpec((1, LANES, LANES), lambda b: (b, 0, 0)),) * 2,
        out_shape=(jax.ShapeDtypeStruct((B, LANES, LANES), BF16),) * 2,
        compiler_params=_params("arbitrary"), name="compress_prompt",
    )(rows_p, rows_p, wcat, pe_term, kn0, gmat, ctab)

    xs = x_sample.reshape(NS, D_MODEL)
    pre_s_shapes = (
        jax.ShapeDtypeStruct((NS, 512), F32), jax.ShapeDtypeStruct((NS, 512), F32),
        jax.ShapeDtypeStruct((NS, 256), F32), jax.ShapeDtypeStruct((NS, 3 * 512), F32),
        jax.ShapeDtypeStruct((NS, 256), F32), jax.ShapeDtypeStruct((NS, 3 * D_MODEL), F32),
        jax.ShapeDtypeStruct((NS, D_MODEL), F32), jax.ShapeDtypeStruct((CONV_W - 1, NS, C_CONV), F32),
    )
    q_s, rows_s, win_s, gae_s, qm_s, gm_s, yb_s, cst_s = pl.pallas_call(
        _pre_sample_kernel, out_shape=pre_s_shapes,
        compiler_params=pltpu.CompilerParams(vmem_limit_bytes=VMEM_LIMIT), name="pre_sample",
    )(xs, na, w_qkv, w_gate, w_rest, qn, kn1, kn2, gmat, tab_s, cw, cb, lng, lnb, woc,
      cache_conv[0].transpose(1, 0, 2), mqn, egate)
    cst_s = cst_s.transpose(1, 0, 2)

    pages = cache_nsa[0].transpose(0, 2, 3, 4, 1).reshape(cache_nsa.shape[1], 512, PAGE)
    cw_t = cache_win[0].transpose(0, 2, 3, 4, 1).reshape(NS, 256, WINDOW)
    cm_t = cache_mem[0].transpose(0, 2, 3, 4, 1).reshape(NS, 512, N_MEM)
    pt_flat = page_table.reshape(-1)

    def page_spec(tok, p):
        return pl.BlockSpec((1, 512, PAGE), lambda i, pt: (pt[(i * TB + tok) * n_pages + p], 0, 0))

    rr = jnp.arange(PAGE)
    perm = (rr[None, :] == ((rr % 8) * CMP_STRIDE + rr // 8)[:, None]).astype(BF16)

    def tok_spec(width):
        return pl.BlockSpec((TB, 1, width), lambda i, pt: (i, 0, 0))

    def cst_spec(shape):
        nd = len(shape)
        return pl.BlockSpec(shape, lambda i, pt: (0,) * nd, pipeline_mode=pl.Buffered(1))

    grid_spec = pltpu.PrefetchScalarGridSpec(
        num_scalar_prefetch=1, grid=(NS // TB,),
        in_specs=[page_spec(tok, p) for tok in range(TB) for p in range(n_pages)] + [
            tok_spec(512), tok_spec(512), tok_spec(256), tok_spec(3 * 512), tok_spec(256),
            pl.BlockSpec((TB, 256, WINDOW), lambda i, pt: (i, 0, 0)),
            pl.BlockSpec((TB, 512, N_MEM), lambda i, pt: (i, 0, 0)),
            cst_spec((16, 256, 512)), cst_spec((8, 256)), cst_spec((1, LANES)), cst_spec((LANES, LANES)),
            cst_spec((3, LANES, LANES)), cst_spec((LANES, LANES)), cst_spec((LANES, T)),
            cst_spec((PAGE, PAGE))],
        out_specs=(tok_spec(512), tok_spec(256), pl.BlockSpec((TB, 256, WINDOW), lambda i, pt: (i, 0, 0))),
    )
    on_s, om_s, wo_s = pl.pallas_call(
        _attn_sample_kernel, grid_spec=grid_spec,
        out_shape=(jax.ShapeDtypeStruct((NS, 1, 512), F32), jax.ShapeDtypeStruct((NS, 1, 256), F32),
                   jax.ShapeDtypeStruct((NS, 256, WINDOW), F32)),
        compiler_params=_params("arbitrary"), name="attn_sample",
    )(pt_flat, *([pages] * (TB * n_pages)), q_s.reshape(NS, 1, 512), rows_s.reshape(NS, 1, 512),
      win_s.reshape(NS, 1, 256), gae_s.reshape(NS, 1, 3 * 512), qm_s.reshape(NS, 1, 256),
      cw_t, cm_t, wcat, pe_term, kn0, gmat, ctab, overlap, expand, perm)
    wo_s = wo_s.reshape(NS, 2, N_KV_A, HEAD_DIM, WINDOW).transpose(0, 4, 1, 2, 3)

    o_nsa_p = pl.pallas_call(
        _attn_prompt_kernel, grid=(B, T // TQ),
        in_specs=[pl.BlockSpec((1, TQ, 512), lambda b, t: (b, t, 0)),
                  pl.BlockSpec((1, T, 384), lambda b, t: (b, 0, 0)),
                  pl.BlockSpec((1, nt, 256, TM), lambda b, t: (b, 0, 0, 0)),
                  pl.BlockSpec((1, LANES, LANES), lambda b, t: (b, 0, 0)),
                  pl.BlockSpec((1, LANES, LANES), lambda b, t: (b, 0, 0)),
                  pl.BlockSpec((1, TQ, LANES), lambda b, t: (b, t, 0)),
                  _const((3 * 512, LANES)), _const((LANES, LANES))],
        out_specs=pl.BlockSpec((1, TQ, 512), lambda b, t: (b, t, 0)),
        out_shape=jax.ShapeDtypeStruct((B, T, 512), BF16),
        compiler_params=_params("arbitrary", "arbitrary"), name="attn_prompt",
    )(q_p, kk_p, vt_p, kc_p, vc_p, ga_p, egate_t, overlap.T)

    y_p, fst_p = pl.pallas_call(
        _post_prompt_kernel, grid=(B, nt),
        in_specs=[tile(D_MODEL), tile(512), tile(D_MODEL), tile(D_MODEL), _const((512, D_MODEL)),
                  _const((D_MODEL, D_MODEL)), _const((1, D_MODEL)), _const((D_MODEL, 2 * D_FF)),
                  _const((FFN_CONV_W, D_FF)), _const((1, D_FF)), _const((D_FF, D_MODEL))],
        out_specs=(tile(D_MODEL), pl.BlockSpec((1, FFN_CONV_W - 1, D_FF), lambda b, t: (b, 0, 0))),
        out_shape=(jax.ShapeDtypeStruct((B, T, D_MODEL), F32),
                   jax.ShapeDtypeStruct((B, FFN_CONV_W - 1, D_FF), F32)),
        scratch_shapes=[pltpu.VMEM((TM + 8, D_FF), F32)],
        compiler_params=_params("arbitrary", "arbitrary"), name="post_prompt",
    )(x_prompt, o_nsa_p, gm0_p, part_p, wonsa, wout, nf, wup, fcw, fcb, wdn)

    y_s, fst_s = pl.pallas_call(
        _post_sample_kernel,
        out_shape=(jax.ShapeDtypeStruct((NS, D_MODEL), F32),
                   jax.ShapeDtypeStruct((NS, FFN_CONV_W - 1, D_FF), F32)),
        compiler_params=pltpu.CompilerParams(vmem_limit_bytes=VMEM_LIMIT), name="post_sample",
    )(xs, on_s.reshape(NS, 512), om_s.reshape(NS, 256), gm_s, yb_s, wonsa, wom, wout, nf, wup, fcw, fcb, wdn,
      cache_ffn[0])

    return (y_p, y_s.reshape(NS, 1, D_MODEL),
            rows_p.reshape(1, B, T, 4, N_KV_A, HEAD_DIM), rows_s.reshape(1, NS, 1, 4, N_KV_A, HEAD_DIM),
            win_p.reshape(1, B, WINDOW, 2, N_KV_A, HEAD_DIM), wo_s[None],
            cst_p[None], cst_s[None], fst_p[None], fst_s[None],
            mem_kv.reshape(1, B, N_MEM, 2, N_HEADS_M, HEAD_DIM))
```

```python
import jax
import jax.numpy as jnp
from jax import lax
from jax.experimental import pallas as pl
from jax.experimental.pallas import tpu as pltpu

F32 = jnp.float32
BF16 = jnp.bfloat16

D_MODEL = 1024
HEAD_DIM = 64
N_HEADS_A = 8
N_KV_A = 2
GROUP_A = 4
CMP_BLK = 32
CMP_STRIDE = 16
SLC_BLK = 64
N_SEL = 16
WINDOW = 512
C_CONV = 512
CONV_W = 31
N_MEM = 256
N_HEADS_M = 4
D_FF = 2816
FFN_CONV_W = 3
ROPE_THETA = 500000.0
ROPE_DIM = 16
EPS = 1e-6
BIG = 1e9
NEG = -1e30
LOG2_E = 1.4426950408889634
PAGE = 128

N_QKV = 1280
N_GATES = 24
C_GLU = 0
C_QM = 1024
C_GM = 1280
N_REST = 4352

LANES = 128
VMEM_LIMIT = 56 * 1024 * 1024

TM = 256
TQ = 256
TB = 4
KB = 128
T_SLC = 32


def _dot(a, b):
    return jnp.dot(a, b, preferred_element_type=F32)


def _dot_t(a, b):
    return lax.dot_general(a, b, (((1,), (1,)), ((), ())), preferred_element_type=F32)


def _dot_hilo(x, m):
    hi = x.astype(BF16)
    lo = (x - hi.astype(F32)).astype(BF16)
    return _dot(hi, m) + _dot(lo, m)


def _rms(x, g):
    return x * lax.rsqrt(jnp.mean(x * x, axis=-1, keepdims=True) + EPS) * g


def _head_norm(blk, gain, gmat):
    ms = _dot_hilo(blk * blk, gmat)
    return blk * lax.rsqrt(ms + EPS) * gain


def _rope(blk, tab):
    c, s1, s2 = tab
    return blk * c + pltpu.roll(blk, 8, 1) * s1 + pltpu.roll(blk, LANES - 8, 1) * s2


def _softmax_rows(s):
    m = jnp.max(s, axis=-1, keepdims=True)
    p = jnp.exp(s - m)
    return p, jnp.sum(p, axis=-1, keepdims=True)


def _project_qkv(h, wa_ref, wg_ref, w_ref):
    return (_dot(h, wa_ref[:, 0:512]), _dot(h, wa_ref[:, 512:N_QKV]), _dot(h, wg_ref[...]),
            _dot(h, w_ref[:, C_GLU:C_GLU + 1024]))


def _project_rest(h, w_ref):
    return _dot(h, w_ref[:, C_QM:C_QM + 256]), _dot(h, w_ref[:, C_GM:C_GM + 3072])


def _in_proj_common(zq, zkv, zga, qn, kn1, kn2, gmat, tab, q_scale):
    q_blocks = []
    for cb in range(4):
        blk = zq[:, cb * LANES:(cb + 1) * LANES]
        q_blocks.append(_rope(_head_norm(blk, qn, gmat), tab) * q_scale)
    q = jnp.concatenate(q_blocks, axis=1)
    k_slc = _rope(_head_norm(zkv[:, 256:384], kn1, gmat), tab)
    k_win = _rope(_head_norm(zkv[:, 512:640], kn2, gmat), tab)
    rows = jnp.concatenate([zkv[:, 0:256], k_slc, zkv[:, 384:512]], axis=1)
    win = jnp.concatenate([k_win, zkv[:, 640:768]], axis=1)
    return q, rows, win, jax.nn.sigmoid(zga)


def _glu(z):
    return z[:, :C_CONV] * jax.nn.sigmoid(z[:, C_CONV:])


def _conv_tail(c, lng, lnb, wo_ref):
    mu = jnp.mean(c, axis=-1, keepdims=True)
    var = jnp.mean(jnp.square(c - mu), axis=-1, keepdims=True)
    y = (c - mu) * lax.rsqrt(var + EPS) * lng + lnb
    return _dot(jax.nn.silu(y).astype(BF16), wo_ref[...])


def _mem_q(z, mqn, gmat):
    return jnp.concatenate(
        [_head_norm(z[:, cb * LANES:(cb + 1) * LANES], mqn, gmat) for cb in range(2)], axis=1) * (HEAD_DIM ** -0.5)


def _ffn_tail(x1, nf, wup_ref, u_prev2, u_prev1_fn, fcw, fcb, wdn_ref):
    h2 = _rms(x1, nf).astype(BF16)
    up = _dot(h2, wup_ref[...])
    u = up[:, :D_FF]
    v = up[:, D_FF:]
    uc = fcw[0:1] * u_prev2(u) + fcw[1:2] * u_prev1_fn(u) + fcw[2:3] * u + fcb
    act = jax.nn.gelu(uc, approximate=True) * v
    return x1 + _dot(act.astype(BF16), wdn_ref[...]), u


def _memkv_kernel(mem_ref, nm_ref, w_ref, mkn_ref, gmat_ref, o_ref):
    h = _rms(mem_ref[0], nm_ref[...]).astype(BF16)
    z = _dot(h, w_ref[...])
    gmat = gmat_ref[...]
    k = [_head_norm(z[:, cb * LANES:(cb + 1) * LANES], mkn_ref[...], gmat) for cb in range(2)]
    o_ref[0] = jnp.concatenate(k + [z[:, 256:512]], axis=1)


def _pre_prompt_kernel(x_ref, na_ref, wa_ref, wg_ref, w_ref, qn_ref, kn1_ref, kn2_ref, gmat_ref, tab_ref, oh_ref,
                       cw_ref, cb_ref, lng_ref, lnb_ref, woc_ref, mkv_ref, mqn_ref, wom_ref,
                       q_ref, rows_ref, kk_ref, vt_ref, win_ref, ga_ref, gm0_ref, part_ref, cst_ref, hbuf, sbuf):
    @pl.when(pl.program_id(1) == 0)
    def _():
        hbuf[0:32, :] = jnp.zeros((32, C_CONV), F32)

    gmat = gmat_ref[...]
    tab = (tab_ref[0], tab_ref[1], tab_ref[2])
    h = _rms(x_ref[0], na_ref[...]).astype(BF16)
    zq, zkv, zga, zglu = _project_qkv(h, wa_ref, wg_ref, w_ref)
    zqm = _dot(h, w_ref[:, C_QM:C_QM + 256])

    glu = _glu(zglu)
    hbuf[32:32 + TM, :] = glu
    cw = cw_ref[...]
    c = jnp.zeros((TM, C_CONV), F32) + cb_ref[...]
    zgm_chunks = []
    for r in range(8):
        if r < 6:
            zgm_chunks.append(_dot(h, w_ref[:, C_GM + r * 512:C_GM + (r + 1) * 512]))
        if r > 0:
            sbuf[r - 1] = hbuf[pl.ds(r, TM + 24), :]
        for a in range(5):
            k = 8 * a + r - 2
            if 0 <= k < CONV_W:
                src = hbuf[8 * a:8 * a + TM, :] if r == 0 else sbuf[r - 1, 8 * a:8 * a + TM, :]
                c = c + cw[k:k + 1] * src
    zgm = jnp.concatenate(zgm_chunks, axis=1)
    cst_ref[0] = hbuf[pl.ds(TM + 2, CONV_W - 1), :]
    hbuf[0:32, :] = hbuf[TM:TM + 32, :]

    q, rows, win, ga = _in_proj_common(zq, zkv, zga, qn_ref[...], kn1_ref[...], kn2_ref[...], gmat, tab,
                                       HEAD_DIM ** -0.5 * LOG2_E)
    q_ref[0] = q.astype(BF16)
    rows_ref[0] = rows
    win_ref[0] = win
    ga_ref[0] = ga
    oh = oh_ref[...]
    kk_ref[0] = jnp.concatenate([rows[:, 256:320].astype(BF16), oh, rows[:, 320:384].astype(BF16), oh,
                                 win[:, 0:128].astype(BF16)], axis=1)
    vt_ref[0, 0] = jnp.concatenate([rows[:, 384:512], win[:, 128:256]], axis=1).T.astype(BF16)

    y_b = _conv_tail(c, lng_ref[...], lnb_ref[...], woc_ref)

    qm = _mem_q(zqm, mqn_ref[...], gmat).astype(BF16)
    mkv = mkv_ref[0].astype(BF16)
    scores = [_dot_t(qm[:, hh * 64:(hh + 1) * 64], mkv[:, hh * 64:(hh + 1) * 64]) for hh in range(N_HEADS_M)]
    probs = [_softmax_rows(s) for s in scores]
    heads = [_dot(p.astype(BF16), mkv[:, 256 + hh * 64:256 + (hh + 1) * 64]) / den
             for hh, (p, den) in enumerate(probs)]
    y_m = _dot(jnp.concatenate(heads, axis=1).astype(BF16), wom_ref[...])

    gm = jax.nn.sigmoid(zgm)
    gm0_ref[0] = gm[:, 0:1024].astype(BF16)
    part_ref[0] = (gm[:, 1024:2048] * y_b + gm[:, 2048:3072] * y_m).astype(BF16)


def _pe_term_kernel(pe_ref, wcat_ref, o_ref):
    acc = jnp.zeros((8, 256), F32)
    for l in range(16):
        top = jnp.broadcast_to(pe_ref[l:l + 1, :], (8, 256)).astype(BF16)
        bot = jnp.broadcast_to(pe_ref[l + 16:l + 17, :], (8, 256)).astype(BF16)
        acc = acc + _dot(top, wcat_ref[l][:, 0:256]) + _dot(bot, wcat_ref[l][:, 256:512])
    o_ref[...] = acc


def _compress(tap_fn, n_rows, wcat_ref, pe_term, kn0, gmat, ctab):
    acc = jnp.zeros((n_rows, 512), F32)
    for l in range(16):
        acc = acc + _dot(tap_fn(l).astype(BF16), wcat_ref[l])
    top = acc[:, 0:256]
    bot = pltpu.roll(acc[:, 256:512], n_rows - 1, 0)
    kcv = top + bot + pe_term
    kc = _rope(_head_norm(kcv[:, 0:128], kn0, gmat), ctab)
    return kc.astype(BF16), kcv[:, 128:256]


def _compress_prompt_kernel(kr_ref, vr_ref, wcat_ref, pet_ref, kn0_ref, gmat_ref, ctab_ref, kc_ref, vc_ref):
    def tap(l):
        return jnp.concatenate([r[0, pl.ds(l, 128, stride=16), :] for r in (kr_ref, vr_ref)], axis=1)

    kc, vc = _compress(tap, 128, wcat_ref, pet_ref[0:1, :],
                       kn0_ref[...], gmat_ref[...], (ctab_ref[0], ctab_ref[1], ctab_ref[2]))
    kc_ref[0] = kc
    vc_ref[0] = vc.T.astype(BF16)


def _select_cols(imp_t, tpos):
    n_slc = imp_t.shape[0]
    sidx = lax.broadcasted_iota(jnp.int32, imp_t.shape, 0)
    qblk = jnp.right_shift(tpos, 6)
    forced = (sidx == 0) | (sidx == qblk) | (sidx == qblk - 1)
    score = jnp.where(forced, BIG, jnp.where(sidx <= qblk, imp_t, -BIG))
    rank = jnp.zeros(imp_t.shape, F32)
    for s in range(n_slc):
        row = score[s:s + 1, :]
        tie = jnp.where(sidx > s, 1.0, 0.0)
        rank = rank + jnp.where(row > score, 1.0, jnp.where(row == score, tie, 0.0))
    return jnp.where((rank < N_SEL) & (score > -0.5 * BIG), 1.0, 0.0)


def _attn_prompt_kernel(q_ref, kk_ref, vt_ref, kc_ref, vct_ref, ga_ref, egt_ref, ovt_ref, o_ref):
    qt = pl.program_id(1)
    t0 = qt * TQ
    q = q_ref[0]
    tpos = t0 + lax.broadcasted_iota(jnp.int32, (1, TQ), 1)
    n_cmp = lax.broadcasted_iota(jnp.int32, (LANES, TQ), 0)
    cmask = jnp.where(((n_cmp * CMP_STRIDE + (CMP_BLK - 1)) <= tpos) & (n_cmp < 127), 1.0, 0.0)
    cbias = (cmask - 1.0) * (-NEG)
    cw0 = jnp.maximum(qt - 2, 0)
    w0 = pl.multiple_of(cw0 * TQ, TQ)
    wdiff = tpos - (w0 + lax.broadcasted_iota(jnp.int32, (3 * TQ, 1), 0))
    wbias = jnp.where((wdiff >= 0) & (wdiff <= WINDOW), 0.0, NEG)
    causal = jnp.where((t0 + lax.broadcasted_iota(jnp.int32, (TQ, 1), 0)) <= tpos, 0.0, NEG)
    ga_t = ga_ref[0].T
    ga_hi = ga_t.astype(BF16)
    ga_lo = (ga_t - ga_hi.astype(F32)).astype(BF16)
    gates_t = _dot(egt_ref[...], ga_hi) + _dot(egt_ref[...], ga_lo)

    def add4(s, b):
        return jnp.concatenate([s[:, j * TQ:(j + 1) * TQ] + b for j in range(GROUP_A)], axis=1)

    def with_ones(vt):
        return jnp.concatenate([vt, jnp.ones((16, vt.shape[1]), BF16)], axis=0)

    def blk_scores(k_rows, qh, add_bias):
        cols, maxs = [], []
        for j in range(GROUP_A):
            blocks = [add_bias(i, _dot_t(k_rows[i * KB:(i + 1) * KB, :], qh[j]))
                      for i in range(k_rows.shape[0] // KB)]
            mx = jnp.max(blocks[0], axis=0, keepdims=True)
            for b in blocks[1:]:
                mx = jnp.maximum(mx, jnp.max(b, axis=0, keepdims=True))
            cols.append(jnp.concatenate(blocks, axis=0))
            maxs.append(mx)
        return jnp.concatenate(cols, axis=1), jnp.concatenate(maxs, axis=1)

    def blk_pv(vt_blk, s, m):
        cols = []
        for j in range(GROUP_A):
            o = jnp.zeros((80, TQ), F32)
            for i in range(s.shape[0] // KB):
                p = jnp.exp2(s[i * KB:(i + 1) * KB, j * TQ:(j + 1) * TQ] - m[:, j * TQ:(j + 1) * TQ])
                o = o + _dot(vt_blk(i), p.astype(BF16))
            cols.append(o)
        return jnp.concatenate(cols, axis=1)

    groups = range(N_KV_A)
    qhs = [[q[:, (4 * g + j) * 64:(4 * g + j + 1) * 64] for j in range(GROUP_A)] for g in groups]
    s_cmps = [add4(_dot_t(kc_ref[0][:, g * 64:(g + 1) * 64], jnp.concatenate(qhs[g], axis=0)), cbias)
              for g in groups]
    wins = [[[_dot_t(kk_ref[0, pl.ds(w0 + i * KB, KB), 256 + g * 64:256 + (g + 1) * 64], qhs[g][j])
              + wbias[i * KB:(i + 1) * KB, :] for i in range(3 * TQ // KB)] for j in range(GROUP_A)]
            for g in groups]

    o_cmps, o_wins, qps = [], [], []
    for g in groups:
        lo, hi = g * 64, (g + 1) * 64
        s_cmp = s_cmps[g]
        p = jnp.exp2(s_cmp - jnp.max(s_cmp, axis=0, keepdims=True))
        p = jnp.concatenate([p[:, j * TQ:(j + 1) * TQ] * cmask for j in range(GROUP_A)], axis=1)
        den = jnp.sum(p, axis=0, keepdims=True)
        p = p / jnp.where(den > 0.0, den, 1.0)
        o_cmps.append(_dot(vct_ref[0][lo:hi, :], p.astype(BF16)))
        p4 = p[:, 0:TQ] + p[:, TQ:2 * TQ] + p[:, 2 * TQ:3 * TQ] + p[:, 3 * TQ:4 * TQ]
        p4_hi = p4.astype(BF16)
        p4_lo = (p4 - p4_hi.astype(F32)).astype(BF16)
        imp_t = _dot(ovt_ref[...], p4_hi) + _dot(ovt_ref[...], p4_lo)
        sel_bias = (_select_cols(imp_t[0:32, :], tpos) - 1.0) * (-NEG)
        bias_t = jnp.concatenate([sel_bias, jnp.zeros((LANES - T_SLC, TQ), F32)], axis=0).T
        bias_t = bias_t[:, 0:64].astype(BF16)
        qps.append([jnp.concatenate([qhs[g][j], bias_t], axis=1) for j in range(GROUP_A)])

    for g in groups:
        cols = []
        for j in range(GROUP_A):
            m_i = jnp.full((1, TQ), NEG, F32)
            acc = jnp.zeros((80, TQ), F32)
            for i in range(3 * TQ // KB):
                s = wins[g][j][i]
                m_new = jnp.maximum(m_i, jnp.max(s, axis=0, keepdims=True))
                p = jnp.exp2(s - m_new).astype(BF16)
                vt = with_ones(vt_ref[0, cw0 + i // 2, 128 + g * 64:128 + (g + 1) * 64,
                                      (i % 2) * KB:(i % 2 + 1) * KB])
                acc = jnp.exp2(m_i - m_new) * acc + _dot(vt, p)
                m_i = m_new
            cols.append(acc[0:64, :] / acc[64:65, :])
        o_wins.append(jnp.concatenate(cols, axis=1))

    def step(c, carry, diagonal):
        k0 = pl.multiple_of(c * TQ, TQ)
        scored = []
        for g in groups:
            def add_bias(i, blk):
                return blk + causal[i * KB:(i + 1) * KB, :] if diagonal else blk

            scored.append(blk_scores(kk_ref[0, pl.ds(k0, TQ), g * LANES:(g + 1) * LANES], qps[g], add_bias))
        new = []
        for g in groups:
            s, smax = scored[g]
            m_i, acc = carry[g]
            m_new = jnp.maximum(m_i, smax)
            pv = blk_pv(lambda i: with_ones(vt_ref[0, c, g * 64:(g + 1) * 64, i * KB:(i + 1) * KB]), s, m_new)
            new.append((m_new, jnp.exp2(m_i - m_new) * acc + pv))
        return tuple(new)

    units = [(i, g, j) for i in range(TQ // KB) for g in groups for j in range(GROUP_A)]

    def loop_step(c, carry):
        k0 = pl.multiple_of(c * TQ, TQ)
        ms = [[carry[g][0][:, j * TQ:(j + 1) * TQ] for j in range(GROUP_A)] for g in groups]
        accs = [[carry[g][1][:, j * TQ:(j + 1) * TQ] for j in range(GROUP_A)] for g in groups]
        scores = [_dot_t(kk_ref[0, pl.ds(k0 + i * KB, KB), g * LANES:(g + 1) * LANES], qps[g][j])
                  for i, g, j in units]
        for s, (i, g, j) in zip(scores, units):
            m_new = jnp.maximum(ms[g][j], jnp.max(s, axis=0, keepdims=True))
            p = jnp.exp2(s - m_new).astype(BF16)
            vt = with_ones(vt_ref[0, c, g * 64:(g + 1) * 64, i * KB:(i + 1) * KB])
            accs[g][j] = jnp.exp2(ms[g][j] - m_new) * accs[g][j] + _dot(vt, p)
            ms[g][j] = m_new
        return tuple((jnp.concatenate(ms[g], axis=1), jnp.concatenate(accs[g], axis=1)) for g in groups)

    init = tuple((jnp.full((1, GROUP_A * TQ), NEG, F32), jnp.zeros((80, GROUP_A * TQ), F32)) for _ in groups)
    final = step(qt, lax.fori_loop(0, qt, loop_step, init), True)

    outs = [[], [], []]
    for g in groups:
        acc = final[g][1]
        o_slc = acc[0:64, :] / acc[64:65, :]
        for c, o in enumerate((o_cmps[g], o_slc, o_wins[g])):
            outs[c] += [o[:, j * TQ:(j + 1) * TQ] for j in range(GROUP_A)]

    o_t = jnp.zeros((512, TQ), F32)
    for c in range(3):
        o_t = o_t + gates_t[c * 512:(c + 1) * 512, :] * jnp.concatenate(outs[c], axis=0)
    o_ref[0] = o_t.T.astype(BF16)


def _post_prompt_kernel(x_ref, on_ref, gm0_ref, part_ref, wonsa_ref, wout_ref, nf_ref, wup_ref, fcw_ref,
                        fcb_ref, wdn_ref, y_ref, fst_ref, ubuf):
    @pl.when(pl.program_id(1) == 0)
    def _():
        ubuf[0:8, :] = jnp.zeros((8, D_FF), F32)

    y_a = _dot(on_ref[0], wonsa_ref[...])
    merged = gm0_ref[0].astype(F32) * y_a + part_ref[0].astype(F32)
    x1 = x_ref[0] + _dot(merged.astype(BF16), wout_ref[...])

    def prev2(u):
        ubuf[8:8 + TM, :] = u
        return ubuf[pl.ds(6, TM), :]

    def prev1(u):
        return ubuf[pl.ds(7, TM), :]

    y, u = _ffn_tail(x1, nf_ref[...], wup_ref, prev2, prev1, fcw_ref[...], fcb_ref[...], wdn_ref)
    y_ref[0] = y
    fst_ref[0] = u[TM - 2:TM, :]
    ubuf[0:8, :] = ubuf[TM:TM + 8, :]


def _pre_sample_kernel(x_ref, na_ref, wa_ref, wg_ref, w_ref, qn_ref, kn1_ref, kn2_ref, gmat_ref, tab_ref,
                       cw_ref, cb_ref, lng_ref, lnb_ref, woc_ref, cc_ref, mqn_ref, eg_ref,
                       q_ref, rows_ref, win_ref, gae_ref, qm_ref, gm_ref, yb_ref, cst_ref):
    gmat = gmat_ref[...]
    tab = (tab_ref[0], tab_ref[1], tab_ref[2])
    h = _rms(x_ref[...], na_ref[...]).astype(BF16)
    zq, zkv, zga, zglu = _project_qkv(h, wa_ref, wg_ref, w_ref)
    q, rows, win, ga = _in_proj_common(zq, zkv, zga, qn_ref[...], kn1_ref[...], kn2_ref[...], gmat, tab,
                                       HEAD_DIM ** -0.5)
    zqm, zgm = _project_rest(h, w_ref)
    q_ref[...] = q
    rows_ref[...] = rows
    win_ref[...] = win
    gae_ref[...] = _dot_hilo(ga, eg_ref[...])
    glu = _glu(zglu)
    cw = cw_ref[...]
    c = cw[CONV_W - 1:CONV_W] * glu + cb_ref[...]
    for k in range(CONV_W - 1):
        c = c + cw[k:k + 1] * cc_ref[k]
    for k in range(CONV_W - 2):
        cst_ref[k] = cc_ref[k + 1]
    cst_ref[CONV_W - 2] = glu
    yb_ref[...] = _conv_tail(c, lng_ref[...], lnb_ref[...], woc_ref)
    qm_ref[...] = _mem_q(zqm, mqn_ref[...], gmat)
    gm_ref[...] = jax.nn.sigmoid(zgm)


def _heads_to_lanes(o8):
    lane = lax.broadcasted_iota(jnp.int32, (1, LANES), 1)
    blocks = []
    for cb in range(4):
        a = o8[2 * cb:2 * cb + 1, :]
        b = o8[2 * cb + 1:2 * cb + 2, :]
        if cb // 2 == 1:
            a = pltpu.roll(a, 64, 1)
        else:
            b = pltpu.roll(b, 64, 1)
        blocks.append(jnp.where(lane < 64, a, b))
    return jnp.concatenate(blocks, axis=1)


def _attn_sample_kernel(pt_ref, pages_hbm, q_ref, rn_ref, wn_ref, gae_ref, qm_ref, cw_ref, cm_ref, wcat_ref, pet_ref,
                        kn0_ref, gmat_ref, ctab_ref, ov_ref, ex_ref, pm_ref, on_ref, om_ref, wo_ref, page_buf, sem):
    step = pl.program_id(0)
    slot = lax.rem(step, 2)

    def page_copy(s, k, into):
        return pltpu.make_async_copy(pages_hbm.at[pt_ref[s * (TB * 16) + k]], page_buf.at[into, k], sem.at[into])

    @pl.when(step == 0)
    def _():
        for k in range(TB * 16):
            page_copy(0, k, 0).start()

    @pl.when(step + 1 < pl.num_programs(0))
    def _():
        for k in range(TB * 16):
            page_copy(step + 1, k, 1 - slot).start()

    for k in range(TB * 16):
        page_copy(step, k, slot).wait()

    lane = lax.broadcasted_iota(jnp.int32, (1, LANES), 1)
    row8 = lax.broadcasted_iota(jnp.int32, (8, LANES), 0)
    lane8 = lax.broadcasted_iota(jnp.int32, (8, LANES), 1)

    pm = pm_ref[...]
    taps = [_dot_t(pm, page_buf[slot, i, 0:256, :].astype(BF16)) for i in range(TB * 16)]

    def tap(l):
        return jnp.concatenate([taps[i][l * 8:(l + 1) * 8, :] for i in range(TB * 16)], axis=0)

    kc_all, vc_all = _compress(tap, TB * 128, wcat_ref, pet_ref[0:1, :], kn0_ref[...], gmat_ref[...],
                               tuple(jnp.concatenate([ctab_ref[i]] * TB, axis=0) for i in range(3)))

    toks = range(TB)
    rowm = lax.broadcasted_iota(jnp.int32, (8, 256), 0)
    headm = jnp.right_shift(lax.broadcasted_iota(jnp.int32, (8, 256), 1), 6)

    def bf(x):
        return x.astype(BF16).astype(F32)

    q8s, s_slc, s_win, s_mem = [], [], [], []
    for tok in toks:
        q = q_ref[tok]
        q_rows = []
        for r in range(N_HEADS_A):
            piece = q[:, (r // 2) * LANES:(r // 2 + 1) * LANES]
            if (r % 2) != (r // 4):
                piece = pltpu.roll(piece, 64, 1)
            q_rows.append(jnp.where(jnp.right_shift(lane, 6) == (r // 4), piece, 0.0))
        q8 = jnp.concatenate(q_rows, axis=0).astype(BF16)
        q8s.append(q8)
        s_slc.append(jnp.concatenate(
            [_dot(q8, page_buf[slot, tok * 16 + i, 256:384, :].astype(BF16)) for i in range(16)], axis=1))
        s_win.append(_dot(q8, cw_ref[tok, 0:128, :].astype(BF16)))
        qm8 = jnp.where(rowm == headm, jnp.broadcast_to(qm_ref[tok], (8, 256)), 0.0).astype(BF16)
        s_mem.append(_dot(qm8, cm_ref[tok, 0:256, :].astype(BF16)))

    o_cmp, imp8 = [], []
    for tok in toks:
        s = jnp.where(lane8 < 127, _dot_t(q8s[tok], kc_all[tok * 128:(tok + 1) * 128]), NEG)
        p, den = _softmax_rows(s)
        p = p / den
        o_cmp.append(_dot(p.astype(BF16), vc_all[tok * 128:(tok + 1) * 128].astype(BF16)))
        imp8.append(_dot_hilo(p, ov_ref[...]))

    o_win = []
    for tok in toks:
        wn = wn_ref[tok]
        s = s_win[tok]
        s_new = jnp.sum(q8s[tok].astype(F32) * bf(wn[:, 0:128]), axis=-1, keepdims=True)
        m = jnp.maximum(jnp.max(s, axis=-1, keepdims=True), s_new)
        p = jnp.exp(s - m)
        p_new = jnp.exp(s_new - m)
        den = jnp.sum(p, axis=-1, keepdims=True) + p_new
        o_win.append((_dot_t(p.astype(BF16), cw_ref[tok, 128:256, :].astype(BF16))
                      + bf(p_new) * bf(wn[:, 128:256])) / den)
        p, den = _softmax_rows(s_mem[tok])
        o8 = _dot_t(p.astype(BF16), cm_ref[tok, 256:512, :].astype(BF16)) / den
        om_ref[tok] = jnp.sum(jnp.where(rowm == headm, o8, 0.0), axis=0, keepdims=True)
        cw = cw_ref[tok]
        rolled = pltpu.roll(cw, WINDOW - 1, 1)
        last = lax.broadcasted_iota(jnp.int32, (1, WINDOW), 1) == WINDOW - 1
        wn_col = jnp.broadcast_to(wn, (LANES, 256)).T[:, 0:1]
        wo_ref[tok] = jnp.where(last, wn_col, rolled)

    selx = []
    for tok in toks:
        sel_rows = []
        for g in range(N_KV_A):
            imp = jnp.sum(imp8[tok][4 * g:4 * g + 4, :], axis=0, keepdims=True)
            forced = (lane == 0) | (lane == 31) | (lane == 32)
            score = jnp.where(lane < 33, jnp.where(forced, BIG, imp), -3.0 * BIG)
            a = jnp.broadcast_to(score, (LANES, LANES))
            b = a.T
            sub = lax.broadcasted_iota(jnp.int32, (LANES, LANES), 0)
            ln = lax.broadcasted_iota(jnp.int32, (LANES, LANES), 1)
            beats = (b > a) | ((b == a) & (sub < ln))
            rank = jnp.sum(jnp.where(beats, 1.0, 0.0), axis=0, keepdims=True)
            sel_rows.append(jnp.where((rank < N_SEL) & (lane < 33), 1.0, 0.0))
        sel8 = jnp.where(row8 < 4, sel_rows[0], sel_rows[1]).astype(BF16)
        selx.append(_dot(sel8, ex_ref[...]))

    for tok in toks:
        rn = rn_ref[tok]
        gae = gae_ref[tok]
        s = jnp.where(selx[tok] > 0.5, s_slc[tok], NEG)
        s_new = jnp.sum(q8s[tok].astype(F32) * bf(rn[:, 256:384]), axis=-1, keepdims=True)
        m = jnp.maximum(jnp.max(s, axis=-1, keepdims=True), s_new)
        p = jnp.exp(s - m)
        p_new = jnp.exp(s_new - m)
        den = jnp.sum(p, axis=-1, keepdims=True) + p_new
        o_slc = bf(p_new) * bf(rn[:, 384:512])
        for i in range(16):
            o_slc = o_slc + _dot_t(p[:, i * PAGE:(i + 1) * PAGE].astype(BF16),
                                   page_buf[slot, tok * 16 + i, 384:512, :].astype(BF16))
        o_slc = o_slc / den
        on_ref[tok] = (gae[:, 0:512] * _heads_to_lanes(o_cmp[tok]) + gae[:, 512:1024] * _heads_to_lanes(o_slc)
                       + gae[:, 1024:1536] * _heads_to_lanes(o_win[tok]))


def _post_sample_kernel(x_ref, on_ref, om_ref, gm_ref, yb_ref, wonsa_ref, wom_ref, wout_ref, nf_ref, wup_ref,
                        fcw_ref, fcb_ref, wdn_ref, cf_ref, y_ref, fst_ref):
    y_a = _dot(on_ref[...].astype(BF16), wonsa_ref[...])
    y_m = _dot(om_ref[...].astype(BF16), wom_ref[...])
    gm = gm_ref[...]
    merged = gm[:, 0:1024] * y_a + gm[:, 1024:2048] * yb_ref[...] + gm[:, 2048:3072] * y_m
    x1 = x_ref[...] + _dot(merged.astype(BF16), wout_ref[...])
    y, u = _ffn_tail(x1, nf_ref[...], wup_ref, lambda u: cf_ref[:, 0, :], lambda u: cf_ref[:, 1, :],
                     fcw_ref[...], fcb_ref[...], wdn_ref)
    y_ref[...] = y
    fst_ref[:, 0, :] = cf_ref[:, 1, :]
    fst_ref[:, 1, :] = u


def _rope_tables(pos):
    inv = ROPE_THETA ** (-jnp.arange(0, ROPE_DIM, 2, dtype=F32) / ROPE_DIM)
    ang = pos.astype(F32)[:, None] * inv
    cos, sin = jnp.cos(ang), jnp.sin(ang)
    n = pos.shape[0]
    one = jnp.ones((n, HEAD_DIM - ROPE_DIM), F32)
    z8 = jnp.zeros((n, 8), F32)
    z48 = jnp.zeros((n, HEAD_DIM - ROPE_DIM), F32)
    c = jnp.concatenate([cos, cos, one], axis=1)
    s1 = jnp.concatenate([z8, sin, z48], axis=1)
    s2 = jnp.concatenate([-sin, z8, z48], axis=1)
    return jnp.stack([jnp.tile(c, (1, 2)), jnp.tile(s1, (1, 2)), jnp.tile(s2, (1, 2))])


def _tile2(v):
    return jnp.tile(v.reshape(1, HEAD_DIM), (1, 2))


def _const(shape):
    nd = len(shape)
    return pl.BlockSpec(shape, lambda *_: (0,) * nd, pipeline_mode=pl.Buffered(1))


def _params(*sem):
    return pltpu.CompilerParams(dimension_semantics=sem, vmem_limit_bytes=VMEM_LIMIT)


def kernel(x_prompt, x_sample, cache_nsa, cache_win, cache_conv, cache_ffn, cache_mem, page_table, mem_prompt,
           norm_attn, w_in, q_norm, k_norm, cmp_pe, w_cmp, w_o_nsa, conv_w, conv_b, conv_ln_g, conv_ln_b, w_o_conv,
           norm_mem, w_mem_kv, mq_norm, mk_norm, w_o_mem, w_out, norm_ffn, w_ffn_up, ffn_conv_w, ffn_conv_b,
           w_ffn_down):
    B, T, _ = x_prompt.shape
    NS = x_sample.shape[0]
    n_pages = page_table.shape[1]
    assert w_in.shape[0] == 1 and T == 2048 and n_pages * PAGE == 2048 and cache_win.shape[2] == WINDOW
    nt = T // TM
    assert TQ == TM

    w_in0 = w_in[0]
    w_qkv = w_in0[:, :N_QKV].astype(BF16)
    w_gate = jnp.pad(w_in0[:, N_QKV:N_QKV + N_GATES], ((0, 0), (0, LANES - N_GATES))).astype(BF16)
    w_rest = w_in0[:, N_QKV + N_GATES:].astype(BF16)
    na = norm_attn.reshape(1, D_MODEL)
    nf = norm_ffn.reshape(1, D_MODEL)
    nm = norm_mem.reshape(1, D_MODEL)
    qn, mqn, mkn = _tile2(q_norm[0]), _tile2(mq_norm[0]), _tile2(mk_norm[0])
    kn0, kn1, kn2 = _tile2(k_norm[0, 0]), _tile2(k_norm[0, 1]), _tile2(k_norm[0, 2])
    ii = jnp.arange(LANES)
    gmat = jnp.where((ii[:, None] // 64) == (ii[None, :] // 64), 1.0 / 64, 0.0).astype(BF16)
    tab_p = _rope_tables(jnp.arange(T))
    tab_s = _rope_tables(jnp.full((1,), n_pages * PAGE))
    ctab = _rope_tables(jnp.arange(128) * CMP_STRIDE + (CMP_BLK - 1))
    wk, wv = w_cmp[0, 0].astype(BF16), w_cmp[0, 1].astype(BF16)
    z = jnp.zeros((CMP_BLK, HEAD_DIM, HEAD_DIM), BF16)
    w_l = jnp.concatenate([jnp.concatenate(r, axis=-1) for r in
                           ([wk, z, z, z], [z, wk, z, z], [z, z, wv, z], [z, z, z, wv])], axis=1)
    wcat = jnp.concatenate([w_l[:16], w_l[16:]], axis=-1)
    pe2 = jnp.broadcast_to(cmp_pe[0].transpose(1, 0, 2)[:, :, None, :], (CMP_BLK, 2, 2, HEAD_DIM)).reshape(
        CMP_BLK, 256)
    col = jnp.arange(3 * 512)
    egate = (jnp.arange(LANES)[:, None] == ((col // 512) * 8 + (col % 512) // 64)[None, :]).astype(BF16)
    egate_t = egate.T
    cs = jnp.arange(LANES)[:, None] * CMP_STRIDE
    ss = jnp.arange(LANES)[None, :] * SLC_BLK
    overlap = ((cs < ss + SLC_BLK) & (cs + CMP_BLK > ss) & (jnp.arange(LANES)[:, None] < 127)
               & (jnp.arange(LANES)[None, :] < 33)).astype(BF16)
    kpos = jnp.arange(T)
    expand = ((jnp.arange(LANES)[:, None] == (kpos // SLC_BLK)[None, :])).astype(BF16)
    blk_onehot = expand[0:64, :].T
    woc = w_o_conv[0].astype(BF16)
    wom = w_o_mem[0].astype(BF16)
    wonsa = w_o_nsa[0].astype(BF16)
    wout = w_out[0].astype(BF16)
    wup = w_ffn_up[0].astype(BF16)
    wdn = w_ffn_down[0].astype(BF16)
    wmkv = w_mem_kv[0].astype(BF16)
    cw, cb = conv_w[0], conv_b.reshape(1, C_CONV)
    lng, lnb = conv_ln_g.reshape(1, C_CONV), conv_ln_b.reshape(1, C_CONV)
    fcw, fcb = ffn_conv_w[0], ffn_conv_b.reshape(1, D_FF)

    mem_kv = pl.pallas_call(
        _memkv_kernel, grid=(B,),
        in_specs=[pl.BlockSpec((1, N_MEM, D_MODEL), lambda b: (b, 0, 0)), _const((1, D_MODEL)),
                  _const((D_MODEL, 512)), _const((1, LANES)), _const((LANES, LANES))],
        out_specs=pl.BlockSpec((1, N_MEM, 512), lambda b: (b, 0, 0)),
        out_shape=jax.ShapeDtypeStruct((B, N_MEM, 512), F32),
        compiler_params=_params("arbitrary"), name="mem_kv",
    )(mem_prompt, nm, wmkv, mkn, gmat)

    def tile(width):
        return pl.BlockSpec((1, TM, width), lambda b, t: (b, t, 0))

    n_win_t = WINDOW // TM
    pre_out_shapes = (
        jax.ShapeDtypeStruct((B, T, 512), BF16),
        jax.ShapeDtypeStruct((B, T, 512), F32),
        jax.ShapeDtypeStruct((B, T, 384), BF16),
        jax.ShapeDtypeStruct((B, nt, 256, TM), BF16),
        jax.ShapeDtypeStruct((B, WINDOW, 256), F32),
        jax.ShapeDtypeStruct((B, T, LANES), F32),
        jax.ShapeDtypeStruct((B, T, D_MODEL), BF16),
        jax.ShapeDtypeStruct((B, T, D_MODEL), BF16),
        jax.ShapeDtypeStruct((B, CONV_W - 1, C_CONV), F32),
    )
    q_p, rows_p, kk_p, vt_p, win_p, ga_p, gm0_p, part_p, cst_p = pl.pallas_call(
        _pre_prompt_kernel, grid=(B, nt),
        in_specs=[tile(D_MODEL), _const((1, D_MODEL)), _const((D_MODEL, N_QKV)), _const((D_MODEL, LANES)),
                  _const((D_MODEL, N_REST)), _const((1, LANES)),
                  _const((1, LANES)), _const((1, LANES)), _const((LANES, LANES)),
                  pl.BlockSpec((3, TM, LANES), lambda b, t: (0, t, 0)),
                  pl.BlockSpec((TM, 64), lambda b, t: (t, 0)),
                  _const((CONV_W, C_CONV)), _const((1, C_CONV)), _const((1, C_CONV)), _const((1, C_CONV)),
                  _const((C_CONV, D_MODEL)),
                  pl.BlockSpec((1, N_MEM, 512), lambda b, t: (b, 0, 0)), _const((1, LANES)),
                  _const((256, D_MODEL))],
        out_specs=(tile(512), tile(512), tile(384),
                   pl.BlockSpec((1, 1, 256, TM), lambda b, t: (b, t, 0, 0)),
                   pl.BlockSpec((1, TM, 256), lambda b, t: (b, jnp.maximum(t - (nt - n_win_t), 0), 0)),
                   tile(LANES), tile(D_MODEL), tile(D_MODEL),
                   pl.BlockSpec((1, CONV_W - 1, C_CONV), lambda b, t: (b, 0, 0))),
        out_shape=pre_out_shapes,
        scratch_shapes=[pltpu.VMEM((TM + 32, C_CONV), F32), pltpu.VMEM((7, TM + 24, C_CONV), F32)],
        compiler_params=_params("arbitrary", "arbitrary"), name="pre_prompt",
    )(x_prompt, na, w_qkv, w_gate, w_rest, qn, kn1, kn2, gmat, tab_p, blk_onehot, cw, cb, lng, lnb, woc, mem_kv,
      mqn, wom)

    pe_term = pl.pallas_call(
        _pe_term_kernel, out_shape=jax.ShapeDtypeStruct((8, 256), F32), name="pe_term",
    )(pe2, wcat)
    kc_p, vc_p = pl.pallas_call(
        _compress_prompt_kernel, grid=(B,),
        in_specs=[pl.BlockSpec((1, T, LANES), lambda b: (b, 0, 0)), pl.BlockSpec((1, T, LANES), lambda b: (b, 0, 1)),
                  _const((16, 256, 512)), _const((8, 256)),
                  _const((1, LANES)), _const((LANES, LANES)), _const((3, LANES, LANES))],
        out_specs=(pl.BlockSpec((1, LANES, LANES), lambda b: (b, 0, 0)),) * 2,
        out_shape=(jax.ShapeDtypeStruct((B, LANES, LANES), BF16),) * 2,
        compiler_params=_params("arbitrary"), name="compress_prompt",
    )(rows_p, rows_p, wcat, pe_term, kn0, gmat, ctab)

    xs = x_sample.reshape(NS, D_MODEL)
    pre_s_shapes = (
        jax.ShapeDtypeStruct((NS, 512), F32), jax.ShapeDtypeStruct((NS, 512), F32),
        jax.ShapeDtypeStruct((NS, 256), F32), jax.ShapeDtypeStruct((NS, 3 * 512), F32),
        jax.ShapeDtypeStruct((NS, 256), F32), jax.ShapeDtypeStruct((NS, 3 * D_MODEL), F32),
        jax.ShapeDtypeStruct((NS, D_MODEL), F32), jax.ShapeDtypeStruct((CONV_W - 1, NS, C_CONV), F32),
    )
    q_s, rows_s, win_s, gae_s, qm_s, gm_s, yb_s, cst_s = pl.pallas_call(
        _pre_sample_kernel, out_shape=pre_s_shapes,
        compiler_params=pltpu.CompilerParams(vmem_limit_bytes=VMEM_LIMIT), name="pre_sample",
    )(xs, na, w_qkv, w_gate, w_rest, qn, kn1, kn2, gmat, tab_s, cw, cb, lng, lnb, woc,
      cache_conv[0].transpose(1, 0, 2), mqn, egate)
    cst_s = cst_s.transpose(1, 0, 2)

    pages = cache_nsa[0].transpose(0, 2, 3, 4, 1).reshape(cache_nsa.shape[1], 512, PAGE)
    cw_t = cache_win[0].transpose(0, 2, 3, 4, 1).reshape(NS, 256, WINDOW)
    cm_t = cache_mem[0].transpose(0, 2, 3, 4, 1).reshape(NS, 512, N_MEM)
    pt_flat = page_table.reshape(-1)

    rr = jnp.arange(PAGE)
    perm = (rr[None, :] == ((rr % 8) * CMP_STRIDE + rr // 8)[:, None]).astype(BF16)

    def tok_spec(width):
        return pl.BlockSpec((TB, 1, width), lambda i, pt: (i, 0, 0))

    def cst_spec(shape):
        nd = len(shape)
        return pl.BlockSpec(shape, lambda i, pt: (0,) * nd, pipeline_mode=pl.Buffered(1))

    grid_spec = pltpu.PrefetchScalarGridSpec(
        num_scalar_prefetch=1, grid=(NS // TB,),
        in_specs=[pl.BlockSpec(memory_space=pl.ANY),
            tok_spec(512), tok_spec(512), tok_spec(256), tok_spec(3 * 512), tok_spec(256),
            pl.BlockSpec((TB, 256, WINDOW), lambda i, pt: (i, 0, 0)),
            pl.BlockSpec((TB, 512, N_MEM), lambda i, pt: (i, 0, 0)),
            cst_spec((16, 256, 512)), cst_spec((8, 256)), cst_spec((1, LANES)), cst_spec((LANES, LANES)),
            cst_spec((3, LANES, LANES)), cst_spec((LANES, LANES)), cst_spec((LANES, T)),
            cst_spec((PAGE, PAGE))],
        out_specs=(tok_spec(512), tok_spec(256), pl.BlockSpec((TB, 256, WINDOW), lambda i, pt: (i, 0, 0))),
        scratch_shapes=[pltpu.VMEM((2, TB * n_pages, 512, PAGE), F32), pltpu.SemaphoreType.DMA((2,))],
    )
    on_s, om_s, wo_s = pl.pallas_call(
        _attn_sample_kernel, grid_spec=grid_spec,
        out_shape=(jax.ShapeDtypeStruct((NS, 1, 512), F32), jax.ShapeDtypeStruct((NS, 1, 256), F32),
                   jax.ShapeDtypeStruct((NS, 256, WINDOW), F32)),
        compiler_params=_params("arbitrary"), name="attn_sample",
    )(pt_flat, pages, q_s.reshape(NS, 1, 512), rows_s.reshape(NS, 1, 512),
      win_s.reshape(NS, 1, 256), gae_s.reshape(NS, 1, 3 * 512), qm_s.reshape(NS, 1, 256),
      cw_t, cm_t, wcat, pe_term, kn0, gmat, ctab, overlap, expand, perm)
    wo_s = wo_s.reshape(NS, 2, N_KV_A, HEAD_DIM, WINDOW).transpose(0, 4, 1, 2, 3)

    o_nsa_p = pl.pallas_call(
        _attn_prompt_kernel, grid=(B, T // TQ),
        in_specs=[pl.BlockSpec((1, TQ, 512), lambda b, t: (b, t, 0)),
                  pl.BlockSpec((1, T, 384), lambda b, t: (b, 0, 0)),
                  pl.BlockSpec((1, nt, 256, TM), lambda b, t: (b, 0, 0, 0)),
                  pl.BlockSpec((1, LANES, LANES), lambda b, t: (b, 0, 0)),
                  pl.BlockSpec((1, LANES, LANES), lambda b, t: (b, 0, 0)),
                  pl.BlockSpec((1, TQ, LANES), lambda b, t: (b, t, 0)),
                  _const((3 * 512, LANES)), _const((LANES, LANES))],
        out_specs=pl.BlockSpec((1, TQ, 512), lambda b, t: (b, t, 0)),
        out_shape=jax.ShapeDtypeStruct((B, T, 512), BF16),
        compiler_params=_params("arbitrary", "arbitrary"), name="attn_prompt",
    )(q_p, kk_p, vt_p, kc_p, vc_p, ga_p, egate_t, overlap.T)

    y_p, fst_p = pl.pallas_call(
        _post_prompt_kernel, grid=(B, nt),
        in_specs=[tile(D_MODEL), tile(512), tile(D_MODEL), tile(D_MODEL), _const((512, D_MODEL)),
                  _const((D_MODEL, D_MODEL)), _const((1, D_MODEL)), _const((D_MODEL, 2 * D_FF)),
                  _const((FFN_CONV_W, D_FF)), _const((1, D_FF)), _const((D_FF, D_MODEL))],
        out_specs=(tile(D_MODEL), pl.BlockSpec((1, FFN_CONV_W - 1, D_FF), lambda b, t: (b, 0, 0))),
        out_shape=(jax.ShapeDtypeStruct((B, T, D_MODEL), F32),
                   jax.ShapeDtypeStruct((B, FFN_CONV_W - 1, D_FF), F32)),
        scratch_shapes=[pltpu.VMEM((TM + 8, D_FF), F32)],
        compiler_params=_params("arbitrary", "arbitrary"), name="post_prompt",
    )(x_prompt, o_nsa_p, gm0_p, part_p, wonsa, wout, nf, wup, fcw, fcb, wdn)

    y_s, fst_s = pl.pallas_call(
        _post_sample_kernel,
        out_shape=(jax.ShapeDtypeStruct((NS, D_MODEL), F32),
                   jax.ShapeDtypeStruct((NS, FFN_CONV_W - 1, D_FF), F32)),
        compiler_params=pltpu.CompilerParams(vmem_limit_bytes=VMEM_LIMIT), name="post_sample",
    )(xs, on_s.reshape(NS, 512), om_s.reshape(NS, 256), gm_s, yb_s, wonsa, wom, wout, nf, wup, fcw, fcb, wdn,
      cache_ffn[0])

    return (y_p, y_s.reshape(NS, 1, D_MODEL),
            rows_p.reshape(1, B, T, 4, N_KV_A, HEAD_DIM), rows_s.reshape(1, NS, 1, 4, N_KV_A, HEAD_DIM),
            win_p.reshape(1, B, WINDOW, 2, N_KV_A, HEAD_DIM), wo_s[None],
            cst_p[None], cst_s[None], fst_p[None], fst_s[None],
            mem_kv.reshape(1, B, N_MEM, 2, N_HEADS_M, HEAD_DIM))
```
